```python
import math
import jax, jax.numpy as jnp
from jax import lax
import numpy as np

D_MODEL = 1024
BATCH = 32
SEQ = 256
DEPTH = 2
DEC_BATCH = 4
DEC_SEQ = 4096
PAST_LEN = 512

GRID_W = 64
MIX_DIM = D_MODEL
MLA_HEADS = 8
QK_NOPE = 64
QK_ROPE = 32
QK_DIM = QK_NOPE + QK_ROPE
V_DIM = 64
MLA_OUT = MLA_HEADS * V_DIM
Q_RANK = 192
KV_RANK = 128
SC_DIM = MIX_DIM // 4
SC_WIDTH = 3
CF_DIM = MIX_DIM // 4
CF_WIDTH = 31
N_EXPERTS = 32
TOP_K = 4
D_EXPERT = D_MODEL
SWIGLU_LIMIT = 7.0
SWIGLU_ALPHA = 1.702
ROPE_BASE = 10000.0
ROPE_AXIS_DIM = QK_ROPE // 2
Q_BLOCK = 128
MOE_BLOCK = 128
EPS = 1e-6
DEEPNORM_ALPHA = (2 * DEPTH) ** 0.25
DEEPNORM_BETA = (8 * DEPTH) ** -0.25
IN_SPLITS = (Q_RANK,
             Q_RANK + KV_RANK,
             Q_RANK + KV_RANK + QK_ROPE,
             Q_RANK + KV_RANK + QK_ROPE + SC_DIM,
             Q_RANK + KV_RANK + QK_ROPE + 2 * SC_DIM,
             Q_RANK + KV_RANK + QK_ROPE + 3 * SC_DIM,
             Q_RANK + KV_RANK + QK_ROPE + 3 * SC_DIM + CF_DIM)
IN_COLS = Q_RANK + KV_RANK + QK_ROPE + 3 * SC_DIM + 2 * CF_DIM

kernel_name = "hymba_mla_conv_moe_deepnorm_diffusion_step"


def layer_norm_plain(x):
    xf = x.astype(jnp.float32)
    mu = jnp.mean(xf, axis=-1, keepdims=True)
    var = jnp.mean(jnp.square(xf - mu), axis=-1, keepdims=True)
    return ((xf - mu) * lax.rsqrt(var + EPS)).astype(x.dtype)


def layer_norm(x, g, b):
    xf = x.astype(jnp.float32)
    mu = jnp.mean(xf, axis=-1, keepdims=True)
    var = jnp.mean(jnp.square(xf - mu), axis=-1, keepdims=True)
    y = (xf - mu) * lax.rsqrt(var + EPS) * g.astype(jnp.float32) + b.astype(jnp.float32)
    return y.astype(x.dtype)


def rms_norm(x, g):
    xf = x.astype(jnp.float32)
    y = xf * lax.rsqrt(jnp.mean(jnp.square(xf), axis=-1, keepdims=True) + EPS) * g.astype(jnp.float32)
    return y.astype(x.dtype)


def axial_rope(rows, dtype):
    row = jnp.repeat(jnp.arange(rows, dtype=jnp.float32), GRID_W)
    col = jnp.tile(jnp.arange(GRID_W, dtype=jnp.float32), rows)
    inv = ROPE_BASE ** (-jnp.arange(0, ROPE_AXIS_DIM, 2, dtype=jnp.float32) / ROPE_AXIS_DIM)
    ang = jnp.concatenate([row[:, None] * inv, col[:, None] * inv], axis=-1)
    return jnp.cos(ang).astype(dtype), jnp.sin(ang).astype(dtype)


def apply_rope(x, cos, sin):
    x1, x2 = jnp.split(x, 2, axis=-1)
    return jnp.concatenate([x1 * cos - x2 * sin, x1 * sin + x2 * cos], axis=-1)


def depthwise_conv(x, w):
    k = w.shape[0]
    return lax.conv_general_dilated(
        x, w[:, None, :].astype(x.dtype), window_strides=(1,), padding=[(k // 2, k // 2)],
        dimension_numbers=("NWC", "WIO", "NWC"), feature_group_count=x.shape[-1])


def mla_attend(q_nope, q_rope, k_nope, k_rope, v):
    b, n = q_nope.shape[:2]
    nb = n // Q_BLOCK
    scale = 1.0 / math.sqrt(QK_DIM)
    qn = q_nope.reshape(b, nb, Q_BLOCK, MLA_HEADS, QK_NOPE).swapaxes(0, 1)
    qr = q_rope.reshape(b, nb, Q_BLOCK, MLA_HEADS, QK_ROPE).swapaxes(0, 1)

    def one_block(args):
        qn_b, qr_b = args
        s = (jnp.einsum("bqhd,bkhd->bhqk", qn_b, k_nope)
             + jnp.einsum("bqhr,bkr->bhqk", qr_b, k_rope))
        p = jax.nn.softmax(s.astype(jnp.float32) * scale, axis=-1).astype(v.dtype)
        return jnp.einsum("bhqk,bkhv->bqhv", p, v)

    o = lax.map(one_block, (qn, qr))
    return o.swapaxes(0, 1).reshape(b, n, MLA_OUT)


def moe_ffn(h, w_router, b_router, w_exp_in, b_exp_in, w_exp_out, b_exp_out):
    shp = h.shape
    x2d = h.reshape(-1, shp[-1])
    t = x2d.shape[0]
    logits = x2d @ w_router + b_router
    top_val, top_idx = lax.top_k(logits, TOP_K)
    gates = jax.nn.softmax(top_val.astype(jnp.float32), axis=-1).astype(h.dtype)
    n_assign = t * TOP_K
    e_flat = top_idx.reshape(-1)
    tok_flat = jnp.arange(n_assign, dtype=jnp.int32) // TOP_K
    g_flat = gates.reshape(-1)
    order = jnp.argsort(e_flat)
    e_sorted, tok_sorted, g_sorted = e_flat[order], tok_flat[order], g_flat[order]
    counts = jnp.bincount(e_flat, length=N_EXPERTS)
    padded = ((counts + MOE_BLOCK - 1) // MOE_BLOCK) * MOE_BLOCK
    start = jnp.cumsum(counts) - counts
    pend = jnp.cumsum(padded)
    pstart = pend - padded
    dest = pstart[e_sorted] + (jnp.arange(n_assign) - start[e_sorted])
    n_blocks = n_assign // MOE_BLOCK + N_EXPERTS
    p_rows = n_blocks * MOE_BLOCK
    buf_tok = jnp.full((p_rows,), t, dtype=jnp.int32).at[dest].set(tok_sorted)
    buf_g = jnp.zeros((p_rows,), h.dtype).at[dest].set(g_sorted)
    blk_expert = jnp.minimum(
        jnp.searchsorted(pend, jnp.arange(n_blocks) * MOE_BLOCK, side="right"), N_EXPERTS - 1)
    x_pad = jnp.concatenate([x2d, jnp.zeros((1, shp[-1]), h.dtype)], axis=0)

    def run_block(args):
        tok, g, e = args
        xb = x_pad[tok]
        z = xb @ w_exp_in[e] + b_exp_in[e]
        gate, up = z[:, :D_EXPERT], z[:, D_EXPERT:]
        gate = jnp.minimum(gate, SWIGLU_LIMIT)
        up = jnp.clip(up, -SWIGLU_LIMIT, SWIGLU_LIMIT)
        act = (up + 1.0) * gate * jax.nn.sigmoid(SWIGLU_ALPHA * gate)
        return (act @ w_exp_out[e] + b_exp_out[e]) * g[:, None]

    y = lax.map(run_block, (buf_tok.reshape(n_blocks, MOE_BLOCK),
                            buf_g.reshape(n_blocks, MOE_BLOCK), blk_expert))
    out = jnp.zeros((t + 1, shp[-1]), h.dtype).at[buf_tok].add(y.reshape(p_rows, shp[-1]))[:t]
    return out.reshape(shp)


def trunk_layer(x, mod, p, rope, ctx_ckv, ctx_krope):
    sh1, sc1, g1, sh2, sc2, g2 = jnp.split(mod, 6, axis=-1)
    b, n = x.shape[:2]
    h = layer_norm_plain(x) * (1.0 + sc1) + sh1
    cq, ckv_raw, krope, b_gate, c_gate, xs, glu_a, glu_b = jnp.split(h @ p["w_in"], IN_SPLITS, axis=-1)
    q = (rms_norm(cq, p["g_q_norm"]) @ p["w_uq"]).reshape(b, n, MLA_HEADS, QK_DIM)
    q_nope, q_rope = q[..., :QK_NOPE], q[..., QK_NOPE:]
    ckv = rms_norm(ckv_raw, p["g_kv_norm"])
    if rope is None:
        keys_ckv, keys_krope = ckv, krope
    else:
        cos, sin = rope
        q_rope = apply_rope(q_rope, cos[:, None, :], sin[:, None, :])
        keys_ckv = jnp.concatenate([ckv, ctx_ckv], axis=1)
        keys_krope = jnp.concatenate([apply_rope(krope, cos, sin), ctx_krope], axis=1)
    m = keys_ckv.shape[1]
    k_nope = (keys_ckv @ p["w_uk"]).reshape(b, m, MLA_HEADS, QK_NOPE)
    v = (keys_ckv @ p["w_uv"]).reshape(b, m, MLA_HEADS, V_DIM)
    attn = mla_attend(q_nope, q_rope, k_nope, keys_krope, v)
    sc_out = b_gate * depthwise_conv(c_gate * xs, p["w_sconv"])
    cf = glu_a * jax.nn.sigmoid(glu_b)
    cf = depthwise_conv(cf, p["w_cconv"]) + p["b_cconv"]
    cf = jax.nn.silu(layer_norm(cf, p["g_cf_norm"], p["b_cf_norm"]))
    mix = jnp.concatenate([attn, sc_out, cf], axis=-1) @ p["w_out"]
    x = layer_norm(DEEPNORM_ALPHA * x + g1 * mix, p["g_ln1"], p["b_ln1"])
    h = layer_norm_plain(x) * (1.0 + sc2) + sh2
    ffn = moe_ffn(h, p["w_router"], p["b_router"], p["w_exp_in"], p["b_exp_in"],
                  p["w_exp_out"], p["b_exp_out"])
    x = layer_norm(DEEPNORM_ALPHA * x + g2 * ffn, p["g_ln2"], p["b_ln2"])
    return x, ckv, krope


def setup_inputs(seed: int = 0) -> dict:
    key = jax.random.key(seed)
    ks = jax.random.split(key, 32)
    f32 = jnp.float32

    def nrm(k, shape, scale):
        return jax.random.normal(k, shape, f32) * scale

    def gain(k, shape):
        return 1.0 + 0.01 * jax.random.normal(k, shape, f32)

    return {
        "x_prompt": nrm(ks[0], (BATCH, SEQ, D_MODEL), 1.0),
        "x_sample": nrm(ks[1], (DEC_BATCH, DEC_SEQ, D_MODEL), 1.0),
        "cache_ckv": nrm(ks[2], (DEC_BATCH, DEPTH, PAST_LEN, KV_RANK), 1.0),
        "cache_krope": nrm(ks[3], (DEC_BATCH, DEPTH, PAST_LEN, QK_ROPE), 1.0),
        "c": nrm(ks[4], (DEC_BATCH, D_MODEL), 1.0),
        "c_ctx": nrm(ks[5], (D_MODEL,), 1.0),
        "w_mod": nrm(ks[6], (DEPTH, D_MODEL, 6 * D_MODEL), D_MODEL ** -0.5),
        "b_mod": nrm(ks[7], (DEPTH, 6 * D_MODEL), 0.01),
        "w_in": nrm(ks[8], (DEPTH, D_MODEL, IN_COLS), D_MODEL ** -0.5),
        "g_q_norm": gain(ks[9], (DEPTH, Q_RANK)),
        "w_uq": nrm(ks[10], (DEPTH, Q_RANK, MLA_HEADS * QK_DIM), Q_RANK ** -0.5),
        "g_kv_norm": gain(ks[11], (DEPTH, KV_RANK)),
        "w_uk": nrm(ks[12], (DEPTH, KV_RANK, MLA_HEADS * QK_NOPE), KV_RANK ** -0.5),
        "w_uv": nrm(ks[13], (DEPTH, KV_RANK, MLA_HEADS * V_DIM), KV_RANK ** -0.5 * DEEPNORM_BETA),
        "w_sconv": nrm(ks[14], (DEPTH, SC_WIDTH, SC_DIM), SC_WIDTH ** -0.5),
        "w_cconv": nrm(ks[15], (DEPTH, CF_WIDTH, CF_DIM), CF_WIDTH ** -0.5),
        "b_cconv": nrm(ks[16], (DEPTH, CF_DIM), 0.01),
        "g_cf_norm": gain(ks[17], (DEPTH, CF_DIM)),
        "b_cf_norm": nrm(ks[18], (DEPTH, CF_DIM), 0.01),
        "w_out": nrm(ks[19], (DEPTH, MIX_DIM, D_MODEL), MIX_DIM ** -0.5 * DEEPNORM_BETA),
        "g_ln1": gain(ks[20], (DEPTH, D_MODEL)),
        "b_ln1": nrm(ks[21], (DEPTH, D_MODEL), 0.01),
        "g_ln2": gain(ks[22], (DEPTH, D_MODEL)),
        "b_ln2": nrm(ks[23], (DEPTH, D_MODEL), 0.01),
        "w_router": nrm(ks[24], (DEPTH, D_MODEL, N_EXPERTS), D_MODEL ** -0.5),
        "b_router": nrm(ks[25], (DEPTH, N_EXPERTS), 0.01),
        "w_exp_in": nrm(ks[26], (DEPTH, N_EXPERTS, D_MODEL, 2 * D_EXPERT), D_MODEL ** -0.5),
        "b_exp_in": nrm(ks[27], (DEPTH, N_EXPERTS, 2 * D_EXPERT), 0.01),
        "w_exp_out": nrm(ks[28], (DEPTH, N_EXPERTS, D_EXPERT, D_MODEL), D_EXPERT ** -0.5 * DEEPNORM_BETA),
        "b_exp_out": nrm(ks[29], (DEPTH, N_EXPERTS, D_MODEL), 0.01),
    }


def reference(x_prompt, x_sample, cache_ckv, cache_krope, c, c_ctx,
              w_mod, b_mod, w_in, g_q_norm, w_uq, g_kv_norm, w_uk, w_uv,
              w_sconv, w_cconv, b_cconv, g_cf_norm, b_cf_norm, w_out,
              g_ln1, b_ln1, g_ln2, b_ln2, w_router, b_router,
              w_exp_in, b_exp_in, w_exp_out, b_exp_out):
    rows = x_sample.shape[1] // GRID_W
    rope = axial_rope(rows, x_sample.dtype)
    xp, xs = x_prompt, x_sample
    new_ckv, new_krope = [], []
    for l in range(DEPTH):
        p = {"w_in": w_in[l], "g_q_norm": g_q_norm[l], "w_uq": w_uq[l], "g_kv_norm": g_kv_norm[l],
             "w_uk": w_uk[l], "w_uv": w_uv[l], "w_sconv": w_sconv[l], "w_cconv": w_cconv[l],
             "b_cconv": b_cconv[l], "g_cf_norm": g_cf_norm[l], "b_cf_norm": b_cf_norm[l],
             "w_out": w_out[l], "g_ln1": g_ln1[l], "b_ln1": b_ln1[l], "g_ln2": g_ln2[l],
             "b_ln2": b_ln2[l], "w_router": w_router[l], "b_router": b_router[l],
             "w_exp_in": w_exp_in[l], "b_exp_in": b_exp_in[l], "w_exp_out": w_exp_out[l],
             "b_exp_out": b_exp_out[l]}
        mod_ctx = jax.nn.silu(c_ctx) @ w_mod[l] + b_mod[l]
        xp, ckv_l, krope_l = trunk_layer(xp, mod_ctx, p, None, None, None)
        new_ckv.append(ckv_l)
        new_krope.append(krope_l)
        mod_lat = (jax.nn.silu(c) @ w_mod[l] + b_mod[l])[:, None, :]
        xs, _, _ = trunk_layer(xs, mod_lat, p, rope, cache_ckv[:, l], cache_krope[:, l])
    ckv_out = jnp.stack(new_ckv, axis=1)
    krope_out = jnp.stack(new_krope, axis=1)
    return (xp, xs, ckv_out, krope_out)
```

```python
import functools
import math

import jax
import jax.numpy as jnp
from jax import lax
from jax.experimental import pallas as pl
from jax.experimental.pallas import tpu as pltpu

F32 = jnp.float32
BF16 = jnp.bfloat16

D_MODEL = 1024
GRID_W = 64
HEADS = 8
QK_NOPE = 64
QK_ROPE = 32
QK_DIM = QK_NOPE + QK_ROPE
V_DIM = 64
Q_RANK = 192
KV_RANK = 128
SC_DIM = 256
SC_WIDTH = 3
CF_DIM = 256
CF_WIDTH = 31
N_EXPERTS = 32
TOP_K = 4
D_EXPERT = 1024
SWIGLU_LIMIT = 7.0
SWIGLU_ALPHA = 1.702
ROPE_BASE = 10000.0
EPS = 1e-6

LANES = 128
SUBLANES = 8
ROW_TILES = D_MODEL // LANES

TM = 256
TQ = 256
MB = 256
HALO = 16
HEAD_PAD = LANES
Q_RANK_PAD = 256
KFEAT = 256
COL_CQ, COL_CKV, COL_KR, COL_BG, COL_CG, COL_XS, COL_GA, COL_GB, IN_COLS_PAD = (
    0, 256, 384, 512, 768, 1024, 1280, 1536, 1792)
VMEM_LIMIT = 56 * 1024 * 1024


def _cparams(*sem):
    return pltpu.CompilerParams(dimension_semantics=sem, vmem_limit_bytes=VMEM_LIMIT)


def _ln_plain(x):
    mu = jnp.mean(x, axis=-1, keepdims=True)
    xc = x - mu
    var = jnp.mean(xc * xc, axis=-1, keepdims=True)
    return xc * lax.rsqrt(var + EPS)


def _mod_kernel(c_ref, w_ref, b_ref, o_ref):
    c = c_ref[...]
    s = c * jax.nn.sigmoid(c)
    o_ref[0] = jnp.dot(s, w_ref[0], preferred_element_type=F32,
                       precision=lax.Precision.HIGHEST) + b_ref[0]


def _modulation(c_rows, w_mod, b_mod):
    depth = w_mod.shape[0]
    rows = c_rows.shape[0]
    nblk = w_mod.shape[2] // D_MODEL
    return pl.pallas_call(
        _mod_kernel,
        grid=(depth, nblk),
        in_specs=[pl.BlockSpec((rows, D_MODEL), lambda l, j: (0, 0)),
                  pl.BlockSpec((1, D_MODEL, D_MODEL), lambda l, j: (l, 0, j)),
                  pl.BlockSpec((1, 1, D_MODEL), lambda l, j: (l, 0, j))],
        out_specs=pl.BlockSpec((1, rows, D_MODEL), lambda l, j: (l, 0, j)),
        out_shape=jax.ShapeDtypeStruct((depth, rows, w_mod.shape[2]), F32),
        compiler_params=_cparams("arbitrary", "arbitrary"),
        name="modulation",
    )(c_rows, w_mod, b_mod.reshape(depth, 1, -1))


def _in_proj_kernel(n_p_tiles, x_ref, mod_ref, win_ref, gq_ref, gkv_ref, wqa_ref, wqb_ref,
                    tcq_ref, tsq_ref, tck_ref, tsk_ref,
                    q_ref, ckv_ref, kr_ref, kf_ref, cv_ref):
    is_s = pl.program_id(0) >= n_p_tiles
    mod = mod_ref[0]
    h = _ln_plain(x_ref[...]) * (1.0 + mod[:, D_MODEL:2 * D_MODEL]) + mod[:, 0:D_MODEL]
    z = jnp.dot(h.astype(BF16), win_ref[...], preferred_element_type=F32)

    cq = z[:, COL_CQ:COL_CQ + Q_RANK_PAD]
    cqn = cq * lax.rsqrt(jnp.sum(cq * cq, axis=-1, keepdims=True) * (1.0 / Q_RANK) + EPS) * gq_ref[...]
    cqn = cqn.astype(BF16)
    qa = jnp.dot(cqn, wqa_ref[...], preferred_element_type=F32)
    qb = jnp.dot(cqn, wqb_ref[...], preferred_element_type=F32)
    cos_q = jnp.where(is_s, tcq_ref[...], 1.0)
    sin_q = jnp.where(is_s, tsq_ref[...], 0.0)
    scale = 1.0 / math.sqrt(QK_DIM)
    for hd in range(HEADS):
        sl = slice(hd * HEAD_PAD, (hd + 1) * HEAD_PAD)
        q_ref[hd] = ((qa[:, sl] * cos_q + qb[:, sl] * sin_q) * scale).astype(BF16)

    cr = z[:, COL_CKV:COL_CKV + KV_RANK]
    ckv = cr * lax.rsqrt(jnp.mean(cr * cr, axis=-1, keepdims=True) + EPS) * gkv_ref[...]
    ckv_ref[...] = ckv
    zk = z[:, COL_KR:COL_KR + LANES]
    kr_ref[...] = zk[:, 0:QK_ROPE]
    lane = lax.broadcasted_iota(jnp.int32, zk.shape, 1)
    raw = jnp.where(lane < QK_ROPE, zk, 0.0)
    roped = zk * tck_ref[...] + pltpu.roll(zk, LANES - QK_ROPE, 1) * tsk_ref[...]
    kf_ref[:, 0:KV_RANK] = ckv.astype(BF16)
    kf_ref[:, KV_RANK:KFEAT] = jnp.where(is_s, roped, raw).astype(BF16)

    cv_ref[:, 0:SC_DIM] = z[:, COL_BG:COL_BG + SC_DIM]
    cv_ref[:, SC_DIM:2 * SC_DIM] = z[:, COL_CG:COL_CG + SC_DIM] * z[:, COL_XS:COL_XS + SC_DIM]
    cv_ref[:, 2 * SC_DIM:] = z[:, COL_GA:COL_GA + CF_DIM] * jax.nn.sigmoid(z[:, COL_GB:COL_GB + CF_DIM])


def _in_proj(x, mod, wl, tabs, n_p_tiles, tiles_per_s_seq, ctx_row):
    t = x.shape[0]
    nt = t // TM

    def mod_idx(i):
        return (jnp.where(i < n_p_tiles, ctx_row, (i - n_p_tiles) // tiles_per_s_seq), 0, 0)

    def pos_idx(i):
        return (jnp.where(i < n_p_tiles, 0, (i - n_p_tiles) % tiles_per_s_seq), 0)

    full = lambda shp: pl.BlockSpec(shp, lambda i: (0,) * len(shp))
    tab = pl.BlockSpec((TM, LANES), pos_idx)
    return pl.pallas_call(
        functools.partial(_in_proj_kernel, n_p_tiles),
        grid=(nt,),
        in_specs=[pl.BlockSpec((TM, D_MODEL), lambda i: (i, 0)),
                  pl.BlockSpec((1, 1, 6 * D_MODEL), mod_idx),
                  full((D_MODEL, IN_COLS_PAD)),
                  full((1, Q_RANK_PAD)), full((1, KV_RANK)),
                  full((Q_RANK_PAD, HEADS * HEAD_PAD)), full((Q_RANK_PAD, HEADS * HEAD_PAD)),
                  tab, tab, tab, tab],
        out_specs=[pl.BlockSpec((HEADS, TM, HEAD_PAD), lambda i: (0, i, 0)),
                   pl.BlockSpec((TM, KV_RANK), lambda i: (i, 0)),
                   pl.BlockSpec((TM, QK_ROPE), lambda i: (i, 0)),
                   pl.BlockSpec((TM, KFEAT), lambda i: (i, 0)),
                   pl.BlockSpec((TM, 3 * SC_DIM), lambda i: (i, 0))],
        out_shape=[jax.ShapeDtypeStruct((HEADS, t, HEAD_PAD), BF16),
                   jax.ShapeDtypeStruct((t, KV_RANK), F32),
                   jax.ShapeDtypeStruct((t, QK_ROPE), F32),
                   jax.ShapeDtypeStruct((t, KFEAT), BF16),
                   jax.ShapeDtypeStruct((t, 3 * SC_DIM), F32)],
        compiler_params=_cparams("arbitrary"),
        name="in_proj",
    )(x, mod, wl["w_in"], wl["g_q"], wl["g_kv"], wl["wq_a"], wl["wq_b"], *tabs)


def _attn_kernel(n_own, n_cache, q_ref, kf_ref, *rest):
    if n_cache:
        kc_ref, wkt_ref, wv_ref, o_ref, kt_scr, v_scr = rest
    else:
        wkt_ref, wv_ref, o_ref, kt_scr, v_scr = rest

    @pl.when(pl.program_id(1) == 0)
    def _():
        def fill(feat, lo, n):
            for hd in range(HEADS):
                kt = lax.dot_general(wkt_ref[hd], feat, (((1,), (1,)), ((), ())),
                                     preferred_element_type=F32)
                kt_scr[hd, :, lo:lo + n] = kt.astype(BF16)
                v = jnp.dot(feat, wv_ref[:, hd * LANES:(hd + 1) * LANES], preferred_element_type=F32)
                v_scr[hd, lo:lo + n, :] = v.astype(BF16)

        fill(kf_ref[...], 0, n_own)
        if n_cache:
            fill(kc_ref[0], n_own, n_cache)

    for pair in range(HEADS // 2):
        acc = None
        for hd in (2 * pair, 2 * pair + 1):
            s = jnp.dot(q_ref[hd], kt_scr[hd], preferred_element_type=F32)
            p = jnp.exp(s - jnp.max(s, axis=-1, keepdims=True))
            l = jnp.sum(p, axis=-1, keepdims=True)
            o = jnp.dot(p.astype(BF16), v_scr[hd], preferred_element_type=F32) / l
            acc = o if acc is None else acc + o
        o_ref[pair] = acc.astype(BF16)


def _attention(q, kfeat, kcache, wl, batch, n_own, row0):
    n_cache = 0 if kcache is None else kcache.shape[1]
    m = n_own + n_cache
    tq = min(TQ, n_own)
    nq = n_own // tq
    q0 = row0 // tq
    k0 = row0 // n_own
    in_specs = [pl.BlockSpec((HEADS, tq, HEAD_PAD), lambda b, j: (0, q0 + b * nq + j, 0)),
                pl.BlockSpec((n_own, KFEAT), lambda b, j: (k0 + b, 0))]
    args = [q, kfeat]
    if n_cache:
        in_specs.append(pl.BlockSpec((1, n_cache, KFEAT), lambda b, j: (b, 0, 0)))
        args.append(kcache)
    in_specs += [pl.BlockSpec((HEADS, HEAD_PAD, KFEAT), lambda b, j: (0, 0, 0)),
                 pl.BlockSpec((KFEAT, HEADS * LANES), lambda b, j: (0, 0))]
    args += [wl["wk_t"], wl["wv"]]
    return pl.pallas_call(
        functools.partial(_attn_kernel, n_own, n_cache),
        grid=(batch, nq),
        in_specs=in_specs,
        out_specs=pl.BlockSpec((HEADS // 2, tq, LANES), lambda b, j: (0, b * nq + j, 0)),
        out_shape=jax.ShapeDtypeStruct((HEADS // 2, batch * n_own, LANES), BF16),
        scratch_shapes=[pltpu.VMEM((HEADS, HEAD_PAD, m), BF16), pltpu.VMEM((HEADS, m, LANES), BF16)],
        compiler_params=_cparams("arbitrary", "arbitrary"),
        name="attention_cache" if n_cache else "attention_ctx",
    )(*args)


def _mix_kernel(n_p_tiles, tiles_per_s_seq, alpha,
                x_ref, at_ref, cprev_ref, ccur_ref, cnext_ref, mod_ref,
                wsc_ref, wcc_ref, bcc_ref, gcf_ref, bcf_ref, wout_ref, g1_ref, b1_ref,
                wr_ref, br_ref,
                x1_ref, h2_ref, idx_ref, rank_ref, gate_ref, cnt_ref,
                ext_scr, carry_scr):
    i = pl.program_id(0)
    is_s = i >= n_p_tiles
    jj = (i - n_p_tiles) % tiles_per_s_seq
    has_prev = jnp.logical_and(is_s, jj > 0)
    has_next = jnp.logical_and(is_s, jj < tiles_per_s_seq - 1)

    @pl.when(i == 0)
    def _():
        carry_scr[...] = jnp.zeros_like(carry_scr)

    ext_scr[0:HALO, :] = jnp.where(has_prev, cprev_ref[:, SC_DIM:], 0.0)
    ext_scr[HALO:HALO + TM, :] = ccur_ref[:, SC_DIM:]
    ext_scr[HALO + TM:, :] = jnp.where(has_next, cnext_ref[:, SC_DIM:], 0.0)

    sconv = jnp.zeros((TM, SC_DIM), F32)
    for k in range(SC_WIDTH):
        o = HALO - SC_WIDTH // 2 + k
        sconv = sconv + ext_scr[o:o + TM, 0:SC_DIM] * wsc_ref[k:k + 1, :]
    sc_out = ccur_ref[:, 0:SC_DIM] * sconv

    cconv = jnp.zeros((TM, CF_DIM), F32)
    for k in range(CF_WIDTH):
        o = HALO - CF_WIDTH // 2 + k
        cconv = cconv + ext_scr[o:o + TM, SC_DIM:] * wcc_ref[k:k + 1, :]
    cf = _ln_plain(cconv + bcc_ref[...]) * gcf_ref[...] + bcf_ref[...]
    cf = cf * jax.nn.sigmoid(cf)

    mix_in = jnp.concatenate([at_ref[0], at_ref[1], at_ref[2], at_ref[3],
                              sc_out.astype(BF16), cf.astype(BF16)], axis=1)
    mix = jnp.dot(mix_in, wout_ref[...], preferred_element_type=F32)

    mod = mod_ref[0]
    g1 = mod[:, 2 * D_MODEL:3 * D_MODEL]
    sh2 = mod[:, 3 * D_MODEL:4 * D_MODEL]
    sc2 = mod[:, 4 * D_MODEL:5 * D_MODEL]
    x1 = _ln_plain(alpha * x_ref[...] + g1 * mix) * g1_ref[...] + b1_ref[...]
    x1_ref[...] = x1
    h2 = _ln_plain(x1) * (1.0 + sc2) + sh2
    for j in range(ROW_TILES):
        h2_ref[:, j, :] = h2[:, j * LANES:(j + 1) * LANES]

    logits = jnp.dot(h2, wr_ref[...], preferred_element_type=F32,
                     precision=lax.Precision.HIGHEST) + br_ref[...]
    lane = lax.broadcasted_iota(jnp.int32, (TM, LANES), 1)
    lane_f = lane.astype(F32)
    neg = jnp.float32(-jnp.inf)
    cur = jnp.where(lane < N_EXPERTS, logits, neg)
    vals, hots, idxs = [], [], []
    for _ in range(TOP_K):
        mx = jnp.max(cur, axis=-1, keepdims=True)
        ik = jnp.min(jnp.where(cur == mx, lane_f, float(LANES)), axis=-1, keepdims=True)
        ik = jnp.minimum(ik, float(N_EXPERTS - 1))
        hot = lane_f == ik
        cur = jnp.where(hot, neg, cur)
        vals.append(mx)
        hots.append(hot)
        idxs.append(ik.astype(jnp.int32))
    exps = [jnp.exp(v - vals[0]) for v in vals]
    denom = exps[0] + exps[1] + exps[2] + exps[3]
    sel = jnp.where(hots[0] | hots[1] | hots[2] | hots[3], 1.0, 0.0)

    r_i = lax.broadcasted_iota(jnp.int32, (TM, TM), 0)
    c_i = lax.broadcasted_iota(jnp.int32, (TM, TM), 1)
    tri = jnp.where(c_i < r_i, 1.0, 0.0).astype(BF16)
    rank_all = jnp.dot(tri, sel.astype(BF16), preferred_element_type=F32) + carry_scr[0:1, :]
    new_carry = carry_scr[0:1, :] + jnp.sum(sel, axis=0, keepdims=True)
    carry_scr[...] = jnp.broadcast_to(new_carry, carry_scr.shape)
    cnt_ref[...] = jnp.broadcast_to(new_carry, cnt_ref.shape)

    idx_o = jnp.zeros((TM, LANES), jnp.int32)
    rank_o = jnp.zeros((TM, LANES), jnp.int32)
    gate_o = jnp.zeros((TM, LANES), F32)
    for k in range(TOP_K):
        rk = jnp.sum(jnp.where(hots[k], rank_all, 0.0), axis=-1, keepdims=True).astype(jnp.int32)
        idx_o = jnp.where(lane == k, idxs[k], idx_o)
        rank_o = jnp.where(lane == k, rk, rank_o)
        gate_o = jnp.where(lane == k, exps[k] / denom, gate_o)
    idx_ref[...] = idx_o
    rank_ref[...] = rank_o
    gate_ref[...] = gate_o


def _mix(x, attn, conv_in, mod, wl, n_p_tiles, tiles_per_s_seq, ctx_row, alpha):
    t = x.shape[0]
    nt = t // TM
    hpt = TM // HALO
    n_halo = t // HALO

    def mod_idx(i):
        return (jnp.where(i < n_p_tiles, ctx_row, (i - n_p_tiles) // tiles_per_s_seq), 0, 0)

    full = lambda shp: pl.BlockSpec(shp, lambda i: (0,) * len(shp))
    row = lambda w: pl.BlockSpec((TM, w), lambda i: (i, 0))
    return pl.pallas_call(
        functools.partial(_mix_kernel, n_p_tiles, tiles_per_s_seq, alpha),
        grid=(nt,),
        in_specs=[row(D_MODEL),
                  pl.BlockSpec((HEADS // 2, TM, LANES), lambda i: (0, i, 0)),
                  pl.BlockSpec((HALO, 3 * SC_DIM), lambda i: (jnp.maximum(i * hpt - 1, 0), 0)),
                  row(3 * SC_DIM),
                  pl.BlockSpec((HALO, 3 * SC_DIM), lambda i: (jnp.minimum((i + 1) * hpt, n_halo - 1), 0)),
                  pl.BlockSpec((1, 1, 6 * D_MODEL), mod_idx),
                  full((SC_WIDTH, SC_DIM)), full((CF_WIDTH, CF_DIM)), full((1, CF_DIM)),
                  full((1, CF_DIM)), full((1, CF_DIM)),
                  full((D_MODEL, D_MODEL)), full((1, D_MODEL)), full((1, D_MODEL)),
                  full((D_MODEL, LANES)), full((1, LANES))],
        out_specs=[row(D_MODEL),
                   pl.BlockSpec((TM, ROW_TILES, LANES), lambda i: (i, 0, 0)),
                   row(LANES), row(LANES), row(LANES),
                   pl.BlockSpec((SUBLANES, LANES), lambda i: (0, 0))],
        out_shape=[jax.ShapeDtypeStruct((t, D_MODEL), F32),
                   jax.ShapeDtypeStruct((t, ROW_TILES, LANES), F32),
                   jax.ShapeDtypeStruct((t, LANES), jnp.int32),
                   jax.ShapeDtypeStruct((t, LANES), jnp.int32),
                   jax.ShapeDtypeStruct((t, LANES), F32),
                   jax.ShapeDtypeStruct((SUBLANES, LANES), F32)],
        scratch_shapes=[pltpu.VMEM((TM + 2 * HALO, 2 * SC_DIM), F32), pltpu.VMEM((SUBLANES, LANES), F32)],
        compiler_params=_cparams("arbitrary"),
        name="mix",
    )(x, attn, conv_in, conv_in, conv_in, mod,
      wl["w_sconv"], wl["w_cconv"], wl["b_cconv"], wl["g_cf"], wl["b_cf"],
      wl["w_out"], wl["g_ln1"], wl["b_ln1"], wl["w_router"], wl["b_router"])


def _dispatch_kernel(dest_ref, h_hbm, xs_in_hbm, xs_hbm, sem):
    del xs_in_hbm
    t0 = pl.program_id(0) * TM

    def issue(t, carry):
        for k in range(TOP_K):
            pltpu.make_async_copy(h_hbm.at[t0 + t], xs_hbm.at[dest_ref[TOP_K * t + k]], sem).start()
        return carry

    lax.fori_loop(0, TM, issue, 0)
    pltpu.make_async_copy(h_hbm.at[pl.ds(0, TM * TOP_K)], xs_hbm.at[pl.ds(0, TM * TOP_K)], sem).wait()


def _dispatch(dest_flat, h_rows, p_rows):
    t = h_rows.shape[0]
    xs0 = jnp.zeros((p_rows, ROW_TILES, LANES), F32)
    return pl.pallas_call(
        _dispatch_kernel,
        grid=(t // TM,),
        in_specs=[pl.BlockSpec((TM * TOP_K,), lambda i: (i,), memory_space=pltpu.SMEM),
                  pl.BlockSpec(memory_space=pl.ANY),
                  pl.BlockSpec(memory_space=pl.ANY)],
        out_specs=pl.BlockSpec(memory_space=pl.ANY),
        out_shape=jax.ShapeDtypeStruct((p_rows, ROW_TILES, LANES), F32),
        scratch_shapes=[pltpu.SemaphoreType.DMA(())],
        input_output_aliases={2: 0},
        compiler_params=_cparams("arbitrary"),
        name="dispatch",
    )(dest_flat, h_rows, xs0)


def _expert_kernel(be_ref, nu_ref, xs_ref, w1_ref, b1_ref, w2_ref, b2_ref, ys_ref):
    del be_ref
    i = pl.program_id(0)

    @pl.when(i < nu_ref[0])
    def _():
        x = jnp.concatenate([xs_ref[:, j, :] for j in range(ROW_TILES)], axis=1).astype(BF16)
        z = jnp.dot(x, w1_ref[0], preferred_element_type=F32) + b1_ref[0]
        gate = jnp.minimum(z[:, :D_EXPERT], SWIGLU_LIMIT)
        up = jnp.clip(z[:, D_EXPERT:], -SWIGLU_LIMIT, SWIGLU_LIMIT)
        act = (up + 1.0) * gate * jax.nn.sigmoid(SWIGLU_ALPHA * gate)
        y = jnp.dot(act.astype(BF16), w2_ref[0], preferred_element_type=F32) + b2_ref[0]
        for j in range(ROW_TILES):
            ys_ref[:, j, :] = y[:, j * LANES:(j + 1) * LANES]

    @pl.when(i >= nu_ref[0])
    def _():
        ys_ref[...] = jnp.zeros_like(ys_ref)


def _experts(blk_expert, n_used, xs, wl):
    p_rows = xs.shape[0]
    nb = p_rows // MB
    grid_spec = pltpu.PrefetchScalarGridSpec(
        num_scalar_prefetch=2,
        grid=(nb,),
        in_specs=[pl.BlockSpec((MB, ROW_TILES, LANES), lambda i, be, nu: (i, 0, 0)),
                  pl.BlockSpec((1, D_MODEL, 2 * D_EXPERT), lambda i, be, nu: (be[i], 0, 0)),
                  pl.BlockSpec((1, 1, 2 * D_EXPERT), lambda i, be, nu: (be[i], 0, 0)),
                  pl.BlockSpec((1, D_EXPERT, D_MODEL), lambda i, be, nu: (be[i], 0, 0)),
                  pl.BlockSpec((1, 1, D_MODEL), lambda i, be, nu: (be[i], 0, 0))],
        out_specs=pl.BlockSpec((MB, ROW_TILES, LANES), lambda i, be, nu: (i, 0, 0)),
    )
    return pl.pallas_call(
        _expert_kernel,
        grid_spec=grid_spec,
        out_shape=jax.ShapeDtypeStruct((p_rows, ROW_TILES, LANES), F32),
        compiler_params=_cparams("arbitrary"),
        name="experts",
    )(blk_expert, n_used, xs, wl["w_exp_in"], wl["b_exp_in"], wl["w_exp_out"], wl["b_exp_out"])


def _combine_kernel(alpha, dest_ref, gate_ref, x1_ref, mod_ref, g2_ref, b2_ref, ys_hbm, o_ref, buf, sem):
    def issue(t, carry):
        for k in range(TOP_K):
            pltpu.make_async_copy(ys_hbm.at[dest_ref[TOP_K * t + k]], buf.at[k * TM + t], sem).start()
        return carry

    lax.fori_loop(0, TM, issue, 0)
    pltpu.make_async_copy(ys_hbm.at[pl.ds(0, TM * TOP_K)], buf, sem).wait()

    gates = gate_ref[...]
    ffn = None
    for k in range(TOP_K):
        yk = jnp.concatenate([buf[k * TM:(k + 1) * TM, j, :] for j in range(ROW_TILES)], axis=1)
        term = yk * gates[:, k:k + 1]
        ffn = term if ffn is None else ffn + term
    g2 = mod_ref[0][:, 5 * D_MODEL:6 * D_MODEL]
    o_ref[...] = _ln_plain(alpha * x1_ref[...] + g2 * ffn) * g2_ref[...] + b2_ref[...]


def _combine(dest_flat, gates, x1, mod, ys, wl, n_p_tiles, tiles_per_s_seq, ctx_row, alpha):
    t = x1.shape[0]

    def mod_idx(i):
        return (jnp.where(i < n_p_tiles, ctx_row, (i - n_p_tiles) // tiles_per_s_seq), 0, 0)

    return pl.pallas_call(
        functools.partial(_combine_kernel, alpha),
        grid=(t // TM,),
        in_specs=[pl.BlockSpec((TM * TOP_K,), lambda i: (i,), memory_space=pltpu.SMEM),
                  pl.BlockSpec((TM, LANES), lambda i: (i, 0)),
                  pl.BlockSpec((TM, D_MODEL), lambda i: (i, 0)),
                  pl.BlockSpec((1, 1, 6 * D_MODEL), mod_idx),
                  pl.BlockSpec((1, D_MODEL), lambda i: (0, 0)),
                  pl.BlockSpec((1, D_MODEL), lambda i: (0, 0)),
                  pl.BlockSpec(memory_space=pl.ANY)],
        out_specs=pl.BlockSpec((TM, D_MODEL), lambda i: (i, 0)),
        out_shape=jax.ShapeDtypeStruct((t, D_MODEL), F32),
        scratch_shapes=[pltpu.VMEM((TM * TOP_K, ROW_TILES, LANES), F32), pltpu.SemaphoreType.DMA(())],
        compiler_params=_cparams("arbitrary"),
        name="combine",
    )(dest_flat, gates, x1, mod, wl["g_ln2"], wl["b_ln2"], ys)


def _pad_cols(w, width):
    return jnp.pad(w, ((0, 0), (0, width - w.shape[1])))


def _rot_cols(w):
    half = w.shape[1] // 2
    return jnp.concatenate([-w[:, half:], w[:, :half]], axis=1)


def _layer_weights(l, w_in, g_q_norm, w_uq, g_kv_norm, w_uk, w_uv, w_sconv, w_cconv, b_cconv,
                   g_cf_norm, b_cf_norm, w_out, g_ln1, b_ln1, g_ln2, b_ln2, w_router, b_router,
                   w_exp_in, b_exp_in, w_exp_out, b_exp_out):
    wi = w_in[l]
    o_ckv, o_kr, o_bg = Q_RANK, Q_RANK + KV_RANK, Q_RANK + KV_RANK + QK_ROPE
    w_kr = wi[:, o_kr:o_bg]
    w_in_p = jnp.concatenate([
        _pad_cols(wi[:, :Q_RANK], COL_CKV - COL_CQ),
        wi[:, o_ckv:o_kr],
        _pad_cols(jnp.concatenate([w_kr, _rot_cols(w_kr)], axis=1), COL_BG - COL_KR),
        wi[:, o_bg:]], axis=1).astype(BF16)

    uq = jnp.pad(w_uq[l], ((0, Q_RANK_PAD - Q_RANK), (0, 0))).reshape(Q_RANK_PAD, HEADS, QK_DIM)
    zer = jnp.zeros((Q_RANK_PAD, HEADS, HEAD_PAD - QK_DIM), F32)
    wq_a = jnp.concatenate([uq, zer], axis=2).reshape(Q_RANK_PAD, HEADS * HEAD_PAD).astype(BF16)
    rot = jnp.concatenate([-uq[:, :, QK_NOPE + QK_ROPE // 2:], uq[:, :, QK_NOPE:QK_NOPE + QK_ROPE // 2]], axis=2)
    wq_b = jnp.concatenate([jnp.zeros((Q_RANK_PAD, HEADS, QK_NOPE), F32), rot, zer],
                           axis=2).reshape(Q_RANK_PAD, HEADS * HEAD_PAD).astype(BF16)

    uk_t = w_uk[l].reshape(KV_RANK, HEADS, QK_NOPE).transpose(1, 2, 0)
    top = jnp.pad(uk_t, ((0, 0), (0, 0), (0, KFEAT - KV_RANK)))
    eye = jnp.pad(jnp.eye(QK_ROPE, dtype=F32), ((0, 0), (KV_RANK, KFEAT - KV_RANK - QK_ROPE)))
    mid = jnp.broadcast_to(eye[None], (HEADS, QK_ROPE, KFEAT))
    wk_t = jnp.concatenate([top, mid, jnp.zeros((HEADS, HEAD_PAD - QK_DIM, KFEAT), F32)], axis=1).astype(BF16)

    uv = w_uv[l].reshape(KV_RANK, HEADS, V_DIM)
    zv = jnp.zeros((KV_RANK, HEADS, V_DIM), F32)
    odd = (jnp.arange(HEADS) % 2 == 1)[None, :, None]
    wv = jnp.concatenate([jnp.where(odd, zv, uv), jnp.where(odd, uv, zv)], axis=2).reshape(KV_RANK, HEADS * LANES)
    wv = jnp.pad(wv, ((0, KFEAT - KV_RANK), (0, 0))).astype(BF16)

    r1 = lambda v: v.reshape(1, -1)
    return {
        "w_in": w_in_p,
        "g_q": jnp.pad(g_q_norm[l], (0, Q_RANK_PAD - Q_RANK)).reshape(1, -1),
        "g_kv": r1(g_kv_norm[l]),
        "wq_a": wq_a, "wq_b": wq_b, "wk_t": wk_t, "wv": wv,
        "w_sconv": w_sconv[l], "w_cconv": w_cconv[l], "b_cconv": r1(b_cconv[l]),
        "g_cf": r1(g_cf_norm[l]), "b_cf": r1(b_cf_norm[l]),
        "w_out": w_out[l].astype(BF16),
        "g_ln1": r1(g_ln1[l]), "b_ln1": r1(b_ln1[l]), "g_ln2": r1(g_ln2[l]), "b_ln2": r1(b_ln2[l]),
        "w_router": _pad_cols(w_router[l], LANES),
        "b_router": jnp.pad(b_router[l], (0, LANES - N_EXPERTS)).reshape(1, -1),
        "w_exp_in": w_exp_in[l].astype(BF16),
        "b_exp_in": b_exp_in[l].reshape(N_EXPERTS, 1, -1),
        "w_exp_out": w_exp_out[l].astype(BF16),
        "b_exp_out": b_exp_out[l].reshape(N_EXPERTS, 1, -1),
    }


def _rope_tables(n_lat):
    rows = n_lat // GRID_W
    row = jnp.repeat(jnp.arange(rows, dtype=F32), GRID_W)
    col = jnp.tile(jnp.arange(GRID_W, dtype=F32), rows)
    axis_dim = QK_ROPE // 2
    inv = ROPE_BASE ** (-jnp.arange(0, axis_dim, 2, dtype=F32) / axis_dim)
    ang = jnp.concatenate([row[:, None] * inv, col[:, None] * inv], axis=-1)
    cos, sin = jnp.cos(ang), jnp.sin(ang)
    one = jnp.ones((n_lat, QK_NOPE), F32)
    zq = jnp.zeros((n_lat, HEAD_PAD - QK_DIM), F32)
    zk = jnp.zeros((n_lat, LANES - QK_ROPE), F32)
    return (jnp.concatenate([one, cos, cos, zq], axis=1),
            jnp.concatenate([0.0 * one, sin, sin, zq], axis=1),
            jnp.concatenate([cos, cos, zk], axis=1),
            jnp.concatenate([sin, sin, zk], axis=1))


def kernel(x_prompt, x_sample, cache_ckv, cache_krope, c, c_ctx, w_mod, b_mod, w_in, g_q_norm, w_uq, g_kv_norm, w_uk, w_uv, w_sconv, w_cconv, b_cconv, g_cf_norm, b_cf_norm, w_out, g_ln1, b_ln1, g_ln2, b_ln2, w_router, b_router, w_exp_in, b_exp_in, w_exp_out, b_exp_out):
    bp, n_p, d = x_prompt.shape
    bs, n_s, _ = x_sample.shape
    depth = w_mod.shape[0]
    past = cache_ckv.shape[2]
    assert d == D_MODEL and n_p == TM and n_s % TM == 0 and n_s % GRID_W == 0
    t_p, t_s = bp * n_p, bs * n_s
    assert t_p % n_s == 0 and past % LANES == 0
    t = t_p + t_s
    n_p_tiles = t_p // TM
    tps = n_s // TM
    alpha = float((2 * depth) ** 0.25)

    ctx_row = bs
    rows = -(-(bs + 1) // SUBLANES) * SUBLANES
    c_rows = jnp.concatenate([c, c_ctx[None], jnp.zeros((rows - bs - 1, d), F32)], axis=0)
    mod_all = _modulation(c_rows, w_mod, b_mod)

    tabs = _rope_tables(n_s)
    x = jnp.concatenate([x_prompt.reshape(t_p, d), x_sample.reshape(t_s, d)], axis=0)
    n_assign = t * TOP_K
    n_blocks = n_assign // MB + N_EXPERTS
    p_rows = n_blocks * MB

    new_ckv, new_krope = [], []
    for l in range(depth):
        wl = _layer_weights(l, w_in, g_q_norm, w_uq, g_kv_norm, w_uk, w_uv, w_sconv, w_cconv, b_cconv,
                            g_cf_norm, b_cf_norm, w_out, g_ln1, b_ln1, g_ln2, b_ln2, w_router, b_router,
                            w_exp_in, b_exp_in, w_exp_out, b_exp_out)
        mod = mod_all[l].reshape(rows, 1, 6 * d)
        q, ckv, krope, kfeat, conv_in = _in_proj(x, mod, wl, tabs, n_p_tiles, tps, ctx_row)
        new_ckv.append(ckv[:t_p].reshape(bp, n_p, KV_RANK))
        new_krope.append(krope[:t_p].reshape(bp, n_p, QK_ROPE))

        kcache = jnp.concatenate(
            [cache_ckv[:, l], cache_krope[:, l], jnp.zeros((bs, past, KFEAT - KV_RANK - QK_ROPE), F32)],
            axis=-1).astype(BF16)
        attn_p = _attention(q, kfeat, None, wl, bp, n_p, 0)
        attn_s = _attention(q, kfeat, kcache, wl, bs, n_s, t_p)
        attn = jnp.concatenate([attn_p, attn_s], axis=1)

        x1, h2, idx, rank, gates, counts = _mix(x, attn, conv_in, mod, wl, n_p_tiles, tps, ctx_row, alpha)

        cnt = counts[0, :N_EXPERTS].astype(jnp.int32)
        padded = ((cnt + MB - 1) // MB) * MB
        pend = jnp.cumsum(padded)
        pstart = pend - padded
        dest = (pstart[idx[:, :TOP_K]] + rank[:, :TOP_K]).reshape(n_assign)
        blk_expert = jnp.minimum(
            jnp.searchsorted(pend, jnp.arange(n_blocks, dtype=jnp.int32) * MB, side="right"),
            N_EXPERTS - 1).astype(jnp.int32)
        n_used = (pend[-1:] // MB).astype(jnp.int32)

        xs = _dispatch(dest, h2, p_rows)
        ys = _experts(blk_expert, n_used, xs, wl)
        x = _combine(dest, gates, x1, mod, ys, wl, n_p_tiles, tps, ctx_row, alpha)

    return (x[:t_p].reshape(bp, n_p, d), x[t_p:].reshape(bs, n_s, d),
            jnp.stack(new_ckv, axis=1), jnp.stack(new_krope, axis=1))
```

```python
import functools
import math

import jax
import jax.numpy as jnp
from jax import lax
from jax.experimental import pallas as pl
from jax.experimental.pallas import tpu as pltpu

F32 = jnp.float32
BF16 = jnp.bfloat16

D_MODEL = 1024
GRID_W = 64
HEADS = 8
QK_NOPE = 64
QK_ROPE = 32
QK_DIM = QK_NOPE + QK_ROPE
V_DIM = 64
Q_RANK = 192
KV_RANK = 128
SC_DIM = 256
SC_WIDTH = 3
CF_DIM = 256
CF_WIDTH = 31
N_EXPERTS = 32
TOP_K = 4
D_EXPERT = 1024
SWIGLU_LIMIT = 7.0
SWIGLU_ALPHA = 1.702
ROPE_BASE = 10000.0
EPS = 1e-6

LANES = 128
SUBLANES = 8
ROW_TILES = D_MODEL // LANES

TM = 256
TQ = 256
MB = 256
HALO = 16
HEAD_PAD = LANES
Q_RANK_PAD = 256
KFEAT = 256
COL_CQ, COL_CKV, COL_KR, COL_BG, COL_CG, COL_XS, COL_GA, COL_GB, IN_COLS_PAD = (
    0, 256, 384, 512, 768, 1024, 1280, 1536, 1792)
VMEM_LIMIT = 56 * 1024 * 1024


def _cparams(*sem):
    return pltpu.CompilerParams(dimension_semantics=sem, vmem_limit_bytes=VMEM_LIMIT)


def _ln_plain(x):
    mu = jnp.mean(x, axis=-1, keepdims=True)
    xc = x - mu
    var = jnp.mean(xc * xc, axis=-1, keepdims=True)
    return xc * lax.rsqrt(var + EPS)


def _mod_kernel(c_ref, w_ref, b_ref, o_ref):
    c = c_ref[...]
    s = c * jax.nn.sigmoid(c)
    o_ref[0] = jnp.dot(s, w_ref[0], preferred_element_type=F32,
                       precision=lax.Precision.HIGHEST) + b_ref[0]


def _modulation(c_rows, w_mod, b_mod):
    depth = w_mod.shape[0]
    rows = c_rows.shape[0]
    nblk = w_mod.shape[2] // D_MODEL
    return pl.pallas_call(
        _mod_kernel,
        grid=(depth, nblk),
        in_specs=[pl.BlockSpec((rows, D_MODEL), lambda l, j: (0, 0)),
                  pl.BlockSpec((1, D_MODEL, D_MODEL), lambda l, j: (l, 0, j)),
                  pl.BlockSpec((1, 1, D_MODEL), lambda l, j: (l, 0, j))],
        out_specs=pl.BlockSpec((1, rows, D_MODEL), lambda l, j: (l, 0, j)),
        out_shape=jax.ShapeDtypeStruct((depth, rows, w_mod.shape[2]), F32),
        compiler_params=_cparams("arbitrary", "arbitrary"),
        name="modulation",
    )(c_rows, w_mod, b_mod.reshape(depth, 1, -1))


def _in_proj_kernel(n_p_tiles, x_ref, mod_ref, win_ref, gq_ref, gkv_ref, wqa_ref, wqb_ref,
                    tcq_ref, tsq_ref, tck_ref, tsk_ref,
                    q_ref, ckv_ref, kr_ref, kf_ref, cv_ref):
    is_s = pl.program_id(0) >= n_p_tiles
    mod = mod_ref[0]
    h = _ln_plain(x_ref[...]) * (1.0 + mod[:, D_MODEL:2 * D_MODEL]) + mod[:, 0:D_MODEL]
    z = jnp.dot(h.astype(BF16), win_ref[...], preferred_element_type=F32)

    cq = z[:, COL_CQ:COL_CQ + Q_RANK_PAD]
    cqn = cq * lax.rsqrt(jnp.sum(cq * cq, axis=-1, keepdims=True) * (1.0 / Q_RANK) + EPS) * gq_ref[...]
    cqn = cqn.astype(BF16)
    qa = jnp.dot(cqn, wqa_ref[...], preferred_element_type=F32)
    qb = jnp.dot(cqn, wqb_ref[...], preferred_element_type=F32)
    cos_q = jnp.where(is_s, tcq_ref[...], 1.0)
    sin_q = jnp.where(is_s, tsq_ref[...], 0.0)
    scale = 1.0 / math.sqrt(QK_DIM)
    for hd in range(HEADS):
        sl = slice(hd * HEAD_PAD, (hd + 1) * HEAD_PAD)
        q_ref[hd] = ((qa[:, sl] * cos_q + qb[:, sl] * sin_q) * scale).astype(BF16)

    cr = z[:, COL_CKV:COL_CKV + KV_RANK]
    ckv = cr * lax.rsqrt(jnp.mean(cr * cr, axis=-1, keepdims=True) + EPS) * gkv_ref[...]
    ckv_ref[...] = ckv
    zk = z[:, COL_KR:COL_KR + LANES]
    kr_ref[...] = zk[:, 0:QK_ROPE]
    lane = lax.broadcasted_iota(jnp.int32, zk.shape, 1)
    raw = jnp.where(lane < QK_ROPE, zk, 0.0)
    roped = zk * tck_ref[...] + pltpu.roll(zk, LANES - QK_ROPE, 1) * tsk_ref[...]
    kf_ref[:, 0:KV_RANK] = ckv.astype(BF16)
    kf_ref[:, KV_RANK:KFEAT] = jnp.where(is_s, roped, raw).astype(BF16)

    cv_ref[:, 0:SC_DIM] = z[:, COL_BG:COL_BG + SC_DIM]
    cv_ref[:, SC_DIM:2 * SC_DIM] = z[:, COL_CG:COL_CG + SC_DIM] * z[:, COL_XS:COL_XS + SC_DIM]
    cv_ref[:, 2 * SC_DIM:] = z[:, COL_GA:COL_GA + CF_DIM] * jax.nn.sigmoid(z[:, COL_GB:COL_GB + CF_DIM])


def _in_proj(x, mod, wl, tabs, n_p_tiles, tiles_per_s_seq, ctx_row):
    t = x.shape[0]
    nt = t // TM

    def mod_idx(i):
        return (jnp.where(i < n_p_tiles, ctx_row, (i - n_p_tiles) // tiles_per_s_seq), 0, 0)

    def pos_idx(i):
        return (jnp.where(i < n_p_tiles, 0, (i - n_p_tiles) % tiles_per_s_seq), 0)

    full = lambda shp: pl.BlockSpec(shp, lambda i: (0,) * len(shp))
    tab = pl.BlockSpec((TM, LANES), pos_idx)
    return pl.pallas_call(
        functools.partial(_in_proj_kernel, n_p_tiles),
        grid=(nt,),
        in_specs=[pl.BlockSpec((TM, D_MODEL), lambda i: (i, 0)),
                  pl.BlockSpec((1, 1, 6 * D_MODEL), mod_idx),
                  full((D_MODEL, IN_COLS_PAD)),
                  full((1, Q_RANK_PAD)), full((1, KV_RANK)),
                  full((Q_RANK_PAD, HEADS * HEAD_PAD)), full((Q_RANK_PAD, HEADS * HEAD_PAD)),
                  tab, tab, tab, tab],
        out_specs=[pl.BlockSpec((HEADS, TM, HEAD_PAD), lambda i: (0, i, 0)),
                   pl.BlockSpec((TM, KV_RANK), lambda i: (i, 0)),
                   pl.BlockSpec((TM, QK_ROPE), lambda i: (i, 0)),
                   pl.BlockSpec((TM, KFEAT), lambda i: (i, 0)),
                   pl.BlockSpec((TM, 3 * SC_DIM), lambda i: (i, 0))],
        out_shape=[jax.ShapeDtypeStruct((HEADS, t, HEAD_PAD), BF16),
                   jax.ShapeDtypeStruct((t, KV_RANK), F32),
                   jax.ShapeDtypeStruct((t, QK_ROPE), F32),
                   jax.ShapeDtypeStruct((t, KFEAT), BF16),
                   jax.ShapeDtypeStruct((t, 3 * SC_DIM), F32)],
        compiler_params=_cparams("arbitrary"),
        name="in_proj",
    )(x, mod, wl["w_in"], wl["g_q"], wl["g_kv"], wl["wq_a"], wl["wq_b"], *tabs)


def _attn_kernel(n_own, n_cache, q_ref, kf_ref, *rest):
    if n_cache:
        kc_ref, wkt_ref, wv_ref, o_ref, kt_scr, v_scr = rest
    else:
        wkt_ref, wv_ref, o_ref, kt_scr, v_scr = rest

    @pl.when(pl.program_id(1) == 0)
    def _():
        def fill(feat, lo, n):
            for hd in range(HEADS):
                kt = lax.dot_general(wkt_ref[hd], feat, (((1,), (1,)), ((), ())),
                                     preferred_element_type=F32)
                kt_scr[hd, :, lo:lo + n] = kt.astype(BF16)
                v = jnp.dot(feat, wv_ref[:, hd * LANES:(hd + 1) * LANES], preferred_element_type=F32)
                v_scr[hd, lo:lo + n, :] = v.astype(BF16)

        fill(kf_ref[...], 0, n_own)
        if n_cache:
            fill(kc_ref[0], n_own, n_cache)

    for pair in range(HEADS // 2):
        acc = None
        for hd in (2 * pair, 2 * pair + 1):
            s = jnp.dot(q_ref[hd], kt_scr[hd], preferred_element_type=F32)
            p = jnp.exp(s - jnp.max(s, axis=-1, keepdims=True))
            l = jnp.sum(p, axis=-1, keepdims=True)
            o = jnp.dot(p.astype(BF16), v_scr[hd], preferred_element_type=F32) / l
            acc = o if acc is None else acc + o
        o_ref[pair] = acc.astype(BF16)


def _attention(q, kfeat, kcache, wl, batch, n_own, row0):
    n_cache = 0 if kcache is None else kcache.shape[1]
    m = n_own + n_cache
    tq = min(TQ, n_own)
    nq = n_own // tq
    q0 = row0 // tq
    k0 = row0 // n_own
    in_specs = [pl.BlockSpec((HEADS, tq, HEAD_PAD), lambda b, j: (0, q0 + b * nq + j, 0)),
                pl.BlockSpec((n_own, KFEAT), lambda b, j: (k0 + b, 0))]
    args = [q, kfeat]
    if n_cache:
        in_specs.append(pl.BlockSpec((1, n_cache, KFEAT), lambda b, j: (b, 0, 0)))
        args.append(kcache)
    in_specs += [pl.BlockSpec((HEADS, HEAD_PAD, KFEAT), lambda b, j: (0, 0, 0)),
                 pl.BlockSpec((KFEAT, HEADS * LANES), lambda b, j: (0, 0))]
    args += [wl["wk_t"], wl["wv"]]
    return pl.pallas_call(
        functools.partial(_attn_kernel, n_own, n_cache),
        grid=(batch, nq),
        in_specs=in_specs,
        out_specs=pl.BlockSpec((HEADS // 2, tq, LANES), lambda b, j: (0, b * nq + j, 0)),
        out_shape=jax.ShapeDtypeStruct((HEADS // 2, batch * n_own, LANES), BF16),
        scratch_shapes=[pltpu.VMEM((HEADS, HEAD_PAD, m), BF16), pltpu.VMEM((HEADS, m, LANES), BF16)],
        compiler_params=_cparams("arbitrary", "arbitrary"),
        name="attention_cache" if n_cache else "attention_ctx",
    )(*args)


def _mix_kernel(n_p_tiles, tiles_per_s_seq, alpha,
                x_ref, at_ref, cprev_ref, ccur_ref, cnext_ref, mod_ref,
                wsc_ref, wcc_ref, bcc_ref, gcf_ref, bcf_ref, wout_ref, g1_ref, b1_ref,
                wr_ref, br_ref,
                x1_ref, h2_ref, idx_ref, rank_ref, gate_ref, cnt_ref,
                ext_scr, carry_scr):
    i = pl.program_id(0)
    is_s = i >= n_p_tiles
    jj = (i - n_p_tiles) % tiles_per_s_seq
    has_prev = jnp.logical_and(is_s, jj > 0)
    has_next = jnp.logical_and(is_s, jj < tiles_per_s_seq - 1)

    @pl.when(i == 0)
    def _():
        carry_scr[...] = jnp.zeros_like(carry_scr)

    ext_scr[0:HALO, :] = jnp.where(has_prev, cprev_ref[:, SC_DIM:], 0.0)
    ext_scr[HALO:HALO + TM, :] = ccur_ref[:, SC_DIM:]
    ext_scr[HALO + TM:, :] = jnp.where(has_next, cnext_ref[:, SC_DIM:], 0.0)

    sconv = jnp.zeros((TM, SC_DIM), F32)
    for k in range(SC_WIDTH):
        o = HALO - SC_WIDTH // 2 + k
        sconv = sconv + ext_scr[o:o + TM, 0:SC_DIM] * wsc_ref[k:k + 1, :]
    sc_out = ccur_ref[:, 0:SC_DIM] * sconv

    cconv = jnp.zeros((TM, CF_DIM), F32)
    for k in range(CF_WIDTH):
        o = HALO - CF_WIDTH // 2 + k
        cconv = cconv + ext_scr[o:o + TM, SC_DIM:] * wcc_ref[k:k + 1, :]
    cf = _ln_plain(cconv + bcc_ref[...]) * gcf_ref[...] + bcf_ref[...]
    cf = cf * jax.nn.sigmoid(cf)

    mix_in = jnp.concatenate([at_ref[0], at_ref[1], at_ref[2], at_ref[3],
                              sc_out.astype(BF16), cf.astype(BF16)], axis=1)
    mix = jnp.dot(mix_in, wout_ref[...], preferred_element_type=F32)

    mod = mod_ref[0]
    g1 = mod[:, 2 * D_MODEL:3 * D_MODEL]
    sh2 = mod[:, 3 * D_MODEL:4 * D_MODEL]
    sc2 = mod[:, 4 * D_MODEL:5 * D_MODEL]
    x1 = _ln_plain(alpha * x_ref[...] + g1 * mix) * g1_ref[...] + b1_ref[...]
    x1_ref[...] = x1
    h2 = _ln_plain(x1) * (1.0 + sc2) + sh2
    for j in range(ROW_TILES):
        h2_ref[:, j, :] = h2[:, j * LANES:(j + 1) * LANES]

    logits = jnp.dot(h2, wr_ref[...], preferred_element_type=F32,
                     precision=lax.Precision.HIGHEST) + br_ref[...]
    lane = lax.broadcasted_iota(jnp.int32, (TM, LANES), 1)
    lane_f = lane.astype(F32)
    neg = jnp.float32(-jnp.inf)
    cur = jnp.where(lane < N_EXPERTS, logits, neg)
    vals, hots, idxs = [], [], []
    for _ in range(TOP_K):
        mx = jnp.max(cur, axis=-1, keepdims=True)
        ik = jnp.min(jnp.where(cur == mx, lane_f, float(LANES)), axis=-1, keepdims=True)
        ik = jnp.minimum(ik, float(N_EXPERTS - 1))
        hot = lane_f == ik
        cur = jnp.where(hot, neg, cur)
        vals.append(mx)
        hots.append(hot)
        idxs.append(ik.astype(jnp.int32))
    exps = [jnp.exp(v - vals[0]) for v in vals]
    denom = exps[0] + exps[1] + exps[2] + exps[3]
    sel = jnp.where(hots[0] | hots[1] | hots[2] | hots[3], 1.0, 0.0)

    r_i = lax.broadcasted_iota(jnp.int32, (TM, TM), 0)
    c_i = lax.broadcasted_iota(jnp.int32, (TM, TM), 1)
    tri = jnp.where(c_i < r_i, 1.0, 0.0).astype(BF16)
    rank_all = jnp.dot(tri, sel.astype(BF16), preferred_element_type=F32) + carry_scr[0:1, :]
    new_carry = carry_scr[0:1, :] + jnp.sum(sel, axis=0, keepdims=True)
    carry_scr[...] = jnp.broadcast_to(new_carry, carry_scr.shape)
    cnt_ref[...] = jnp.broadcast_to(new_carry, cnt_ref.shape)

    idx_o = jnp.zeros((TM, LANES), jnp.int32)
    rank_o = jnp.zeros((TM, LANES), jnp.int32)
    gate_o = jnp.zeros((TM, LANES), F32)
    for k in range(TOP_K):
        rk = jnp.sum(jnp.where(hots[k], rank_all, 0.0), axis=-1, keepdims=True).astype(jnp.int32)
        idx_o = jnp.where(lane == k, idxs[k], idx_o)
        rank_o = jnp.where(lane == k, rk, rank_o)
        gate_o = jnp.where(lane == k, exps[k] / denom, gate_o)
    idx_ref[...] = idx_o
    rank_ref[...] = rank_o
    gate_ref[...] = gate_o


def _mix(x, attn, conv_in, mod, wl, n_p_tiles, tiles_per_s_seq, ctx_row, alpha):
    t = x.shape[0]
    nt = t // TM
    hpt = TM // HALO
    n_halo = t // HALO

    def mod_idx(i):
        return (jnp.where(i < n_p_tiles, ctx_row, (i - n_p_tiles) // tiles_per_s_seq), 0, 0)

    full = lambda shp: pl.BlockSpec(shp, lambda i: (0,) * len(shp))
    row = lambda w: pl.BlockSpec((TM, w), lambda i: (i, 0))
    return pl.pallas_call(
        functools.partial(_mix_kernel, n_p_tiles, tiles_per_s_seq, alpha),
        grid=(nt,),
        in_specs=[row(D_MODEL),
                  pl.BlockSpec((HEADS // 2, TM, LANES), lambda i: (0, i, 0)),
                  pl.BlockSpec((HALO, 3 * SC_DIM), lambda i: (jnp.maximum(i * hpt - 1, 0), 0)),
                  row(3 * SC_DIM),
                  pl.BlockSpec((HALO, 3 * SC_DIM), lambda i: (jnp.minimum((i + 1) * hpt, n_halo - 1), 0)),
                  pl.BlockSpec((1, 1, 6 * D_MODEL), mod_idx),
                  full((SC_WIDTH, SC_DIM)), full((CF_WIDTH, CF_DIM)), full((1, CF_DIM)),
                  full((1, CF_DIM)), full((1, CF_DIM)),
                  full((D_MODEL, D_MODEL)), full((1, D_MODEL)), full((1, D_MODEL)),
                  full((D_MODEL, LANES)), full((1, LANES))],
        out_specs=[row(D_MODEL),
                   pl.BlockSpec((TM, ROW_TILES, LANES), lambda i: (i, 0, 0)),
                   row(LANES), row(LANES), row(LANES),
                   pl.BlockSpec((SUBLANES, LANES), lambda i: (0, 0))],
        out_shape=[jax.ShapeDtypeStruct((t, D_MODEL), F32),
                   jax.ShapeDtypeStruct((t, ROW_TILES, LANES), F32),
                   jax.ShapeDtypeStruct((t, LANES), jnp.int32),
                   jax.ShapeDtypeStruct((t, LANES), jnp.int32),
                   jax.ShapeDtypeStruct((t, LANES), F32),
                   jax.ShapeDtypeStruct((SUBLANES, LANES), F32)],
        scratch_shapes=[pltpu.VMEM((TM + 2 * HALO, 2 * SC_DIM), F32), pltpu.VMEM((SUBLANES, LANES), F32)],
        compiler_params=_cparams("arbitrary"),
        name="mix",
    )(x, attn, conv_in, conv_in, conv_in, mod,
      wl["w_sconv"], wl["w_cconv"], wl["b_cconv"], wl["g_cf"], wl["b_cf"],
      wl["w_out"], wl["g_ln1"], wl["b_ln1"], wl["w_router"], wl["b_router"])


def _dispatch_kernel(dest_ref, h_ref, xs_in_hbm, xs_hbm, sem):
    del xs_in_hbm

    def issue(t, carry):
        for k in range(TOP_K):
            pltpu.make_async_copy(h_ref.at[t], xs_hbm.at[dest_ref[TOP_K * t + k]], sem).start()
        return carry

    lax.fori_loop(0, TM, issue, 0)
    for _ in range(TOP_K):
        pltpu.make_async_copy(h_ref, xs_hbm.at[pl.ds(0, TM)], sem).wait()


def _dispatch(dest_flat, h_rows, p_rows):
    t = h_rows.shape[0]
    xs0 = jnp.zeros((p_rows, ROW_TILES, LANES), F32)
    return pl.pallas_call(
        _dispatch_kernel,
        grid=(t // TM,),
        in_specs=[pl.BlockSpec((TM * TOP_K,), lambda i: (i,), memory_space=pltpu.SMEM),
                  pl.BlockSpec((TM, ROW_TILES, LANES), lambda i: (i, 0, 0)),
                  pl.BlockSpec(memory_space=pl.ANY)],
        out_specs=pl.BlockSpec(memory_space=pl.ANY),
        out_shape=jax.ShapeDtypeStruct((p_rows, ROW_TILES, LANES), F32),
        scratch_shapes=[pltpu.SemaphoreType.DMA(())],
        input_output_aliases={2: 0},
        compiler_params=_cparams("arbitrary"),
        name="dispatch",
    )(dest_flat, h_rows, xs0)


def _expert_kernel(be_ref, nu_ref, xs_ref, w1_ref, b1_ref, w2_ref, b2_ref, ys_ref, w1_scr, w2_scr):
    i = pl.program_id(0)
    used = i < nu_ref[0]

    @pl.when(jnp.logical_and(used, jnp.logical_or(i == 0, be_ref[i] != be_ref[jnp.maximum(i - 1, 0)])))
    def _():
        w1_scr[...] = w1_ref[0, 0].astype(BF16)
        w2_scr[...] = w2_ref[0, 0].astype(BF16)

    @pl.when(used)
    def _():
        x = jnp.concatenate([xs_ref[:, j, :] for j in range(ROW_TILES)], axis=1).astype(BF16)
        z = jnp.dot(x, w1_scr[...], preferred_element_type=F32) + b1_ref[0, 0]
        gate = jnp.minimum(z[:, :D_EXPERT], SWIGLU_LIMIT)
        up = jnp.clip(z[:, D_EXPERT:], -SWIGLU_LIMIT, SWIGLU_LIMIT)
        act = (up + 1.0) * gate * jax.nn.sigmoid(SWIGLU_ALPHA * gate)
        y = jnp.dot(act.astype(BF16), w2_scr[...], preferred_element_type=F32) + b2_ref[0, 0]
        for j in range(ROW_TILES):
            ys_ref[:, j, :] = y[:, j * LANES:(j + 1) * LANES]

    @pl.when(i >= nu_ref[0])
    def _():
        ys_ref[...] = jnp.zeros_like(ys_ref)


def _experts(blk_expert, n_used, xs, l, w_exp_in, b_exp_in, w_exp_out, b_exp_out):
    p_rows = xs.shape[0]
    nb = p_rows // MB
    depth = w_exp_in.shape[0]
    wmap = lambda i, be, nu: (l, be[i], 0, 0)
    grid_spec = pltpu.PrefetchScalarGridSpec(
        num_scalar_prefetch=2,
        grid=(nb,),
        in_specs=[pl.BlockSpec((MB, ROW_TILES, LANES), lambda i, be, nu: (i, 0, 0)),
                  pl.BlockSpec((1, 1, D_MODEL, 2 * D_EXPERT), wmap),
                  pl.BlockSpec((1, 1, 1, 2 * D_EXPERT), wmap),
                  pl.BlockSpec((1, 1, D_EXPERT, D_MODEL), wmap),
                  pl.BlockSpec((1, 1, 1, D_MODEL), wmap)],
        out_specs=pl.BlockSpec((MB, ROW_TILES, LANES), lambda i, be, nu: (i, 0, 0)),
        scratch_shapes=[pltpu.VMEM((D_MODEL, 2 * D_EXPERT), BF16), pltpu.VMEM((D_EXPERT, D_MODEL), BF16)],
    )
    return pl.pallas_call(
        _expert_kernel,
        grid_spec=grid_spec,
        out_shape=jax.ShapeDtypeStruct((p_rows, ROW_TILES, LANES), F32),
        compiler_params=_cparams("arbitrary"),
        name="experts",
    )(blk_expert, n_used, xs, w_exp_in, b_exp_in.reshape(depth, N_EXPERTS, 1, -1),
      w_exp_out, b_exp_out.reshape(depth, N_EXPERTS, 1, -1))


def _combine_kernel(alpha, dest_ref, gate_ref, x1_ref, mod_ref, g2_ref, b2_ref, ys_hbm, o_ref, buf, sem):
    def issue(t, carry):
        for k in range(TOP_K):
            pltpu.make_async_copy(ys_hbm.at[dest_ref[TOP_K * t + k]], buf.at[k * TM + t], sem).start()
        return carry

    lax.fori_loop(0, TM, issue, 0)
    pltpu.make_async_copy(ys_hbm.at[pl.ds(0, TM * TOP_K)], buf, sem).wait()

    gates = gate_ref[...]
    ffn = None
    for k in range(TOP_K):
        yk = jnp.concatenate([buf[k * TM:(k + 1) * TM, j, :] for j in range(ROW_TILES)], axis=1)
        term = yk * gates[:, k:k + 1]
        ffn = term if ffn is None else ffn + term
    g2 = mod_ref[0][:, 5 * D_MODEL:6 * D_MODEL]
    o_ref[...] = _ln_plain(alpha * x1_ref[...] + g2 * ffn) * g2_ref[...] + b2_ref[...]


def _combine(dest_flat, gates, x1, mod, ys, wl, n_p_tiles, tiles_per_s_seq, ctx_row, alpha):
    t = x1.shape[0]

    def mod_idx(i):
        return (jnp.where(i < n_p_tiles, ctx_row, (i - n_p_tiles) // tiles_per_s_seq), 0, 0)

    return pl.pallas_call(
        functools.partial(_combine_kernel, alpha),
        grid=(t // TM,),
        in_specs=[pl.BlockSpec((TM * TOP_K,), lambda i: (i,), memory_space=pltpu.SMEM),
                  pl.BlockSpec((TM, LANES), lambda i: (i, 0)),
                  pl.BlockSpec((TM, D_MODEL), lambda i: (i, 0)),
                  pl.BlockSpec((1, 1, 6 * D_MODEL), mod_idx),
                  pl.BlockSpec((1, D_MODEL), lambda i: (0, 0)),
                  pl.BlockSpec((1, D_MODEL), lambda i: (0, 0)),
                  pl.BlockSpec(memory_space=pl.ANY)],
        out_specs=pl.BlockSpec((TM, D_MODEL), lambda i: (i, 0)),
        out_shape=jax.ShapeDtypeStruct((t, D_MODEL), F32),
        scratch_shapes=[pltpu.VMEM((TM * TOP_K, ROW_TILES, LANES), F32), pltpu.SemaphoreType.DMA(())],
        compiler_params=_cparams("arbitrary"),
        name="combine",
    )(dest_flat, gates, x1, mod, wl["g_ln2"], wl["b_ln2"], ys)


def _pad_cols(w, width):
    return jnp.pad(w, ((0, 0), (0, width - w.shape[1])))


def _rot_cols(w):
    half = w.shape[1] // 2
    return jnp.concatenate([-w[:, half:], w[:, :half]], axis=1)


def _layer_weights(l, w_in, g_q_norm, w_uq, g_kv_norm, w_uk, w_uv, w_sconv, w_cconv, b_cconv,
                   g_cf_norm, b_cf_norm, w_out, g_ln1, b_ln1, g_ln2, b_ln2, w_router, b_router):
    wi = w_in[l]
    o_ckv, o_kr, o_bg = Q_RANK, Q_RANK + KV_RANK, Q_RANK + KV_RANK + QK_ROPE
    w_kr = wi[:, o_kr:o_bg]
    w_in_p = jnp.concatenate([
        _pad_cols(wi[:, :Q_RANK], COL_CKV - COL_CQ),
        wi[:, o_ckv:o_kr],
        _pad_cols(jnp.concatenate([w_kr, _rot_cols(w_kr)], axis=1), COL_BG - COL_KR),
        wi[:, o_bg:]], axis=1).astype(BF16)

    uq = jnp.pad(w_uq[l], ((0, Q_RANK_PAD - Q_RANK), (0, 0))).reshape(Q_RANK_PAD, HEADS, QK_DIM)
    zer = jnp.zeros((Q_RANK_PAD, HEADS, HEAD_PAD - QK_DIM), F32)
    wq_a = jnp.concatenate([uq, zer], axis=2).reshape(Q_RANK_PAD, HEADS * HEAD_PAD).astype(BF16)
    rot = jnp.concatenate([-uq[:, :, QK_NOPE + QK_ROPE // 2:], uq[:, :, QK_NOPE:QK_NOPE + QK_ROPE // 2]], axis=2)
    wq_b = jnp.concatenate([jnp.zeros((Q_RANK_PAD, HEADS, QK_NOPE), F32), rot, zer],
                           axis=2).reshape(Q_RANK_PAD, HEADS * HEAD_PAD).astype(BF16)

    uk_t = w_uk[l].reshape(KV_RANK, HEADS, QK_NOPE).transpose(1, 2, 0)
    top = jnp.pad(uk_t, ((0, 0), (0, 0), (0, KFEAT - KV_RANK)))
    eye = jnp.pad(jnp.eye(QK_ROPE, dtype=F32), ((0, 0), (KV_RANK, KFEAT - KV_RANK - QK_ROPE)))
    mid = jnp.broadcast_to(eye[None], (HEADS, QK_ROPE, KFEAT))
    wk_t = jnp.concatenate([top, mid, jnp.zeros((HEADS, HEAD_PAD - QK_DIM, KFEAT), F32)], axis=1).astype(BF16)

    uv = w_uv[l].reshape(KV_RANK, HEADS, V_DIM)
    zv = jnp.zeros((KV_RANK, HEADS, V_DIM), F32)
    odd = (jnp.arange(HEADS) % 2 == 1)[None, :, None]
    wv = jnp.concatenate([jnp.where(odd, zv, uv), jnp.where(odd, uv, zv)], axis=2).reshape(KV_RANK, HEADS * LANES)
    wv = jnp.pad(wv, ((0, KFEAT - KV_RANK), (0, 0))).astype(BF16)

    r1 = lambda v: v.reshape(1, -1)
    return {
        "w_in": w_in_p,
        "g_q": jnp.pad(g_q_norm[l], (0, Q_RANK_PAD - Q_RANK)).reshape(1, -1),
        "g_kv": r1(g_kv_norm[l]),
        "wq_a": wq_a, "wq_b": wq_b, "wk_t": wk_t, "wv": wv,
        "w_sconv": w_sconv[l], "w_cconv": w_cconv[l], "b_cconv": r1(b_cconv[l]),
        "g_cf": r1(g_cf_norm[l]), "b_cf": r1(b_cf_norm[l]),
        "w_out": w_out[l].astype(BF16),
        "g_ln1": r1(g_ln1[l]), "b_ln1": r1(b_ln1[l]), "g_ln2": r1(g_ln2[l]), "b_ln2": r1(b_ln2[l]),
        "w_router": _pad_cols(w_router[l], LANES),
        "b_router": jnp.pad(b_router[l], (0, LANES - N_EXPERTS)).reshape(1, -1),
    }


def _rope_tables(n_lat):
    rows = n_lat // GRID_W
    row = jnp.repeat(jnp.arange(rows, dtype=F32), GRID_W)
    col = jnp.tile(jnp.arange(GRID_W, dtype=F32), rows)
    axis_dim = QK_ROPE // 2
    inv = ROPE_BASE ** (-jnp.arange(0, axis_dim, 2, dtype=F32) / axis_dim)
    ang = jnp.concatenate([row[:, None] * inv, col[:, None] * inv], axis=-1)
    cos, sin = jnp.cos(ang), jnp.sin(ang)
    one = jnp.ones((n_lat, QK_NOPE), F32)
    zq = jnp.zeros((n_lat, HEAD_PAD - QK_DIM), F32)
    zk = jnp.zeros((n_lat, LANES - QK_ROPE), F32)
    return (jnp.concatenate([one, cos, cos, zq], axis=1),
            jnp.concatenate([0.0 * one, sin, sin, zq], axis=1),
            jnp.concatenate([cos, cos, zk], axis=1),
            jnp.concatenate([sin, sin, zk], axis=1))


def kernel(x_prompt, x_sample, cache_ckv, cache_krope, c, c_ctx, w_mod, b_mod, w_in, g_q_norm, w_uq, g_kv_norm, w_uk, w_uv, w_sconv, w_cconv, b_cconv, g_cf_norm, b_cf_norm, w_out, g_ln1, b_ln1, g_ln2, b_ln2, w_router, b_router, w_exp_in, b_exp_in, w_exp_out, b_exp_out):
    bp, n_p, d = x_prompt.shape
    bs, n_s, _ = x_sample.shape
    depth = w_mod.shape[0]
    past = cache_ckv.shape[2]
    assert d == D_MODEL and n_p == TM and n_s % TM == 0 and n_s % GRID_W == 0
    t_p, t_s = bp * n_p, bs * n_s
    assert t_p % n_s == 0 and past % LANES == 0
    t = t_p + t_s
    n_p_tiles = t_p // TM
    tps = n_s // TM
    alpha = float((2 * depth) ** 0.25)

    ctx_row = bs
    rows = -(-(bs + 1) // SUBLANES) * SUBLANES
    c_rows = jnp.concatenate([c, c_ctx[None], jnp.zeros((rows - bs - 1, d), F32)], axis=0)
    mod_all = _modulation(c_rows, w_mod, b_mod)

    tabs = _rope_tables(n_s)
    x = jnp.concatenate([x_prompt.reshape(t_p, d), x_sample.reshape(t_s, d)], axis=0)
    n_assign = t * TOP_K
    n_blocks = n_assign // MB + N_EXPERTS
    p_rows = n_blocks * MB

    new_ckv, new_krope = [], []
    for l in range(depth):
        wl = _layer_weights(l, w_in, g_q_norm, w_uq, g_kv_norm, w_uk, w_uv, w_sconv, w_cconv, b_cconv,
                            g_cf_norm, b_cf_norm, w_out, g_ln1, b_ln1, g_ln2, b_ln2, w_router, b_router)
        mod = mod_all[l].reshape(rows, 1, 6 * d)
        q, ckv, krope, kfeat, conv_in = _in_proj(x, mod, wl, tabs, n_p_tiles, tps, ctx_row)
        new_ckv.append(ckv[:t_p].reshape(bp, n_p, KV_RANK))
        new_krope.append(krope[:t_p].reshape(bp, n_p, QK_ROPE))

        kcache = jnp.concatenate(
            [cache_ckv[:, l], cache_krope[:, l], jnp.zeros((bs, past, KFEAT - KV_RANK - QK_ROPE), F32)],
            axis=-1).astype(BF16)
        attn_p = _attention(q, kfeat, None, wl, bp, n_p, 0)
        attn_s = _attention(q, kfeat, kcache, wl, bs, n_s, t_p)
        attn = jnp.concatenate([attn_p, attn_s], axis=1)

        x1, h2, idx, rank, gates, counts = _mix(x, attn, conv_in, mod, wl, n_p_tiles, tps, ctx_row, alpha)

        cnt = counts[0, :N_EXPERTS].astype(jnp.int32)
        padded = ((cnt + MB - 1) // MB) * MB
        pend = jnp.cumsum(padded)
        pstart = pend - padded
        dest = (pstart[idx[:, :TOP_K]] + rank[:, :TOP_K]).reshape(n_assign)
        blk_row0 = jnp.arange(n_blocks, dtype=jnp.int32) * MB
        blk_expert = jnp.minimum(
            jnp.sum((pend[None, :] <= blk_row0[:, None]).astype(jnp.int32), axis=1), N_EXPERTS - 1)
        n_used = (pend[-1:] // MB).astype(jnp.int32)

        xs = _dispatch(dest, h2, p_rows)
        ys = _experts(blk_expert, n_used, xs, l, w_exp_in, b_exp_in, w_exp_out, b_exp_out)
        x = _combine(dest, gates, x1, mod, ys, wl, n_p_tiles, tps, ctx_row, alpha)

    return (x[:t_p].reshape(bp, n_p, d), x[t_p:].reshape(bs, n_s, d),
            jnp.stack(new_ckv, axis=1), jnp.stack(new_krope, axis=1))
```

```python
import functools
import math

import jax
import jax.numpy as jnp
from jax import lax
from jax.experimental import pallas as pl
from jax.experimental.pallas import tpu as pltpu

F32 = jnp.float32
BF16 = jnp.bfloat16

D_MODEL = 1024
GRID_W = 64
HEADS = 8
QK_NOPE = 64
QK_ROPE = 32
QK_DIM = QK_NOPE + QK_ROPE
V_DIM = 64
Q_RANK = 192
KV_RANK = 128
SC_DIM = 256
SC_WIDTH = 3
CF_DIM = 256
CF_WIDTH = 31
N_EXPERTS = 32
TOP_K = 4
D_EXPERT = 1024
SWIGLU_LIMIT = 7.0
SWIGLU_ALPHA = 1.702
ROPE_BASE = 10000.0
EPS = 1e-6

LANES = 128
SUBLANES = 8

TM = 256
TQ = 256
MB = 256
HALO = 16
SEG = SUBLANES
SEG_SIZES = (256, 128, 64, 32, 16, 8)
R_TILE = TM * TOP_K + N_EXPERTS * SEG
HEAD_PAD = LANES
Q_RANK_PAD = 256
KFEAT = 256
COL_CQ, COL_CKV, COL_KR, COL_BG, COL_CG, COL_XS, COL_GA, COL_GB, IN_COLS_PAD = (
    0, 256, 384, 512, 768, 1024, 1280, 1536, 1792)
VMEM_LIMIT = 56 * 1024 * 1024


def _cparams(*sem):
    return pltpu.CompilerParams(dimension_semantics=sem, vmem_limit_bytes=VMEM_LIMIT)


def _ln_plain(x):
    mu = jnp.mean(x, axis=-1, keepdims=True)
    xc = x - mu
    var = jnp.mean(xc * xc, axis=-1, keepdims=True)
    return xc * lax.rsqrt(var + EPS)


def _mod_kernel(c_ref, w_ref, b_ref, o_ref):
    c = c_ref[...]
    s = c * jax.nn.sigmoid(c)
    o_ref[0] = jnp.dot(s, w_ref[0], preferred_element_type=F32,
                       precision=lax.Precision.HIGHEST) + b_ref[0]


def _modulation(c_rows, w_mod, b_mod):
    depth = w_mod.shape[0]
    rows = c_rows.shape[0]
    nblk = w_mod.shape[2] // D_MODEL
    return pl.pallas_call(
        _mod_kernel,
        grid=(depth, nblk),
        in_specs=[pl.BlockSpec((rows, D_MODEL), lambda l, j: (0, 0)),
                  pl.BlockSpec((1, D_MODEL, D_MODEL), lambda l, j: (l, 0, j)),
                  pl.BlockSpec((1, 1, D_MODEL), lambda l, j: (l, 0, j))],
        out_specs=pl.BlockSpec((1, rows, D_MODEL), lambda l, j: (l, 0, j)),
        out_shape=jax.ShapeDtypeStruct((depth, rows, w_mod.shape[2]), F32),
        compiler_params=_cparams("arbitrary", "arbitrary"),
        name="modulation",
    )(c_rows, w_mod, b_mod.reshape(depth, 1, -1))


def _in_proj_kernel(n_p_tiles, x_ref, mod_ref, win_ref, gq_ref, gkv_ref, wqa_ref, wqb_ref,
                    tcq_ref, tsq_ref, tck_ref, tsk_ref,
                    q_ref, ckv_ref, kr_ref, kf_ref, cv_ref):
    is_s = pl.program_id(0) >= n_p_tiles
    mod = mod_ref[0]
    h = _ln_plain(x_ref[...]) * (1.0 + mod[:, D_MODEL:2 * D_MODEL]) + mod[:, 0:D_MODEL]
    z = jnp.dot(h.astype(BF16), win_ref[...], preferred_element_type=F32)

    cq = z[:, COL_CQ:COL_CQ + Q_RANK_PAD]
    cqn = cq * lax.rsqrt(jnp.sum(cq * cq, axis=-1, keepdims=True) * (1.0 / Q_RANK) + EPS) * gq_ref[...]
    cqn = cqn.astype(BF16)
    qa = jnp.dot(cqn, wqa_ref[...], preferred_element_type=F32)
    qb = jnp.dot(cqn, wqb_ref[...], preferred_element_type=F32)
    cos_q = jnp.where(is_s, tcq_ref[...], 1.0)
    sin_q = jnp.where(is_s, tsq_ref[...], 0.0)
    scale = 1.0 / math.sqrt(QK_DIM)
    for hd in range(HEADS):
        sl = slice(hd * HEAD_PAD, (hd + 1) * HEAD_PAD)
        q_ref[hd] = ((qa[:, sl] * cos_q + qb[:, sl] * sin_q) * scale).astype(BF16)

    cr = z[:, COL_CKV:COL_CKV + KV_RANK]
    ckv = cr * lax.rsqrt(jnp.mean(cr * cr, axis=-1, keepdims=True) + EPS) * gkv_ref[...]
    ckv_ref[...] = ckv
    zk = z[:, COL_KR:COL_KR + LANES]
    kr_ref[...] = zk[:, 0:QK_ROPE]
    lane = lax.broadcasted_iota(jnp.int32, zk.shape, 1)
    raw = jnp.where(lane < QK_ROPE, zk, 0.0)
    roped = zk * tck_ref[...] + pltpu.roll(zk, LANES - QK_ROPE, 1) * tsk_ref[...]
    kf_ref[:, 0:KV_RANK] = ckv.astype(BF16)
    kf_ref[:, KV_RANK:KFEAT] = jnp.where(is_s, roped, raw).astype(BF16)

    cv_ref[:, 0:SC_DIM] = z[:, COL_BG:COL_BG + SC_DIM]
    cv_ref[:, SC_DIM:2 * SC_DIM] = z[:, COL_CG:COL_CG + SC_DIM] * z[:, COL_XS:COL_XS + SC_DIM]
    cv_ref[:, 2 * SC_DIM:] = z[:, COL_GA:COL_GA + CF_DIM] * jax.nn.sigmoid(z[:, COL_GB:COL_GB + CF_DIM])


def _in_proj(x, mod, wl, tabs, n_p_tiles, tiles_per_s_seq, ctx_row):
    t = x.shape[0]
    nt = t // TM

    def mod_idx(i):
        return (jnp.where(i < n_p_tiles, ctx_row, (i - n_p_tiles) // tiles_per_s_seq), 0, 0)

    def pos_idx(i):
        return (jnp.where(i < n_p_tiles, 0, (i - n_p_tiles) % tiles_per_s_seq), 0)

    full = lambda shp: pl.BlockSpec(shp, lambda i: (0,) * len(shp))
    tab = pl.BlockSpec((TM, LANES), pos_idx)
    return pl.pallas_call(
        functools.partial(_in_proj_kernel, n_p_tiles),
        grid=(nt,),
        in_specs=[pl.BlockSpec((TM, D_MODEL), lambda i: (i, 0)),
                  pl.BlockSpec((1, 1, 6 * D_MODEL), mod_idx),
                  full((D_MODEL, IN_COLS_PAD)),
                  full((1, Q_RANK_PAD)), full((1, KV_RANK)),
                  full((Q_RANK_PAD, HEADS * HEAD_PAD)), full((Q_RANK_PAD, HEADS * HEAD_PAD)),
                  tab, tab, tab, tab],
        out_specs=[pl.BlockSpec((HEADS, TM, HEAD_PAD), lambda i: (0, i, 0)),
                   pl.BlockSpec((TM, KV_RANK), lambda i: (i, 0)),
                   pl.BlockSpec((TM, QK_ROPE), lambda i: (i, 0)),
                   pl.BlockSpec((TM, KFEAT), lambda i: (i, 0)),
                   pl.BlockSpec((TM, 3 * SC_DIM), lambda i: (i, 0))],
        out_shape=[jax.ShapeDtypeStruct((HEADS, t, HEAD_PAD), BF16),
                   jax.ShapeDtypeStruct((t, KV_RANK), F32),
                   jax.ShapeDtypeStruct((t, QK_ROPE), F32),
                   jax.ShapeDtypeStruct((t, KFEAT), BF16),
                   jax.ShapeDtypeStruct((t, 3 * SC_DIM), F32)],
        compiler_params=_cparams("arbitrary"),
        name="in_proj",
    )(x, mod, wl["w_in"], wl["g_q"], wl["g_kv"], wl["wq_a"], wl["wq_b"], *tabs)


def _attn_kernel(n_own, n_cache, q_ref, kf_ref, *rest):
    if n_cache:
        kc_ref, wkt_ref, wv_ref, o_ref, kt_scr, v_scr = rest
    else:
        wkt_ref, wv_ref, o_ref, kt_scr, v_scr = rest

    @pl.when(pl.program_id(1) == 0)
    def _():
        def fill(feat, lo, n):
            for hd in range(HEADS):
                kt = lax.dot_general(wkt_ref[hd], feat, (((1,), (1,)), ((), ())),
                                     preferred_element_type=F32)
                kt_scr[hd, :, lo:lo + n] = kt.astype(BF16)
                v = jnp.dot(feat, wv_ref[:, hd * LANES:(hd + 1) * LANES], preferred_element_type=F32)
                v_scr[hd, lo:lo + n, :] = v.astype(BF16)

        fill(kf_ref[...], 0, n_own)
        if n_cache:
            fill(kc_ref[0], n_own, n_cache)

    for pair in range(HEADS // 2):
        acc = None
        for hd in (2 * pair, 2 * pair + 1):
            s = jnp.dot(q_ref[hd], kt_scr[hd], preferred_element_type=F32)
            p = jnp.exp(s - jnp.max(s, axis=-1, keepdims=True))
            l = jnp.sum(p, axis=-1, keepdims=True)
            o = jnp.dot(p.astype(BF16), v_scr[hd], preferred_element_type=F32) / l
            acc = o if acc is None else acc + o
        o_ref[pair] = acc.astype(BF16)


def _attention(q, kfeat, kcache, wl, batch, n_own, row0):
    n_cache = 0 if kcache is None else kcache.shape[1]
    m = n_own + n_cache
    tq = min(TQ, n_own)
    nq = n_own // tq
    q0 = row0 // tq
    k0 = row0 // n_own
    in_specs = [pl.BlockSpec((HEADS, tq, HEAD_PAD), lambda b, j: (0, q0 + b * nq + j, 0)),
                pl.BlockSpec((n_own, KFEAT), lambda b, j: (k0 + b, 0))]
    args = [q, kfeat]
    if n_cache:
        in_specs.append(pl.BlockSpec((1, n_cache, KFEAT), lambda b, j: (b, 0, 0)))
        args.append(kcache)
    in_specs += [pl.BlockSpec((HEADS, HEAD_PAD, KFEAT), lambda b, j: (0, 0, 0)),
                 pl.BlockSpec((KFEAT, HEADS * LANES), lambda b, j: (0, 0))]
    args += [wl["wk_t"], wl["wv"]]
    return pl.pallas_call(
        functools.partial(_attn_kernel, n_own, n_cache),
        grid=(batch, nq),
        in_specs=in_specs,
        out_specs=pl.BlockSpec((HEADS // 2, tq, LANES), lambda b, j: (0, b * nq + j, 0)),
        out_shape=jax.ShapeDtypeStruct((HEADS // 2, batch * n_own, LANES), BF16),
        scratch_shapes=[pltpu.VMEM((HEADS, HEAD_PAD, m), BF16), pltpu.VMEM((HEADS, m, LANES), BF16)],
        compiler_params=_cparams("arbitrary", "arbitrary"),
        name="attention_cache" if n_cache else "attention_ctx",
    )(*args)


def _mix_kernel(n_p_tiles, tiles_per_s_seq, alpha,
                x_ref, atp_ref, ats_ref, cprev_ref, ccur_ref, cnext_ref, mod_ref,
                wsc_ref, wcc_ref, bcc_ref, gcf_ref, bcf_ref, wout_ref, g1_ref, b1_ref,
                wr_ref, br_ref,
                x1_ref, h2_ref, idx_ref, rank_ref, gate_ref, cnt_ref,
                ext_scr):
    i = pl.program_id(0)
    is_s = i >= n_p_tiles
    jj = (i - n_p_tiles) % tiles_per_s_seq
    has_prev = jnp.logical_and(is_s, jj > 0)
    has_next = jnp.logical_and(is_s, jj < tiles_per_s_seq - 1)

    ext_scr[0:HALO, :] = jnp.where(has_prev, cprev_ref[:, SC_DIM:], 0.0)
    ext_scr[HALO:HALO + TM, :] = ccur_ref[:, SC_DIM:]
    ext_scr[HALO + TM:, :] = jnp.where(has_next, cnext_ref[:, SC_DIM:], 0.0)

    sconv = jnp.zeros((TM, SC_DIM), F32)
    for k in range(SC_WIDTH):
        o = HALO - SC_WIDTH // 2 + k
        sconv = sconv + ext_scr[o:o + TM, 0:SC_DIM] * wsc_ref[k:k + 1, :]
    sc_out = ccur_ref[:, 0:SC_DIM] * sconv

    cconv = jnp.zeros((TM, CF_DIM), F32)
    for k in range(CF_WIDTH):
        o = HALO - CF_WIDTH // 2 + k
        cconv = cconv + ext_scr[o:o + TM, SC_DIM:] * wcc_ref[k:k + 1, :]
    cf = _ln_plain(cconv + bcc_ref[...]) * gcf_ref[...] + bcf_ref[...]
    cf = cf * jax.nn.sigmoid(cf)

    at = [jnp.where(is_s, ats_ref[p], atp_ref[p]) for p in range(HEADS // 2)]
    mix_in = jnp.concatenate(at + [sc_out.astype(BF16), cf.astype(BF16)], axis=1)
    mix = jnp.dot(mix_in, wout_ref[...], preferred_element_type=F32)

    mod = mod_ref[0]
    g1 = mod[:, 2 * D_MODEL:3 * D_MODEL]
    sh2 = mod[:, 3 * D_MODEL:4 * D_MODEL]
    sc2 = mod[:, 4 * D_MODEL:5 * D_MODEL]
    x1 = _ln_plain(alpha * x_ref[...] + g1 * mix) * g1_ref[...] + b1_ref[...]
    x1_ref[...] = x1
    h2 = _ln_plain(x1) * (1.0 + sc2) + sh2
    h2_ref[...] = h2.astype(BF16)

    logits = jnp.dot(h2, wr_ref[...], preferred_element_type=F32,
                     precision=lax.Precision.HIGHEST) + br_ref[...]
    lane = lax.broadcasted_iota(jnp.int32, (TM, LANES), 1)
    lane_f = lane.astype(F32)
    neg = jnp.float32(-jnp.inf)
    cur = jnp.where(lane < N_EXPERTS, logits, neg)
    vals, hots, idxs = [], [], []
    for _ in range(TOP_K):
        mx = jnp.max(cur, axis=-1, keepdims=True)
        ik = jnp.min(jnp.where(cur == mx, lane_f, float(LANES)), axis=-1, keepdims=True)
        ik = jnp.minimum(ik, float(N_EXPERTS - 1))
        hot = lane_f == ik
        cur = jnp.where(hot, neg, cur)
        vals.append(mx)
        hots.append(hot)
        idxs.append(ik.astype(jnp.int32))
    exps = [jnp.exp(v - vals[0]) for v in vals]
    denom = exps[0] + exps[1] + exps[2] + exps[3]
    sel = jnp.where(hots[0] | hots[1] | hots[2] | hots[3], 1.0, 0.0)

    r_i = lax.broadcasted_iota(jnp.int32, (TM, TM), 0)
    c_i = lax.broadcasted_iota(jnp.int32, (TM, TM), 1)
    tri = jnp.where(c_i < r_i, 1.0, 0.0).astype(BF16)
    rank_all = jnp.dot(tri, sel.astype(BF16), preferred_element_type=F32)
    cnt_ref[0] = jnp.broadcast_to(jnp.sum(sel, axis=0, keepdims=True), (SUBLANES, LANES))

    idx_o = jnp.zeros((TM, LANES), jnp.int32)
    rank_o = jnp.zeros((TM, LANES), jnp.int32)
    gate_o = jnp.zeros((TM, LANES), F32)
    for k in range(TOP_K):
        rk = jnp.sum(jnp.where(hots[k], rank_all, 0.0), axis=-1, keepdims=True).astype(jnp.int32)
        idx_o = jnp.where(lane == k, idxs[k], idx_o)
        rank_o = jnp.where(lane == k, rk, rank_o)
        gate_o = jnp.where(lane == k, exps[k] / denom, gate_o)
    idx_ref[...] = idx_o
    rank_ref[...] = rank_o
    gate_ref[...] = gate_o


def _mix(x, attn_p, attn_s, conv_in, mod, wl, n_p_tiles, tiles_per_s_seq, ctx_row, alpha):
    t = x.shape[0]
    nt = t // TM
    hpt = TM // HALO
    n_halo = t // HALO

    def mod_idx(i):
        return (jnp.where(i < n_p_tiles, ctx_row, (i - n_p_tiles) // tiles_per_s_seq), 0, 0)

    full = lambda shp: pl.BlockSpec(shp, lambda i: (0,) * len(shp))
    row = lambda w: pl.BlockSpec((TM, w), lambda i: (i, 0))
    return pl.pallas_call(
        functools.partial(_mix_kernel, n_p_tiles, tiles_per_s_seq, alpha),
        grid=(nt,),
        in_specs=[row(D_MODEL),
                  pl.BlockSpec((HEADS // 2, TM, LANES), lambda i: (0, jnp.minimum(i, n_p_tiles - 1), 0)),
                  pl.BlockSpec((HEADS // 2, TM, LANES), lambda i: (0, jnp.maximum(i - n_p_tiles, 0), 0)),
                  pl.BlockSpec((HALO, 3 * SC_DIM), lambda i: (jnp.maximum(i * hpt - 1, 0), 0)),
                  row(3 * SC_DIM),
                  pl.BlockSpec((HALO, 3 * SC_DIM), lambda i: (jnp.minimum((i + 1) * hpt, n_halo - 1), 0)),
                  pl.BlockSpec((1, 1, 6 * D_MODEL), mod_idx),
                  full((SC_WIDTH, SC_DIM)), full((CF_WIDTH, CF_DIM)), full((1, CF_DIM)),
                  full((1, CF_DIM)), full((1, CF_DIM)),
                  full((D_MODEL, D_MODEL)), full((1, D_MODEL)), full((1, D_MODEL)),
                  full((D_MODEL, LANES)), full((1, LANES))],
        out_specs=[row(D_MODEL), row(D_MODEL),
                   row(LANES), row(LANES), row(LANES),
                   pl.BlockSpec((1, SUBLANES, LANES), lambda i: (i, 0, 0))],
        out_shape=[jax.ShapeDtypeStruct((t, D_MODEL), F32),
                   jax.ShapeDtypeStruct((t, D_MODEL), BF16),
                   jax.ShapeDtypeStruct((t, LANES), jnp.int32),
                   jax.ShapeDtypeStruct((t, LANES), jnp.int32),
                   jax.ShapeDtypeStruct((t, LANES), F32),
                   jax.ShapeDtypeStruct((nt, SUBLANES, LANES), F32)],
        scratch_shapes=[pltpu.VMEM((TM + 2 * HALO, 2 * SC_DIM), F32)],
        compiler_params=_cparams("arbitrary"),
        name="mix",
    )(x, attn_p, attn_s, conv_in, conv_in, conv_in, mod,
      wl["w_sconv"], wl["w_cconv"], wl["b_cconv"], wl["g_cf"], wl["b_cf"],
      wl["w_out"], wl["g_ln1"], wl["b_ln1"], wl["w_router"], wl["b_router"])


def _segment_copies(n, make_copy, start=True):
    pos = jnp.int32(0)
    for size in SEG_SIZES:
        take = n & size

        @pl.when(take != 0)
        def _(pos=pos, size=size):
            cp = make_copy(pos, size)
            cp.start() if start else cp.wait()

        pos = pos + take


def _dispatch_kernel(a_ref, off_ref, seg_ref, tot_ref, tail0_ref, tailn_ref, nu_ref,
                     h_ref, qt_ref, xs_hbm, buf, zbuf, sem, zsem):
    i = pl.program_id(0)
    last = pl.num_programs(0) - 1
    slot = i % 2

    r = lax.broadcasted_iota(jnp.int32, (R_TILE, TM), 0)
    hit = r == qt_ref[0:1, :]
    for k in range(1, TOP_K):
        hit = jnp.logical_or(hit, r == qt_ref[k:k + 1, :])
    perm = jnp.where(hit, 1.0, 0.0).astype(BF16)
    buf[slot] = jnp.dot(perm, h_ref[...], preferred_element_type=F32)

    def wait_tile(j, s):
        n = pl.multiple_of(tot_ref[j], SEG)

        @pl.when(n > 0)
        def _():
            pltpu.make_async_copy(buf.at[s, pl.ds(0, n)], xs_hbm.at[pl.ds(0, n)], sem.at[s]).wait()

    @pl.when(i > 0)
    def _():
        wait_tile(i - 1, 1 - slot)

    def per_expert(e, carry):
        n = a_ref[i * N_EXPERTS + e]
        src = off_ref[i * N_EXPERTS + e]
        dst = seg_ref[i * N_EXPERTS + e]
        _segment_copies(n, lambda pos, size: pltpu.make_async_copy(
            buf.at[slot, pl.ds(pl.multiple_of(src + pos, SEG), size)],
            xs_hbm.at[pl.ds(pl.multiple_of(dst + pos, SEG), size)], sem.at[slot]))
        return carry

    lax.fori_loop(0, N_EXPERTS, per_expert, 0)

    @pl.when(i == last)
    def _():
        zbuf[...] = jnp.zeros_like(zbuf)

        def tail(start):
            def per_tail(e, carry):
                dst = tail0_ref[e]
                _segment_copies(tailn_ref[e], lambda pos, size: pltpu.make_async_copy(
                    zbuf.at[pl.ds(0, size)], xs_hbm.at[pl.ds(pl.multiple_of(dst + pos, SEG), size)], zsem),
                    start=start)
                return carry
            lax.fori_loop(0, N_EXPERTS, per_tail, 0)

        def spare(b):
            return pltpu.make_async_copy(zbuf, xs_hbm.at[pl.ds(pl.multiple_of(b * MB, MB), MB)], zsem)

        def start_spare(b, carry):
            spare(b).start()
            return carry

        def wait_spare(b, carry):
            spare(b).wait()
            return carry

        n_blocks = xs_hbm.shape[0] // MB
        tail(True)
        lax.fori_loop(nu_ref[0], n_blocks, start_spare, 0)
        tail(False)
        lax.fori_loop(nu_ref[0], n_blocks, wait_spare, 0)
        wait_tile(i, slot)


def _dispatch(tables, n_used, h2, q_t, p_rows):
    t = h2.shape[0]
    assert SEG_SIZES[0] == MB
    grid_spec = pltpu.PrefetchScalarGridSpec(
        num_scalar_prefetch=7,
        grid=(t // TM,),
        in_specs=[pl.BlockSpec((TM, D_MODEL), lambda i, *_: (i, 0)),
                  pl.BlockSpec((TOP_K, TM), lambda i, *_: (0, i))],
        out_specs=pl.BlockSpec(memory_space=pl.ANY),
        scratch_shapes=[pltpu.VMEM((2, R_TILE, D_MODEL), F32), pltpu.VMEM((SEG_SIZES[0], D_MODEL), F32),
                        pltpu.SemaphoreType.DMA((2,)), pltpu.SemaphoreType.DMA(())],
    )
    return pl.pallas_call(
        _dispatch_kernel,
        grid_spec=grid_spec,
        out_shape=jax.ShapeDtypeStruct((p_rows, D_MODEL), F32),
        compiler_params=_cparams("arbitrary"),
        name="dispatch",
    )(tables["a"], tables["off"], tables["seg"], tables["tot"], tables["tail0"], tables["tailn"], n_used, h2, q_t)


def _expert_kernel(be_ref, nu_ref, xs_ref, w1_ref, b1_ref, w2_ref, b2_ref, ys_ref, w1_scr, w2_scr):
    i = pl.program_id(0)
    used = i < nu_ref[0]

    @pl.when(jnp.logical_and(used, jnp.logical_or(i == 0, be_ref[i] != be_ref[jnp.maximum(i - 1, 0)])))
    def _():
        w1_scr[...] = w1_ref[0, 0].astype(BF16)
        w2_scr[...] = w2_ref[0, 0].astype(BF16)

    @pl.when(used)
    def _():
        z = jnp.dot(xs_ref[...].astype(BF16), w1_scr[...], preferred_element_type=F32) + b1_ref[0, 0]
        gate = jnp.minimum(z[:, :D_EXPERT], SWIGLU_LIMIT)
        up = jnp.clip(z[:, D_EXPERT:], -SWIGLU_LIMIT, SWIGLU_LIMIT)
        act = (up + 1.0) * gate * jax.nn.sigmoid(SWIGLU_ALPHA * gate)
        ys_ref[...] = jnp.dot(act.astype(BF16), w2_scr[...], preferred_element_type=F32) + b2_ref[0, 0]

    @pl.when(jnp.logical_not(used))
    def _():
        ys_ref[...] = jnp.zeros_like(ys_ref)


def _experts(blk_expert, n_used, xs, l, w_exp_in, b_exp_in, w_exp_out, b_exp_out):
    p_rows = xs.shape[0]
    nb = p_rows // MB
    depth = w_exp_in.shape[0]
    blk = lambda i, nu: jnp.maximum(jnp.minimum(i, nu[0] - 1), 0)
    wmap = lambda i, be, nu: (l, be[blk(i, nu)], 0, 0)
    xmap = lambda i, be, nu: (blk(i, nu), 0)
    grid_spec = pltpu.PrefetchScalarGridSpec(
        num_scalar_prefetch=2,
        grid=(nb,),
        in_specs=[pl.BlockSpec((MB, D_MODEL), xmap),
                  pl.BlockSpec((1, 1, D_MODEL, 2 * D_EXPERT), wmap),
                  pl.BlockSpec((1, 1, 1, 2 * D_EXPERT), wmap),
                  pl.BlockSpec((1, 1, D_EXPERT, D_MODEL), wmap),
                  pl.BlockSpec((1, 1, 1, D_MODEL), wmap)],
        out_specs=pl.BlockSpec((MB, D_MODEL), lambda i, be, nu: (i, 0)),
        scratch_shapes=[pltpu.VMEM((D_MODEL, 2 * D_EXPERT), BF16), pltpu.VMEM((D_EXPERT, D_MODEL), BF16)],
    )
    return pl.pallas_call(
        _expert_kernel,
        grid_spec=grid_spec,
        out_shape=jax.ShapeDtypeStruct((p_rows, D_MODEL), F32),
        compiler_params=_cparams("arbitrary"),
        name="experts",
    )(blk_expert, n_used, xs, w_exp_in, b_exp_in.reshape(depth, N_EXPERTS, 1, -1),
      w_exp_out, b_exp_out.reshape(depth, N_EXPERTS, 1, -1))


def _combine_kernel(alpha, n_p_tiles, split, a_ref, off_ref, seg_ref, tot_ref,
                    q_ref, gate_ref, x1_ref, mod_ref, g2_ref, b2_ref, ys_hbm, *rest):
    if split:
        o_p_ref, o_s_ref, buf, sem = rest
    else:
        o_ref, buf, sem = rest
    i = pl.program_id(0)
    slot = i % 2

    def fetch(j, s):
        def per_expert(e, carry):
            n = a_ref[j * N_EXPERTS + e]
            src = seg_ref[j * N_EXPERTS + e]
            dst = off_ref[j * N_EXPERTS + e]
            _segment_copies(n, lambda pos, size: pltpu.make_async_copy(
                ys_hbm.at[pl.ds(pl.multiple_of(src + pos, SEG), size)],
                buf.at[s, pl.ds(pl.multiple_of(dst + pos, SEG), size)], sem.at[s]))
            return carry

        lax.fori_loop(0, N_EXPERTS, per_expert, 0)

    @pl.when(i == 0)
    def _():
        buf[...] = jnp.zeros_like(buf)
        fetch(0, 0)

    @pl.when(i + 1 < pl.num_programs(0))
    def _():
        fetch(i + 1, 1 - slot)

    n = pl.multiple_of(tot_ref[i], SEG)

    @pl.when(n > 0)
    def _():
        pltpu.make_async_copy(ys_hbm.at[pl.ds(0, n)], buf.at[slot, pl.ds(0, n)], sem.at[slot]).wait()

    lane = lax.broadcasted_iota(jnp.int32, (TM, R_TILE), 1)
    q = q_ref[...]
    gates = gate_ref[...]
    g = jnp.zeros((TM, R_TILE), F32)
    for k in range(TOP_K):
        g = jnp.where(lane == q[:, k:k + 1], gates[:, k:k + 1], g)
    ffn = jnp.dot(g.astype(BF16), buf[slot].astype(BF16), preferred_element_type=F32)
    g2 = mod_ref[0][:, 5 * D_MODEL:6 * D_MODEL]
    res = _ln_plain(alpha * x1_ref[...] + g2 * ffn) * g2_ref[...] + b2_ref[...]
    if split:
        @pl.when(i < n_p_tiles)
        def _():
            o_p_ref[...] = res

        @pl.when(i >= n_p_tiles)
        def _():
            o_s_ref[...] = res
    else:
        o_ref[...] = res


def _combine(tables, q, gates, x1, mod, ys, wl, n_p_tiles, tiles_per_s_seq, ctx_row, alpha, split):
    t = x1.shape[0]
    nt = t // TM

    def mod_idx(i, *_):
        return (jnp.where(i < n_p_tiles, ctx_row, (i - n_p_tiles) // tiles_per_s_seq), 0, 0)

    if split:
        out_specs = [pl.BlockSpec((TM, D_MODEL), lambda i, *_: (jnp.minimum(i, n_p_tiles - 1), 0)),
                     pl.BlockSpec((TM, D_MODEL), lambda i, *_: (jnp.maximum(i - n_p_tiles, 0), 0))]
        out_shape = [jax.ShapeDtypeStruct((n_p_tiles * TM, D_MODEL), F32),
                     jax.ShapeDtypeStruct((t - n_p_tiles * TM, D_MODEL), F32)]
    else:
        out_specs = pl.BlockSpec((TM, D_MODEL), lambda i, *_: (i, 0))
        out_shape = jax.ShapeDtypeStruct((t, D_MODEL), F32)
    grid_spec = pltpu.PrefetchScalarGridSpec(
        num_scalar_prefetch=4,
        grid=(nt,),
        in_specs=[pl.BlockSpec((TM, TOP_K), lambda i, *_: (i, 0)),
                  pl.BlockSpec((TM, LANES), lambda i, *_: (i, 0)),
                  pl.BlockSpec((TM, D_MODEL), lambda i, *_: (i, 0)),
                  pl.BlockSpec((1, 1, 6 * D_MODEL), mod_idx),
                  pl.BlockSpec((1, D_MODEL), lambda i, *_: (0, 0)),
                  pl.BlockSpec((1, D_MODEL), lambda i, *_: (0, 0)),
                  pl.BlockSpec(memory_space=pl.ANY)],
        out_specs=out_specs,
        scratch_shapes=[pltpu.VMEM((2, R_TILE, D_MODEL), F32), pltpu.SemaphoreType.DMA((2,))],
    )
    return pl.pallas_call(
        functools.partial(_combine_kernel, alpha, n_p_tiles, split),
        grid_spec=grid_spec,
        out_shape=out_shape,
        compiler_params=_cparams("arbitrary"),
        name="combine",
    )(tables["a"], tables["off"], tables["seg"], tables["tot"], q, gates, x1, mod, wl["g_ln2"], wl["b_ln2"], ys)


def _routing_tables(counts, idx, lrank, n_blocks):
    cnt = counts[:, 0, :N_EXPERTS].astype(jnp.int32)
    a = (cnt + SEG - 1) // SEG * SEG
    off = jnp.cumsum(a, axis=1) - a
    acc = jnp.cumsum(a, axis=0) - a
    tot_e = jnp.sum(a, axis=0)
    padded = (tot_e + MB - 1) // MB * MB
    pend = jnp.cumsum(padded)
    pstart = pend - padded
    seg = pstart[None, :] + acc
    blk_row0 = jnp.arange(n_blocks, dtype=jnp.int32) * MB
    blk_expert = jnp.minimum(jnp.sum((pend[None, :] <= blk_row0[:, None]).astype(jnp.int32), axis=1), N_EXPERTS - 1)
    n_used = (pend[-1:] // MB).astype(jnp.int32)
    off_tok = jnp.repeat(off, TM, axis=0)
    hot = idx[:, :TOP_K, None] == jnp.arange(N_EXPERTS, dtype=jnp.int32)
    q = jnp.sum(jnp.where(hot, off_tok[:, None, :], 0), axis=-1) + lrank[:, :TOP_K]
    tables = {"a": a.reshape(-1), "off": off.reshape(-1), "seg": seg.reshape(-1), "tot": jnp.sum(a, axis=1),
              "tail0": pstart + tot_e, "tailn": padded - tot_e}
    return tables, q, blk_expert, n_used


def _pad_cols(w, width):
    return jnp.pad(w, ((0, 0), (0, width - w.shape[1])))


def _rot_cols(w):
    half = w.shape[1] // 2
    return jnp.concatenate([-w[:, half:], w[:, :half]], axis=1)


def _layer_weights(l, w_in, g_q_norm, w_uq, g_kv_norm, w_uk, w_uv, w_sconv, w_cconv, b_cconv,
                   g_cf_norm, b_cf_norm, w_out, g_ln1, b_ln1, g_ln2, b_ln2, w_router, b_router):
    wi = w_in[l]
    o_ckv, o_kr, o_bg = Q_RANK, Q_RANK + KV_RANK, Q_RANK + KV_RANK + QK_ROPE
    w_kr = wi[:, o_kr:o_bg]
    w_in_p = jnp.concatenate([
        _pad_cols(wi[:, :Q_RANK], COL_CKV - COL_CQ),
        wi[:, o_ckv:o_kr],
        _pad_cols(jnp.concatenate([w_kr, _rot_cols(w_kr)], axis=1), COL_BG - COL_KR),
        wi[:, o_bg:]], axis=1).astype(BF16)

    uq = jnp.pad(w_uq[l], ((0, Q_RANK_PAD - Q_RANK), (0, 0))).reshape(Q_RANK_PAD, HEADS, QK_DIM)
    zer = jnp.zeros((Q_RANK_PAD, HEADS, HEAD_PAD - QK_DIM), F32)
    wq_a = jnp.concatenate([uq, zer], axis=2).reshape(Q_RANK_PAD, HEADS * HEAD_PAD).astype(BF16)
    rot = jnp.concatenate([-uq[:, :, QK_NOPE + QK_ROPE // 2:], uq[:, :, QK_NOPE:QK_NOPE + QK_ROPE // 2]], axis=2)
    wq_b = jnp.concatenate([jnp.zeros((Q_RANK_PAD, HEADS, QK_NOPE), F32), rot, zer],
                           axis=2).reshape(Q_RANK_PAD, HEADS * HEAD_PAD).astype(BF16)

    uk_t = w_uk[l].reshape(KV_RANK, HEADS, QK_NOPE).transpose(1, 2, 0)
    top = jnp.pad(uk_t, ((0, 0), (0, 0), (0, KFEAT - KV_RANK)))
    eye = jnp.pad(jnp.eye(QK_ROPE, dtype=F32), ((0, 0), (KV_RANK, KFEAT - KV_RANK - QK_ROPE)))
    mid = jnp.broadcast_to(eye[None], (HEADS, QK_ROPE, KFEAT))
    wk_t = jnp.concatenate([top, mid, jnp.zeros((HEADS, HEAD_PAD - QK_DIM, KFEAT), F32)], axis=1).astype(BF16)

    uv = w_uv[l].reshape(KV_RANK, HEADS, V_DIM)
    zv = jnp.zeros((KV_RANK, HEADS, V_DIM), F32)
    odd = (jnp.arange(HEADS) % 2 == 1)[None, :, None]
    wv = jnp.concatenate([jnp.where(odd, zv, uv), jnp.where(odd, uv, zv)], axis=2).reshape(KV_RANK, HEADS * LANES)
    wv = jnp.pad(wv, ((0, KFEAT - KV_RANK), (0, 0))).astype(BF16)

    r1 = lambda v: v.reshape(1, -1)
    return {
        "w_in": w_in_p,
        "g_q": jnp.pad(g_q_norm[l], (0, Q_RANK_PAD - Q_RANK)).reshape(1, -1),
        "g_kv": r1(g_kv_norm[l]),
        "wq_a": wq_a, "wq_b": wq_b, "wk_t": wk_t, "wv": wv,
        "w_sconv": w_sconv[l], "w_cconv": w_cconv[l], "b_cconv": r1(b_cconv[l]),
        "g_cf": r1(g_cf_norm[l]), "b_cf": r1(b_cf_norm[l]),
        "w_out": w_out[l].astype(BF16),
        "g_ln1": r1(g_ln1[l]), "b_ln1": r1(b_ln1[l]), "g_ln2": r1(g_ln2[l]), "b_ln2": r1(b_ln2[l]),
        "w_router": _pad_cols(w_router[l], LANES),
        "b_router": jnp.pad(b_router[l], (0, LANES - N_EXPERTS)).reshape(1, -1),
    }


def _rope_tables(n_lat):
    rows = n_lat // GRID_W
    row = jnp.repeat(jnp.arange(rows, dtype=F32), GRID_W)
    col = jnp.tile(jnp.arange(GRID_W, dtype=F32), rows)
    axis_dim = QK_ROPE // 2
    inv = ROPE_BASE ** (-jnp.arange(0, axis_dim, 2, dtype=F32) / axis_dim)
    ang = jnp.concatenate([row[:, None] * inv, col[:, None] * inv], axis=-1)
    cos, sin = jnp.cos(ang), jnp.sin(ang)
    one = jnp.ones((n_lat, QK_NOPE), F32)
    zq = jnp.zeros((n_lat, HEAD_PAD - QK_DIM), F32)
    zk = jnp.zeros((n_lat, LANES - QK_ROPE), F32)
    return (jnp.concatenate([one, cos, cos, zq], axis=1),
            jnp.concatenate([0.0 * one, sin, sin, zq], axis=1),
            jnp.concatenate([cos, cos, zk], axis=1),
            jnp.concatenate([sin, sin, zk], axis=1))


def kernel(x_prompt, x_sample, cache_ckv, cache_krope, c, c_ctx, w_mod, b_mod, w_in, g_q_norm, w_uq, g_kv_norm, w_uk, w_uv, w_sconv, w_cconv, b_cconv, g_cf_norm, b_cf_norm, w_out, g_ln1, b_ln1, g_ln2, b_ln2, w_router, b_router, w_exp_in, b_exp_in, w_exp_out, b_exp_out):
    bp, n_p, d = x_prompt.shape
    bs, n_s, _ = x_sample.shape
    depth = w_mod.shape[0]
    past = cache_ckv.shape[2]
    assert d == D_MODEL and n_p == TM and n_s % TM == 0 and n_s % GRID_W == 0
    t_p, t_s = bp * n_p, bs * n_s
    assert t_p % n_s == 0 and past % LANES == 0
    t = t_p + t_s
    n_p_tiles = t_p // TM
    tps = n_s // TM
    alpha = float((2 * depth) ** 0.25)

    ctx_row = bs
    rows = -(-(bs + 1) // SUBLANES) * SUBLANES
    c_rows = jnp.concatenate([c, c_ctx[None], jnp.zeros((rows - bs - 1, d), F32)], axis=0)
    mod_all = _modulation(c_rows, w_mod, b_mod)

    tabs = _rope_tables(n_s)
    x = jnp.concatenate([x_prompt.reshape(t_p, d), x_sample.reshape(t_s, d)], axis=0)
    nt = t // TM
    n_blocks = -(-(t * TOP_K + nt * N_EXPERTS * (SEG - 1)) // MB) + N_EXPERTS
    p_rows = n_blocks * MB

    new_ckv, new_krope = [], []
    for l in range(depth):
        wl = _layer_weights(l, w_in, g_q_norm, w_uq, g_kv_norm, w_uk, w_uv, w_sconv, w_cconv, b_cconv,
                            g_cf_norm, b_cf_norm, w_out, g_ln1, b_ln1, g_ln2, b_ln2, w_router, b_router)
        mod = mod_all[l].reshape(rows, 1, 6 * d)
        q, ckv, krope, kfeat, conv_in = _in_proj(x, mod, wl, tabs, n_p_tiles, tps, ctx_row)
        new_ckv.append(ckv[:t_p].reshape(bp, n_p, KV_RANK))
        new_krope.append(krope[:t_p].reshape(bp, n_p, QK_ROPE))

        kcache = jnp.concatenate(
            [cache_ckv[:, l], cache_krope[:, l], jnp.zeros((bs, past, KFEAT - KV_RANK - QK_ROPE), F32)],
            axis=-1).astype(BF16)
        attn_p = _attention(q, kfeat, None, wl, bp, n_p, 0)
        attn_s = _attention(q, kfeat, kcache, wl, bs, n_s, t_p)
        x1, h2, idx, lrank, gates, counts = _mix(x, attn_p, attn_s, conv_in, mod, wl, n_p_tiles, tps,
                                                 ctx_row, alpha)
        tables, q, blk_expert, n_used = _routing_tables(counts, idx, lrank, n_blocks)
        xs = _dispatch(tables, n_used, h2, q.T, p_rows)
        ys = _experts(blk_expert, n_used, xs, l, w_exp_in, b_exp_in, w_exp_out, b_exp_out)
        x = _combine(tables, q, gates, x1, mod, ys, wl, n_p_tiles, tps, ctx_row, alpha, split=l == depth - 1)

    y_p, y_s = x
    return (y_p.reshape(bp, n_p, d), y_s.reshape(bs, n_s, d),
            jnp.stack(new_ckv, axis=1), jnp.stack(new_krope, axis=1))
```

```python
import functools
import math

import jax
import jax.numpy as jnp
from jax import lax
from jax.experimental import pallas as pl
from jax.experimental.pallas import tpu as pltpu

F32 = jnp.float32
BF16 = jnp.bfloat16

D_MODEL = 1024
GRID_W = 64
HEADS = 8
QK_NOPE = 64
QK_ROPE = 32
QK_DIM = QK_NOPE + QK_ROPE
V_DIM = 64
Q_RANK = 192
KV_RANK = 128
SC_DIM = 256
SC_WIDTH = 3
CF_DIM = 256
CF_WIDTH = 31
N_EXPERTS = 32
TOP_K = 4
D_EXPERT = 1024
SWIGLU_LIMIT = 7.0
SWIGLU_ALPHA = 1.702
ROPE_BASE = 10000.0
EPS = 1e-6

LANES = 128
SUBLANES = 8

TM = 256
TQ = 512
MB = 512
HALO = 16
SEG = SUBLANES
SEG_SIZES = tuple(MB >> s for s in range((MB // SEG).bit_length()))
R_TILE = TM * TOP_K + N_EXPERTS * SEG
HEAD_PAD = LANES
Q_RANK_PAD = 256
KFEAT = 256
COL_CQ, COL_CKV, COL_KR, COL_BG, COL_CG, COL_XS, COL_GA, COL_GB, IN_COLS_PAD = (
    0, 256, 384, 512, 768, 1024, 1280, 1536, 1792)
VMEM_LIMIT = 60 * 1024 * 1024


def _cparams(*sem):
    return pltpu.CompilerParams(dimension_semantics=sem, vmem_limit_bytes=VMEM_LIMIT)


def _ln_plain(x):
    mu = jnp.mean(x, axis=-1, keepdims=True)
    xc = x - mu
    var = jnp.mean(xc * xc, axis=-1, keepdims=True)
    return xc * lax.rsqrt(var + EPS)


def _mod_kernel(c_ref, w_ref, b_ref, o_ref):
    c = c_ref[...]
    s = c * jax.nn.sigmoid(c)
    o_ref[0] = jnp.dot(s, w_ref[0], preferred_element_type=F32,
                       precision=lax.Precision.HIGHEST) + b_ref[0]


def _modulation(c_rows, w_mod, b_mod):
    depth = w_mod.shape[0]
    rows = c_rows.shape[0]
    nblk = w_mod.shape[2] // D_MODEL
    return pl.pallas_call(
        _mod_kernel,
        grid=(depth, nblk),
        in_specs=[pl.BlockSpec((rows, D_MODEL), lambda l, j: (0, 0)),
                  pl.BlockSpec((1, D_MODEL, D_MODEL), lambda l, j: (l, 0, j)),
                  pl.BlockSpec((1, 1, D_MODEL), lambda l, j: (l, 0, j))],
        out_specs=pl.BlockSpec((1, rows, D_MODEL), lambda l, j: (l, 0, j)),
        out_shape=jax.ShapeDtypeStruct((depth, rows, w_mod.shape[2]), F32),
        compiler_params=_cparams("arbitrary", "arbitrary"),
        name="modulation",
    )(c_rows, w_mod, b_mod.reshape(depth, 1, -1))


def _in_proj_kernel(n_p_tiles, x_ref, mod_ref, win_ref, gq_ref, gkv_ref, wqa_ref, wqb_ref,
                    tcq_ref, tsq_ref, tck_ref, tsk_ref,
                    q_ref, ckv_ref, kr_ref, kf_ref, cv_ref):
    is_s = pl.program_id(0) >= n_p_tiles
    mod = mod_ref[0]
    h = _ln_plain(x_ref[...]) * (1.0 + mod[:, D_MODEL:2 * D_MODEL]) + mod[:, 0:D_MODEL]
    z = jnp.dot(h.astype(BF16), win_ref[...], preferred_element_type=F32)

    cq = z[:, COL_CQ:COL_CQ + Q_RANK_PAD]
    cqn = cq * lax.rsqrt(jnp.sum(cq * cq, axis=-1, keepdims=True) * (1.0 / Q_RANK) + EPS) * gq_ref[...]
    cqn = cqn.astype(BF16)
    qa = jnp.dot(cqn, wqa_ref[...], preferred_element_type=F32)
    qb = jnp.dot(cqn, wqb_ref[...], preferred_element_type=F32)
    cos_q = jnp.where(is_s, tcq_ref[...], 1.0)
    sin_q = jnp.where(is_s, tsq_ref[...], 0.0)
    scale = 1.0 / math.sqrt(QK_DIM)
    for hd in range(HEADS):
        sl = slice(hd * HEAD_PAD, (hd + 1) * HEAD_PAD)
        q_ref[hd] = ((qa[:, sl] * cos_q + qb[:, sl] * sin_q) * scale).astype(BF16)

    cr = z[:, COL_CKV:COL_CKV + KV_RANK]
    ckv = cr * lax.rsqrt(jnp.mean(cr * cr, axis=-1, keepdims=True) + EPS) * gkv_ref[...]
    ckv_ref[...] = ckv
    zk = z[:, COL_KR:COL_KR + LANES]
    kr_ref[...] = zk[:, 0:QK_ROPE]
    lane = lax.broadcasted_iota(jnp.int32, zk.shape, 1)
    raw = jnp.where(lane < QK_ROPE, zk, 0.0)
    roped = zk * tck_ref[...] + pltpu.roll(zk, LANES - QK_ROPE, 1) * tsk_ref[...]
    kf_ref[:, 0:KV_RANK] = ckv.astype(BF16)
    kf_ref[:, KV_RANK:KFEAT] = jnp.where(is_s, roped, raw).astype(BF16)

    cv_ref[:, 0:SC_DIM] = z[:, COL_BG:COL_BG + SC_DIM]
    cv_ref[:, SC_DIM:2 * SC_DIM] = z[:, COL_CG:COL_CG + SC_DIM] * z[:, COL_XS:COL_XS + SC_DIM]
    cv_ref[:, 2 * SC_DIM:] = z[:, COL_GA:COL_GA + CF_DIM] * jax.nn.sigmoid(z[:, COL_GB:COL_GB + CF_DIM])


def _in_proj(x, mod, wl, tabs, n_p_tiles, tiles_per_s_seq, ctx_row):
    t = x.shape[0]
    nt = t // TM

    def mod_idx(i):
        return (jnp.where(i < n_p_tiles, ctx_row, (i - n_p_tiles) // tiles_per_s_seq), 0, 0)

    def pos_idx(i):
        return (jnp.where(i < n_p_tiles, 0, (i - n_p_tiles) % tiles_per_s_seq), 0)

    full = lambda shp: pl.BlockSpec(shp, lambda i: (0,) * len(shp))
    tab = pl.BlockSpec((TM, LANES), pos_idx)
    return pl.pallas_call(
        functools.partial(_in_proj_kernel, n_p_tiles),
        grid=(nt,),
        in_specs=[pl.BlockSpec((TM, D_MODEL), lambda i: (i, 0)),
                  pl.BlockSpec((1, 1, 6 * D_MODEL), mod_idx),
                  full((D_MODEL, IN_COLS_PAD)),
                  full((1, Q_RANK_PAD)), full((1, KV_RANK)),
                  full((Q_RANK_PAD, HEADS * HEAD_PAD)), full((Q_RANK_PAD, HEADS * HEAD_PAD)),
                  tab, tab, tab, tab],
        out_specs=[pl.BlockSpec((HEADS, TM, HEAD_PAD), lambda i: (0, i, 0)),
                   pl.BlockSpec((TM, KV_RANK), lambda i: (i, 0)),
                   pl.BlockSpec((TM, QK_ROPE), lambda i: (i, 0)),
                   pl.BlockSpec((TM, KFEAT), lambda i: (i, 0)),
                   pl.BlockSpec((TM, 3 * SC_DIM), lambda i: (i, 0))],
        out_shape=[jax.ShapeDtypeStruct((HEADS, t, HEAD_PAD), BF16),
                   jax.ShapeDtypeStruct((t, KV_RANK), F32),
                   jax.ShapeDtypeStruct((t, QK_ROPE), F32),
                   jax.ShapeDtypeStruct((t, KFEAT), BF16),
                   jax.ShapeDtypeStruct((t, 3 * SC_DIM), F32)],
        compiler_params=_cparams("arbitrary"),
        name="in_proj",
    )(x, mod, wl["w_in"], wl["g_q"], wl["g_kv"], wl["wq_a"], wl["wq_b"], *tabs)


def _attn_kernel(n_own, n_cache, q_ref, kf_ref, *rest):
    if n_cache:
        kc_ref, wkt_ref, wv_ref, o_ref, kt_scr, v_scr = rest
    else:
        wkt_ref, wv_ref, o_ref, kt_scr, v_scr = rest

    @pl.when(pl.program_id(1) == 0)
    def _():
        def fill(feat, lo, n):
            for hd in range(HEADS):
                kt = lax.dot_general(wkt_ref[hd], feat, (((1,), (1,)), ((), ())),
                                     preferred_element_type=F32)
                kt_scr[hd, :, lo:lo + n] = kt.astype(BF16)
                v = jnp.dot(feat, wv_ref[:, hd * LANES:(hd + 1) * LANES], preferred_element_type=F32)
                v_scr[hd, lo:lo + n, :] = v.astype(BF16)

        fill(kf_ref[...], 0, n_own)
        if n_cache:
            fill(kc_ref[0], n_own, n_cache)

    for pair in range(HEADS // 2):
        acc = None
        for hd in (2 * pair, 2 * pair + 1):
            s = jnp.dot(q_ref[hd], kt_scr[hd], preferred_element_type=F32)
            p = jnp.exp(s - jnp.max(s, axis=-1, keepdims=True))
            l = jnp.sum(p, axis=-1, keepdims=True)
            o = jnp.dot(p.astype(BF16), v_scr[hd], preferred_element_type=F32) / l
            acc = o if acc is None else acc + o
        o_ref[pair] = acc.astype(BF16)


def _attention(q, kfeat, kcache, wl, batch, n_own, row0):
    n_cache = 0 if kcache is None else kcache.shape[1]
    m = n_own + n_cache
    tq = min(TQ, n_own)
    nq = n_own // tq
    q0 = row0 // tq
    k0 = row0 // n_own
    in_specs = [pl.BlockSpec((HEADS, tq, HEAD_PAD), lambda b, j: (0, q0 + b * nq + j, 0)),
                pl.BlockSpec((n_own, KFEAT), lambda b, j: (k0 + b, 0))]
    args = [q, kfeat]
    if n_cache:
        in_specs.append(pl.BlockSpec((1, n_cache, KFEAT), lambda b, j: (b, 0, 0)))
        args.append(kcache)
    in_specs += [pl.BlockSpec((HEADS, HEAD_PAD, KFEAT), lambda b, j: (0, 0, 0)),
                 pl.BlockSpec((KFEAT, HEADS * LANES), lambda b, j: (0, 0))]
    args += [wl["wk_t"], wl["wv"]]
    return pl.pallas_call(
        functools.partial(_attn_kernel, n_own, n_cache),
        grid=(batch, nq),
        in_specs=in_specs,
        out_specs=pl.BlockSpec((HEADS // 2, tq, LANES), lambda b, j: (0, b * nq + j, 0)),
        out_shape=jax.ShapeDtypeStruct((HEADS // 2, batch * n_own, LANES), BF16),
        scratch_shapes=[pltpu.VMEM((HEADS, HEAD_PAD, m), BF16), pltpu.VMEM((HEADS, m, LANES), BF16)],
        compiler_params=_cparams("arbitrary", "arbitrary"),
        name="attention_cache" if n_cache else "attention_ctx",
    )(*args)


def _mix_kernel(n_p_tiles, tiles_per_s_seq, alpha,
                x_ref, atp_ref, ats_ref, cprev_ref, ccur_ref, cnext_ref, mod_ref,
                wsc_ref, wcc_ref, bcc_ref, gcf_ref, bcf_ref, wout_ref, g1_ref, b1_ref,
                wr_ref, br_ref,
                x1_ref, h2_ref, idx_ref, rank_ref, gate_ref, cnt_ref,
                ext_scr, shf_scr):
    i = pl.program_id(0)
    is_s = i >= n_p_tiles
    jj = (i - n_p_tiles) % tiles_per_s_seq
    has_prev = jnp.logical_and(is_s, jj > 0)
    has_next = jnp.logical_and(is_s, jj < tiles_per_s_seq - 1)

    ext_scr[0:HALO, :] = jnp.where(has_prev, cprev_ref[:, SC_DIM:], 0.0)
    ext_scr[HALO:HALO + TM, :] = ccur_ref[:, SC_DIM:]
    ext_scr[HALO + TM:, :] = jnp.where(has_next, cnext_ref[:, SC_DIM:], 0.0)

    sconv = jnp.zeros((TM, SC_DIM), F32)
    for k in range(SC_WIDTH):
        o = HALO - SC_WIDTH // 2 + k
        sconv = sconv + ext_scr[o:o + TM, 0:SC_DIM] * wsc_ref[k:k + 1, :]
    sc_out = ccur_ref[:, 0:SC_DIM] * sconv

    n_sh = TM + 2 * HALO - SUBLANES
    for s in range(1, SUBLANES):
        shf_scr[s - 1] = ext_scr[s:s + n_sh, SC_DIM:]
    cconv = jnp.zeros((TM, CF_DIM), F32)
    for k in range(CF_WIDTH):
        o = HALO - CF_WIDTH // 2 + k
        s, base = o % SUBLANES, o - o % SUBLANES
        rows = ext_scr[base:base + TM, SC_DIM:] if s == 0 else shf_scr[s - 1, base:base + TM, :]
        cconv = cconv + rows * wcc_ref[k:k + 1, :]
    cf = _ln_plain(cconv + bcc_ref[...]) * gcf_ref[...] + bcf_ref[...]
    cf = cf * jax.nn.sigmoid(cf)

    at = [jnp.where(is_s, ats_ref[p], atp_ref[p]) for p in range(HEADS // 2)]
    mix_in = jnp.concatenate(at + [sc_out.astype(BF16), cf.astype(BF16)], axis=1)
    mix = jnp.dot(mix_in, wout_ref[...], preferred_element_type=F32)

    mod = mod_ref[0]
    g1 = mod[:, 2 * D_MODEL:3 * D_MODEL]
    sh2 = mod[:, 3 * D_MODEL:4 * D_MODEL]
    sc2 = mod[:, 4 * D_MODEL:5 * D_MODEL]
    x1 = _ln_plain(alpha * x_ref[...] + g1 * mix) * g1_ref[...] + b1_ref[...]
    x1_ref[...] = x1
    h2 = _ln_plain(x1) * (1.0 + sc2) + sh2
    h2_hi = h2.astype(BF16)
    h2_ref[...] = h2_hi
    h2_lo = (h2 - h2_hi.astype(F32)).astype(BF16)
    hi_both = jnp.dot(h2_hi, wr_ref[...], preferred_element_type=F32)
    lo_hi = jnp.dot(h2_lo, wr_ref[:, 0:LANES], preferred_element_type=F32)
    logits = hi_both[:, 0:LANES] + hi_both[:, LANES:] + lo_hi + br_ref[...]
    lane = lax.broadcasted_iota(jnp.int32, (TM, LANES), 1)
    lane_f = lane.astype(F32)
    neg = jnp.float32(-jnp.inf)
    cur = jnp.where(lane < N_EXPERTS, logits, neg)
    vals, hots, idxs = [], [], []
    for _ in range(TOP_K):
        mx = jnp.max(cur, axis=-1, keepdims=True)
        ik = jnp.min(jnp.where(cur == mx, lane_f, float(LANES)), axis=-1, keepdims=True)
        ik = jnp.minimum(ik, float(N_EXPERTS - 1))
        hot = lane_f == ik
        cur = jnp.where(hot, neg, cur)
        vals.append(mx)
        hots.append(hot)
        idxs.append(ik.astype(jnp.int32))
    exps = [jnp.exp(v - vals[0]) for v in vals]
    denom = exps[0] + exps[1] + exps[2] + exps[3]
    sel = jnp.where(hots[0] | hots[1] | hots[2] | hots[3], 1.0, 0.0)

    r_i = lax.broadcasted_iota(jnp.int32, (TM, TM), 0)
    c_i = lax.broadcasted_iota(jnp.int32, (TM, TM), 1)
    tri = jnp.where(c_i < r_i, 1.0, 0.0).astype(BF16)
    rank_all = jnp.dot(tri, sel.astype(BF16), preferred_element_type=F32)
    cnt_ref[0] = jnp.broadcast_to(jnp.sum(sel, axis=0, keepdims=True), (SUBLANES, LANES))

    idx_o = jnp.zeros((TM, LANES), jnp.int32)
    rank_o = jnp.zeros((TM, LANES), jnp.int32)
    gate_o = jnp.zeros((TM, LANES), F32)
    for k in range(TOP_K):
        rk = jnp.sum(jnp.where(hots[k], rank_all, 0.0), axis=-1, keepdims=True).astype(jnp.int32)
        idx_o = jnp.where(lane == k, idxs[k], idx_o)
        rank_o = jnp.where(lane == k, rk, rank_o)
        gate_o = jnp.where(lane == k, exps[k] / denom, gate_o)
    idx_ref[...] = idx_o
    rank_ref[...] = rank_o
    gate_ref[...] = gate_o


def _mix(x, attn_p, attn_s, conv_in, mod, wl, n_p_tiles, tiles_per_s_seq, ctx_row, alpha):
    t = x.shape[0]
    nt = t // TM
    hpt = TM // HALO
    n_halo = t // HALO

    def mod_idx(i):
        return (jnp.where(i < n_p_tiles, ctx_row, (i - n_p_tiles) // tiles_per_s_seq), 0, 0)

    full = lambda shp: pl.BlockSpec(shp, lambda i: (0,) * len(shp))
    row = lambda w: pl.BlockSpec((TM, w), lambda i: (i, 0))
    return pl.pallas_call(
        functools.partial(_mix_kernel, n_p_tiles, tiles_per_s_seq, alpha),
        grid=(nt,),
        in_specs=[row(D_MODEL),
                  pl.BlockSpec((HEADS // 2, TM, LANES), lambda i: (0, jnp.minimum(i, n_p_tiles - 1), 0)),
                  pl.BlockSpec((HEADS // 2, TM, LANES), lambda i: (0, jnp.maximum(i - n_p_tiles, 0), 0)),
                  pl.BlockSpec((HALO, 3 * SC_DIM), lambda i: (jnp.maximum(i * hpt - 1, 0), 0)),
                  row(3 * SC_DIM),
                  pl.BlockSpec((HALO, 3 * SC_DIM), lambda i: (jnp.minimum((i + 1) * hpt, n_halo - 1), 0)),
                  pl.BlockSpec((1, 1, 6 * D_MODEL), mod_idx),
                  full((SC_WIDTH, SC_DIM)), full((CF_WIDTH, CF_DIM)), full((1, CF_DIM)),
                  full((1, CF_DIM)), full((1, CF_DIM)),
                  full((D_MODEL, D_MODEL)), full((1, D_MODEL)), full((1, D_MODEL)),
                  full((D_MODEL, 2 * LANES)), full((1, LANES))],
        out_specs=[row(D_MODEL), row(D_MODEL),
                   row(LANES), row(LANES), row(LANES),
                   pl.BlockSpec((1, SUBLANES, LANES), lambda i: (i, 0, 0))],
        out_shape=[jax.ShapeDtypeStruct((t, D_MODEL), F32),
                   jax.ShapeDtypeStruct((t, D_MODEL), BF16),
                   jax.ShapeDtypeStruct((t, LANES), jnp.int32),
                   jax.ShapeDtypeStruct((t, LANES), jnp.int32),
                   jax.ShapeDtypeStruct((t, LANES), F32),
                   jax.ShapeDtypeStruct((nt, SUBLANES, LANES), F32)],
        scratch_shapes=[pltpu.VMEM((TM + 2 * HALO, 2 * SC_DIM), F32),
                        pltpu.VMEM((SUBLANES - 1, TM + 2 * HALO - SUBLANES, CF_DIM), F32)],
        compiler_params=_cparams("arbitrary"),
        name="mix",
    )(x, attn_p, attn_s, conv_in, conv_in, conv_in, mod,
      wl["w_sconv"], wl["w_cconv"], wl["b_cconv"], wl["g_cf"], wl["b_cf"],
      wl["w_out"], wl["g_ln1"], wl["b_ln1"], wl["w_router"], wl["b_router"])


def _segment_copies(n, make_copy, start=True):
    pos = jnp.int32(0)
    for size in SEG_SIZES:
        take = n & size

        @pl.when(take != 0)
        def _(pos=pos, size=size):
            cp = make_copy(pos, size)
            cp.start() if start else cp.wait()

        pos = pos + take


def _dispatch_kernel(a_ref, off_ref, seg_ref, tot_ref, tail0_ref, tailn_ref, nu_ref,
                     h_ref, qt_ref, xs_hbm, buf, zbuf, sem, zsem):
    i = pl.program_id(0)
    last = pl.num_programs(0) - 1
    slot = i % 2

    r = lax.broadcasted_iota(jnp.int32, (R_TILE, TM), 0)
    hit = r == qt_ref[0:1, :]
    for k in range(1, TOP_K):
        hit = jnp.logical_or(hit, r == qt_ref[k:k + 1, :])
    perm = jnp.where(hit, 1.0, 0.0).astype(BF16)
    buf[slot] = jnp.dot(perm, h_ref[...], preferred_element_type=F32)

    def wait_tile(j, s):
        n = pl.multiple_of(tot_ref[j], SEG)

        @pl.when(n > 0)
        def _():
            pltpu.make_async_copy(buf.at[s, pl.ds(0, n)], xs_hbm.at[pl.ds(0, n)], sem.at[s]).wait()

    @pl.when(i > 0)
    def _():
        wait_tile(i - 1, 1 - slot)

    def per_expert(e, carry):
        n = a_ref[i * N_EXPERTS + e]
        src = off_ref[i * N_EXPERTS + e]
        dst = seg_ref[i * N_EXPERTS + e]
        _segment_copies(n, lambda pos, size: pltpu.make_async_copy(
            buf.at[slot, pl.ds(pl.multiple_of(src + pos, SEG), size)],
            xs_hbm.at[pl.ds(pl.multiple_of(dst + pos, SEG), size)], sem.at[slot]))
        return carry

    lax.fori_loop(0, N_EXPERTS, per_expert, 0)

    @pl.when(i == last)
    def _():
        zbuf[...] = jnp.zeros_like(zbuf)

        def tail(start):
            def per_tail(e, carry):
                dst = tail0_ref[e]
                _segment_copies(tailn_ref[e], lambda pos, size: pltpu.make_async_copy(
                    zbuf.at[pl.ds(0, size)], xs_hbm.at[pl.ds(pl.multiple_of(dst + pos, SEG), size)], zsem),
                    start=start)
                return carry
            lax.fori_loop(0, N_EXPERTS, per_tail, 0)

        def spare(b):
            return pltpu.make_async_copy(zbuf, xs_hbm.at[pl.ds(pl.multiple_of(b * MB, MB), MB)], zsem)

        def start_spare(b, carry):
            spare(b).start()
            return carry

        def wait_spare(b, carry):
            spare(b).wait()
            return carry

        n_blocks = xs_hbm.shape[0] // MB
        tail(True)
        lax.fori_loop(nu_ref[0], n_blocks, start_spare, 0)
        tail(False)
        lax.fori_loop(nu_ref[0], n_blocks, wait_spare, 0)
        wait_tile(i, slot)


def _dispatch(tables, n_used, h2, q_t, p_rows):
    t = h2.shape[0]
    assert SEG_SIZES[0] == MB
    grid_spec = pltpu.PrefetchScalarGridSpec(
        num_scalar_prefetch=7,
        grid=(t // TM,),
        in_specs=[pl.BlockSpec((TM, D_MODEL), lambda i, *_: (i, 0)),
                  pl.BlockSpec((TOP_K, TM), lambda i, *_: (0, i))],
        out_specs=pl.BlockSpec(memory_space=pl.ANY),
        scratch_shapes=[pltpu.VMEM((2, R_TILE, D_MODEL), F32), pltpu.VMEM((SEG_SIZES[0], D_MODEL), F32),
                        pltpu.SemaphoreType.DMA((2,)), pltpu.SemaphoreType.DMA(())],
    )
    return pl.pallas_call(
        _dispatch_kernel,
        grid_spec=grid_spec,
        out_shape=jax.ShapeDtypeStruct((p_rows, D_MODEL), F32),
        compiler_params=_cparams("arbitrary"),
        name="dispatch",
    )(tables["a"], tables["off"], tables["seg"], tables["tot"], tables["tail0"], tables["tailn"], n_used, h2, q_t)


def _expert_kernel(be_ref, nu_ref, xs_ref, w1_ref, b1_ref, w2_ref, b2_ref, ys_ref, w1_scr, w2_scr):
    i = pl.program_id(0)
    used = i < nu_ref[0]

    @pl.when(jnp.logical_and(used, jnp.logical_or(i == 0, be_ref[i] != be_ref[jnp.maximum(i - 1, 0)])))
    def _():
        w1_scr[...] = w1_ref[0, 0].astype(BF16)
        w2_scr[...] = w2_ref[0, 0].astype(BF16)

    @pl.when(used)
    def _():
        z = jnp.dot(xs_ref[...].astype(BF16), w1_scr[...], preferred_element_type=F32) + b1_ref[0, 0]
        gate = jnp.minimum(z[:, :D_EXPERT], SWIGLU_LIMIT)
        up = jnp.clip(z[:, D_EXPERT:], -SWIGLU_LIMIT, SWIGLU_LIMIT)
        act = (up + 1.0) * gate * jax.nn.sigmoid(SWIGLU_ALPHA * gate)
        ys_ref[...] = jnp.dot(act.astype(BF16), w2_scr[...], preferred_element_type=F32) + b2_ref[0, 0]

    @pl.when(jnp.logical_not(used))
    def _():
        ys_ref[...] = jnp.zeros_like(ys_ref)


def _experts(blk_expert, n_used, xs, l, w_exp_in, b_exp_in, w_exp_out, b_exp_out):
    p_rows = xs.shape[0]
    nb = p_rows // MB
    depth = w_exp_in.shape[0]
    blk = lambda i, nu: jnp.maximum(jnp.minimum(i, nu[0] - 1), 0)
    wmap = lambda i, be, nu: (l, be[blk(i, nu)], 0, 0)
    xmap = lambda i, be, nu: (blk(i, nu), 0)
    grid_spec = pltpu.PrefetchScalarGridSpec(
        num_scalar_prefetch=2,
        grid=(nb,),
        in_specs=[pl.BlockSpec((MB, D_MODEL), xmap),
                  pl.BlockSpec((1, 1, D_MODEL, 2 * D_EXPERT), wmap),
                  pl.BlockSpec((1, 1, 1, 2 * D_EXPERT), wmap),
                  pl.BlockSpec((1, 1, D_EXPERT, D_MODEL), wmap),
                  pl.BlockSpec((1, 1, 1, D_MODEL), wmap)],
        out_specs=pl.BlockSpec((MB, D_MODEL), lambda i, be, nu: (i, 0)),
        scratch_shapes=[pltpu.VMEM((D_MODEL, 2 * D_EXPERT), BF16), pltpu.VMEM((D_EXPERT, D_MODEL), BF16)],
    )
    return pl.pallas_call(
        _expert_kernel,
        grid_spec=grid_spec,
        out_shape=jax.ShapeDtypeStruct((p_rows, D_MODEL), F32),
        compiler_params=_cparams("arbitrary"),
        name="experts",
    )(blk_expert, n_used, xs, w_exp_in, b_exp_in.reshape(depth, N_EXPERTS, 1, -1),
      w_exp_out, b_exp_out.reshape(depth, N_EXPERTS, 1, -1))


def _combine_kernel(alpha, n_p_tiles, split, a_ref, off_ref, seg_ref, tot_ref,
                    q_ref, gate_ref, x1_ref, mod_ref, g2_ref, b2_ref, ys_hbm, *rest):
    if split:
        o_p_ref, o_s_ref, buf, sem = rest
    else:
        o_ref, buf, sem = rest
    i = pl.program_id(0)
    slot = i % 2

    def fetch(j, s):
        def per_expert(e, carry):
            n = a_ref[j * N_EXPERTS + e]
            src = seg_ref[j * N_EXPERTS + e]
            dst = off_ref[j * N_EXPERTS + e]
            _segment_copies(n, lambda pos, size: pltpu.make_async_copy(
                ys_hbm.at[pl.ds(pl.multiple_of(src + pos, SEG), size)],
                buf.at[s, pl.ds(pl.multiple_of(dst + pos, SEG), size)], sem.at[s]))
            return carry

        lax.fori_loop(0, N_EXPERTS, per_expert, 0)

    @pl.when(i == 0)
    def _():
        buf[...] = jnp.zeros_like(buf)
        fetch(0, 0)

    @pl.when(i + 1 < pl.num_programs(0))
    def _():
        fetch(i + 1, 1 - slot)

    n = pl.multiple_of(tot_ref[i], SEG)

    @pl.when(n > 0)
    def _():
        pltpu.make_async_copy(ys_hbm.at[pl.ds(0, n)], buf.at[slot, pl.ds(0, n)], sem.at[slot]).wait()

    lane = lax.broadcasted_iota(jnp.int32, (TM, R_TILE), 1)
    q = q_ref[...]
    gates = gate_ref[...]
    g = jnp.zeros((TM, R_TILE), F32)
    for k in range(TOP_K):
        g = jnp.where(lane == q[:, k:k + 1], gates[:, k:k + 1], g)
    ffn = jnp.dot(g.astype(BF16), buf[slot].astype(BF16), preferred_element_type=F32)
    g2 = mod_ref[0][:, 5 * D_MODEL:6 * D_MODEL]
    res = _ln_plain(alpha * x1_ref[...] + g2 * ffn) * g2_ref[...] + b2_ref[...]
    if split:
        @pl.when(i < n_p_tiles)
        def _():
            o_p_ref[...] = res

        @pl.when(i >= n_p_tiles)
        def _():
            o_s_ref[...] = res
    else:
        o_ref[...] = res


def _combine(tables, q, gates, x1, mod, ys, wl, n_p_tiles, tiles_per_s_seq, ctx_row, alpha, split):
    t = x1.shape[0]
    nt = t // TM

    def mod_idx(i, *_):
        return (jnp.where(i < n_p_tiles, ctx_row, (i - n_p_tiles) // tiles_per_s_seq), 0, 0)

    if split:
        out_specs = [pl.BlockSpec((TM, D_MODEL), lambda i, *_: (jnp.minimum(i, n_p_tiles - 1), 0)),
                     pl.BlockSpec((TM, D_MODEL), lambda i, *_: (jnp.maximum(i - n_p_tiles, 0), 0))]
        out_shape = [jax.ShapeDtypeStruct((n_p_tiles * TM, D_MODEL), F32),
                     jax.ShapeDtypeStruct((t - n_p_tiles * TM, D_MODEL), F32)]
    else:
        out_specs = pl.BlockSpec((TM, D_MODEL), lambda i, *_: (i, 0))
        out_shape = jax.ShapeDtypeStruct((t, D_MODEL), F32)
    grid_spec = pltpu.PrefetchScalarGridSpec(
        num_scalar_prefetch=4,
        grid=(nt,),
        in_specs=[pl.BlockSpec((TM, TOP_K), lambda i, *_: (i, 0)),
                  pl.BlockSpec((TM, LANES), lambda i, *_: (i, 0)),
                  pl.BlockSpec((TM, D_MODEL), lambda i, *_: (i, 0)),
                  pl.BlockSpec((1, 1, 6 * D_MODEL), mod_idx),
                  pl.BlockSpec((1, D_MODEL), lambda i, *_: (0, 0)),
                  pl.BlockSpec((1, D_MODEL), lambda i, *_: (0, 0)),
                  pl.BlockSpec(memory_space=pl.ANY)],
        out_specs=out_specs,
        scratch_shapes=[pltpu.VMEM((2, R_TILE, D_MODEL), F32), pltpu.SemaphoreType.DMA((2,))],
    )
    return pl.pallas_call(
        functools.partial(_combine_kernel, alpha, n_p_tiles, split),
        grid_spec=grid_spec,
        out_shape=out_shape,
        compiler_params=_cparams("arbitrary"),
        name="combine",
    )(tables["a"], tables["off"], tables["seg"], tables["tot"], q, gates, x1, mod, wl["g_ln2"], wl["b_ln2"], ys)


def _routing_tables(counts, idx, lrank, n_blocks):
    cnt = counts[:, 0, :N_EXPERTS].astype(jnp.int32)
    a = (cnt + SEG - 1) // SEG * SEG
    off = jnp.cumsum(a, axis=1) - a
    acc = jnp.cumsum(a, axis=0) - a
    tot_e = jnp.sum(a, axis=0)
    padded = (tot_e + MB - 1) // MB * MB
    pend = jnp.cumsum(padded)
    pstart = pend - padded
    seg = pstart[None, :] + acc
    blk_row0 = jnp.arange(n_blocks, dtype=jnp.int32) * MB
    blk_expert = jnp.minimum(jnp.sum((pend[None, :] <= blk_row0[:, None]).astype(jnp.int32), axis=1), N_EXPERTS - 1)
    n_used = (pend[-1:] // MB).astype(jnp.int32)
    off_tok = jnp.repeat(off, TM, axis=0)
    hot = idx[:, :TOP_K, None] == jnp.arange(N_EXPERTS, dtype=jnp.int32)
    q = jnp.sum(jnp.where(hot, off_tok[:, None, :], 0), axis=-1) + lrank[:, :TOP_K]
    tables = {"a": a.reshape(-1), "off": off.reshape(-1), "seg": seg.reshape(-1), "tot": jnp.sum(a, axis=1),
              "tail0": pstart + tot_e, "tailn": padded - tot_e}
    return tables, q, blk_expert, n_used


def _pad_cols(w, width):
    return jnp.pad(w, ((0, 0), (0, width - w.shape[1])))


def _hi_lo(w):
    hi = w.astype(BF16)
    return jnp.concatenate([hi, (w - hi.astype(F32)).astype(BF16)], axis=1)


def _rot_cols(w):
    half = w.shape[1] // 2
    return jnp.concatenate([-w[:, half:], w[:, :half]], axis=1)


def _layer_weights(l, w_in, g_q_norm, w_uq, g_kv_norm, w_uk, w_uv, w_sconv, w_cconv, b_cconv,
                   g_cf_norm, b_cf_norm, w_out, g_ln1, b_ln1, g_ln2, b_ln2, w_router, b_router):
    wi = w_in[l]
    o_ckv, o_kr, o_bg = Q_RANK, Q_RANK + KV_RANK, Q_RANK + KV_RANK + QK_ROPE
    w_kr = wi[:, o_kr:o_bg]
    w_in_p = jnp.concatenate([
        _pad_cols(wi[:, :Q_RANK], COL_CKV - COL_CQ),
        wi[:, o_ckv:o_kr],
        _pad_cols(jnp.concatenate([w_kr, _rot_cols(w_kr)], axis=1), COL_BG - COL_KR),
        wi[:, o_bg:]], axis=1).astype(BF16)

    uq = jnp.pad(w_uq[l], ((0, Q_RANK_PAD - Q_RANK), (0, 0))).reshape(Q_RANK_PAD, HEADS, QK_DIM)
    zer = jnp.zeros((Q_RANK_PAD, HEADS, HEAD_PAD - QK_DIM), F32)
    wq_a = jnp.concatenate([uq, zer], axis=2).reshape(Q_RANK_PAD, HEADS * HEAD_PAD).astype(BF16)
    rot = jnp.concatenate([-uq[:, :, QK_NOPE + QK_ROPE // 2:], uq[:, :, QK_NOPE:QK_NOPE + QK_ROPE // 2]], axis=2)
    wq_b = jnp.concatenate([jnp.zeros((Q_RANK_PAD, HEADS, QK_NOPE), F32), rot, zer],
                           axis=2).reshape(Q_RANK_PAD, HEADS * HEAD_PAD).astype(BF16)

    uk_t = w_uk[l].reshape(KV_RANK, HEADS, QK_NOPE).transpose(1, 2, 0)
    top = jnp.pad(uk_t, ((0, 0), (0, 0), (0, KFEAT - KV_RANK)))
    eye = jnp.pad(jnp.eye(QK_ROPE, dtype=F32), ((0, 0), (KV_RANK, KFEAT - KV_RANK - QK_ROPE)))
    mid = jnp.broadcast_to(eye[None], (HEADS, QK_ROPE, KFEAT))
    wk_t = jnp.concatenate([top, mid, jnp.zeros((HEADS, HEAD_PAD - QK_DIM, KFEAT), F32)], axis=1).astype(BF16)

    uv = w_uv[l].reshape(KV_RANK, HEADS, V_DIM)
    zv = jnp.zeros((KV_RANK, HEADS, V_DIM), F32)
    odd = (jnp.arange(HEADS) % 2 == 1)[None, :, None]
    wv = jnp.concatenate([jnp.where(odd, zv, uv), jnp.where(odd, uv, zv)], axis=2).reshape(KV_RANK, HEADS * LANES)
    wv = jnp.pad(wv, ((0, KFEAT - KV_RANK), (0, 0))).astype(BF16)

    r1 = lambda v: v.reshape(1, -1)
    return {
        "w_in": w_in_p,
        "g_q": jnp.pad(g_q_norm[l], (0, Q_RANK_PAD - Q_RANK)).reshape(1, -1),
        "g_kv": r1(g_kv_norm[l]),
        "wq_a": wq_a, "wq_b": wq_b, "wk_t": wk_t, "wv": wv,
        "w_sconv": w_sconv[l], "w_cconv": w_cconv[l], "b_cconv": r1(b_cconv[l]),
        "g_cf": r1(g_cf_norm[l]), "b_cf": r1(b_cf_norm[l]),
        "w_out": w_out[l].astype(BF16),
        "g_ln1": r1(g_ln1[l]), "b_ln1": r1(b_ln1[l]), "g_ln2": r1(g_ln2[l]), "b_ln2": r1(b_ln2[l]),
        "w_router": _hi_lo(_pad_cols(w_router[l], LANES)),
        "b_router": jnp.pad(b_router[l], (0, LANES - N_EXPERTS)).reshape(1, -1),
    }


def _rope_tables(n_lat):
    rows = n_lat // GRID_W
    row = jnp.repeat(jnp.arange(rows, dtype=F32), GRID_W)
    col = jnp.tile(jnp.arange(GRID_W, dtype=F32), rows)
    axis_dim = QK_ROPE // 2
    inv = ROPE_BASE ** (-jnp.arange(0, axis_dim, 2, dtype=F32) / axis_dim)
    ang = jnp.concatenate([row[:, None] * inv, col[:, None] * inv], axis=-1)
    cos, sin = jnp.cos(ang), jnp.sin(ang)
    one = jnp.ones((n_lat, QK_NOPE), F32)
    zq = jnp.zeros((n_lat, HEAD_PAD - QK_DIM), F32)
    zk = jnp.zeros((n_lat, LANES - QK_ROPE), F32)
    return (jnp.concatenate([one, cos, cos, zq], axis=1),
            jnp.concatenate([0.0 * one, sin, sin, zq], axis=1),
            jnp.concatenate([cos, cos, zk], axis=1),
            jnp.concatenate([sin, sin, zk], axis=1))


def kernel(x_prompt, x_sample, cache_ckv, cache_krope, c, c_ctx, w_mod, b_mod, w_in, g_q_norm, w_uq, g_kv_norm, w_uk, w_uv, w_sconv, w_cconv, b_cconv, g_cf_norm, b_cf_norm, w_out, g_ln1, b_ln1, g_ln2, b_ln2, w_router, b_router, w_exp_in, b_exp_in, w_exp_out, b_exp_out):
    bp, n_p, d = x_prompt.shape
    bs, n_s, _ = x_sample.shape
    depth = w_mod.shape[0]
    past = cache_ckv.shape[2]
    assert d == D_MODEL and n_p == TM and n_s % TM == 0 and n_s % GRID_W == 0
    t_p, t_s = bp * n_p, bs * n_s
    assert t_p % n_s == 0 and past % LANES == 0
    t = t_p + t_s
    n_p_tiles = t_p // TM
    tps = n_s // TM
    alpha = float((2 * depth) ** 0.25)

    ctx_row = bs
    rows = -(-(bs + 1) // SUBLANES) * SUBLANES
    c_rows = jnp.concatenate([c, c_ctx[None], jnp.zeros((rows - bs - 1, d), F32)], axis=0)
    mod_all = _modulation(c_rows, w_mod, b_mod)

    tabs = _rope_tables(n_s)
    x = jnp.concatenate([x_prompt.reshape(t_p, d), x_sample.reshape(t_s, d)], axis=0)
    nt = t // TM
    n_blocks = -(-(t * TOP_K + nt * N_EXPERTS * (SEG - 1)) // MB) + N_EXPERTS
    p_rows = n_blocks * MB

    new_ckv, new_krope = [], []
    for l in range(depth):
        wl = _layer_weights(l, w_in, g_q_norm, w_uq, g_kv_norm, w_uk, w_uv, w_sconv, w_cconv, b_cconv,
                            g_cf_norm, b_cf_norm, w_out, g_ln1, b_ln1, g_ln2, b_ln2, w_router, b_router)
        mod = mod_all[l].reshape(rows, 1, 6 * d)
        q, ckv, krope, kfeat, conv_in = _in_proj(x, mod, wl, tabs, n_p_tiles, tps, ctx_row)
        new_ckv.append(ckv[:t_p].reshape(bp, n_p, KV_RANK))
        new_krope.append(krope[:t_p].reshape(bp, n_p, QK_ROPE))

        kcache = jnp.concatenate(
            [cache_ckv[:, l], cache_krope[:, l], jnp.zeros((bs, past, KFEAT - KV_RANK - QK_ROPE), F32)],
            axis=-1).astype(BF16)
        attn_p = _attention(q, kfeat, None, wl, bp, n_p, 0)
        attn_s = _attention(q, kfeat, kcache, wl, bs, n_s, t_p)
        x1, h2, idx, lrank, gates, counts = _mix(x, attn_p, attn_s, conv_in, mod, wl, n_p_tiles, tps,
                                                 ctx_row, alpha)
        tables, q, blk_expert, n_used = _routing_tables(counts, idx, lrank, n_blocks)
        xs = _dispatch(tables, n_used, h2, q.T, p_rows)
        ys = _experts(blk_expert, n_used, xs, l, w_exp_in, b_exp_in, w_exp_out, b_exp_out)
        x = _combine(tables, q, gates, x1, mod, ys, wl, n_p_tiles, tps, ctx_row, alpha, split=l == depth - 1)

    y_p, y_s = x
    return (y_p.reshape(bp, n_p, d), y_s.reshape(bs, n_s, d),
            jnp.stack(new_ckv, axis=1), jnp.stack(new_krope, axis=1))
```

```python
import functools
import math

import jax
import jax.numpy as jnp
from jax import lax
from jax.experimental import pallas as pl
from jax.experimental.pallas import tpu as pltpu

F32 = jnp.float32
BF16 = jnp.bfloat16

D_MODEL = 1024
GRID_W = 64
HEADS = 8
QK_NOPE = 64
QK_ROPE = 32
QK_DIM = QK_NOPE + QK_ROPE
V_DIM = 64
Q_RANK = 192
KV_RANK = 128
SC_DIM = 256
SC_WIDTH = 3
CF_DIM = 256
CF_WIDTH = 31
N_EXPERTS = 32
TOP_K = 4
D_EXPERT = 1024
SWIGLU_LIMIT = 7.0
SWIGLU_ALPHA = 1.702
ROPE_BASE = 10000.0
EPS = 1e-6

LANES = 128
SUBLANES = 8

TM = 256
TQ = 512
MB = 512
HALO = 16
SEG = SUBLANES
SEG_SIZES = tuple(MB >> s for s in range((MB // SEG).bit_length()))
R_TILE = TM * TOP_K + N_EXPERTS * SEG
MXU_ROWSUM_MIN_KEYS = 2048
DISPATCH_SLOTS = 3
HEAD_PAD = LANES
Q_RANK_PAD = 256
KFEAT = 256
ONES_FEAT = KV_RANK + QK_ROPE
COL_CQ, COL_CKV, COL_KR, COL_BG, COL_CG, COL_XS, COL_GA, COL_GB, IN_COLS_PAD = (
    0, 256, 384, 512, 768, 1024, 1280, 1536, 1792)
VMEM_LIMIT = 60 * 1024 * 1024


def _cparams(*sem):
    return pltpu.CompilerParams(dimension_semantics=sem, vmem_limit_bytes=VMEM_LIMIT)


def _ln_plain(x):
    mu = jnp.mean(x, axis=-1, keepdims=True)
    xc = x - mu
    var = jnp.mean(xc * xc, axis=-1, keepdims=True)
    return xc * lax.rsqrt(var + EPS)


def _mod_kernel(c_ref, w_ref, b_ref, o_ref):
    c = c_ref[...]
    s = c * jax.nn.sigmoid(c)
    o_ref[0] = jnp.dot(s, w_ref[0], preferred_element_type=F32,
                       precision=lax.Precision.HIGHEST) + b_ref[0]


def _modulation(c_rows, w_mod, b_mod):
    depth = w_mod.shape[0]
    rows = c_rows.shape[0]
    nblk = w_mod.shape[2] // D_MODEL
    return pl.pallas_call(
        _mod_kernel,
        grid=(depth, nblk),
        in_specs=[pl.BlockSpec((rows, D_MODEL), lambda l, j: (0, 0)),
                  pl.BlockSpec((1, D_MODEL, D_MODEL), lambda l, j: (l, 0, j)),
                  pl.BlockSpec((1, 1, D_MODEL), lambda l, j: (l, 0, j))],
        out_specs=pl.BlockSpec((1, rows, D_MODEL), lambda l, j: (l, 0, j)),
        out_shape=jax.ShapeDtypeStruct((depth, rows, w_mod.shape[2]), F32),
        compiler_params=_cparams("arbitrary", "arbitrary"),
        name="modulation",
    )(c_rows, w_mod, b_mod.reshape(depth, 1, -1))


def _in_proj_kernel(n_p_tiles, xp_ref, xs_ref, mod_ref, win_ref, gq_ref, gkv_ref, wqa_ref, wqb_ref,
                    tcq_ref, tsq_ref, tck_ref, tsk_ref,
                    q_ref, ckv_ref, kr_ref, kf_ref, cv_ref):
    is_s = pl.program_id(0) >= n_p_tiles
    mod = mod_ref[0]
    x = jnp.where(is_s, xs_ref[...], xp_ref[...])
    h = _ln_plain(x) * (1.0 + mod[:, D_MODEL:2 * D_MODEL]) + mod[:, 0:D_MODEL]
    z = jnp.dot(h.astype(BF16), win_ref[...], preferred_element_type=F32)

    cq = z[:, COL_CQ:COL_CQ + Q_RANK_PAD]
    cqn = cq * lax.rsqrt(jnp.sum(cq * cq, axis=-1, keepdims=True) * (1.0 / Q_RANK) + EPS) * gq_ref[...]
    cqn = cqn.astype(BF16)
    qa = jnp.dot(cqn, wqa_ref[...], preferred_element_type=F32)
    qb = jnp.dot(cqn, wqb_ref[...], preferred_element_type=F32)
    cos_q = jnp.where(is_s, tcq_ref[...], 1.0)
    sin_q = jnp.where(is_s, tsq_ref[...], 0.0)
    scale = math.log2(math.e) / math.sqrt(QK_DIM)
    for hd in range(HEADS):
        sl = slice(hd * HEAD_PAD, (hd + 1) * HEAD_PAD)
        q_ref[hd] = ((qa[:, sl] * cos_q + qb[:, sl] * sin_q) * scale).astype(BF16)

    cr = z[:, COL_CKV:COL_CKV + KV_RANK]
    ckv = cr * lax.rsqrt(jnp.mean(cr * cr, axis=-1, keepdims=True) + EPS) * gkv_ref[...]
    ckv_ref[...] = ckv
    zk = z[:, COL_KR:COL_KR + LANES]
    kr_ref[...] = zk[:, 0:QK_ROPE]
    lane = lax.broadcasted_iota(jnp.int32, zk.shape, 1)
    raw = jnp.where(lane < QK_ROPE, zk, 0.0)
    roped = zk * tck_ref[...] + pltpu.roll(zk, LANES - QK_ROPE, 1) * tsk_ref[...]
    kf_ref[:, 0:KV_RANK] = ckv.astype(BF16)
    kf_ref[:, KV_RANK:KFEAT] = jnp.where(lane == ONES_FEAT - KV_RANK, 1.0, jnp.where(is_s, roped, raw)).astype(BF16)

    cv_ref[:, 0:SC_DIM] = z[:, COL_BG:COL_BG + SC_DIM]
    cv_ref[:, SC_DIM:2 * SC_DIM] = z[:, COL_CG:COL_CG + SC_DIM] * z[:, COL_XS:COL_XS + SC_DIM]
    cv_ref[:, 2 * SC_DIM:] = z[:, COL_GA:COL_GA + CF_DIM] * jax.nn.sigmoid(z[:, COL_GB:COL_GB + CF_DIM])


def _x_specs(n_p_tiles):
    return [pl.BlockSpec((TM, D_MODEL), lambda i, *_: (jnp.minimum(i, n_p_tiles - 1), 0)),
            pl.BlockSpec((TM, D_MODEL), lambda i, *_: (jnp.maximum(i - n_p_tiles, 0), 0))]


def _in_proj(x_p, x_s, mod, wl, tabs, n_p_tiles, tiles_per_s_seq, ctx_row):
    t = x_p.shape[0] + x_s.shape[0]
    nt = t // TM

    def mod_idx(i):
        return (jnp.where(i < n_p_tiles, ctx_row, (i - n_p_tiles) // tiles_per_s_seq), 0, 0)

    def pos_idx(i):
        return (jnp.where(i < n_p_tiles, 0, (i - n_p_tiles) % tiles_per_s_seq), 0)

    full = lambda shp: pl.BlockSpec(shp, lambda i: (0,) * len(shp))
    tab = pl.BlockSpec((TM, LANES), pos_idx)
    return pl.pallas_call(
        functools.partial(_in_proj_kernel, n_p_tiles),
        grid=(nt,),
        in_specs=_x_specs(n_p_tiles) + [
                  pl.BlockSpec((1, 1, 6 * D_MODEL), mod_idx),
                  full((D_MODEL, IN_COLS_PAD)),
                  full((1, Q_RANK_PAD)), full((1, KV_RANK)),
                  full((Q_RANK_PAD, HEADS * HEAD_PAD)), full((Q_RANK_PAD, HEADS * HEAD_PAD)),
                  tab, tab, tab, tab],
        out_specs=[pl.BlockSpec((HEADS, TM, HEAD_PAD), lambda i: (0, i, 0)),
                   pl.BlockSpec((TM, KV_RANK), lambda i: (i, 0)),
                   pl.BlockSpec((TM, QK_ROPE), lambda i: (i, 0)),
                   pl.BlockSpec((TM, KFEAT), lambda i: (i, 0)),
                   pl.BlockSpec((TM, 3 * SC_DIM), lambda i: (i, 0))],
        out_shape=[jax.ShapeDtypeStruct((HEADS, t, HEAD_PAD), BF16),
                   jax.ShapeDtypeStruct((t, KV_RANK), F32),
                   jax.ShapeDtypeStruct((t, QK_ROPE), F32),
                   jax.ShapeDtypeStruct((t, KFEAT), BF16),
                   jax.ShapeDtypeStruct((t, 3 * SC_DIM), F32)],
        compiler_params=_cparams("arbitrary"),
        name="in_proj",
    )(x_p, x_s, mod, wl["w_in"], wl["g_q"], wl["g_kv"], wl["wq_a"], wl["wq_b"], *tabs)


def _attn_kernel(n_own, n_cache, q_ref, kf_ref, *rest):
    if n_cache:
        kc_ref, wkt_ref, wv_ref, o_ref, kt_scr, v_scr = rest
    else:
        wkt_ref, wv_ref, o_ref, kt_scr, v_scr = rest

    @pl.when(pl.program_id(1) == 0)
    def _():
        def fill(feat, lo, n):
            for hd in range(HEADS):
                kt = lax.dot_general(wkt_ref[hd], feat, (((1,), (1,)), ((), ())),
                                     preferred_element_type=F32)
                kt_scr[hd, :, lo:lo + n] = kt.astype(BF16)
                v = jnp.dot(feat, wv_ref[:, hd * LANES:(hd + 1) * LANES], preferred_element_type=F32)
                v_scr[hd, lo:lo + n, :] = v.astype(BF16)

        fill(kf_ref[...], 0, n_own)
        if n_cache:
            fill(kc_ref[0], n_own, n_cache)

    mxu_rowsum = n_own + n_cache >= MXU_ROWSUM_MIN_KEYS
    lane = lax.broadcasted_iota(jnp.int32, o_ref.shape[1:], 1)
    for pair in range(HEADS // 2):
        outs, sums = [], []
        for hd in (2 * pair, 2 * pair + 1):
            s = jnp.dot(q_ref[hd], kt_scr[hd], preferred_element_type=F32)
            p = jnp.exp2(s - jnp.max(s, axis=-1, keepdims=True))
            o = jnp.dot(p.astype(BF16), v_scr[hd], preferred_element_type=F32)
            outs.append(o)
            ones_lane = V_DIM if hd % 2 == 0 else 0
            sums.append(o[:, ones_lane:ones_lane + 1] if mxu_rowsum else jnp.sum(p, axis=-1, keepdims=True))
        o_ref[pair] = jnp.where(lane < V_DIM, outs[0] / sums[0], outs[1] / sums[1]).astype(BF16)


def _attention(q, kfeat, kcache, wl, batch, n_own, row0):
    n_cache = 0 if kcache is None else kcache.shape[1]
    m = n_own + n_cache
    tq = min(TQ, n_own)
    nq = n_own // tq
    q0 = row0 // tq
    k0 = row0 // n_own
    in_specs = [pl.BlockSpec((HEADS, tq, HEAD_PAD), lambda b, j: (0, q0 + b * nq + j, 0)),
                pl.BlockSpec((n_own, KFEAT), lambda b, j: (k0 + b, 0))]
    args = [q, kfeat]
    if n_cache:
        in_specs.append(pl.BlockSpec((1, n_cache, KFEAT), lambda b, j: (b, 0, 0)))
        args.append(kcache)
    in_specs += [pl.BlockSpec((HEADS, HEAD_PAD, KFEAT), lambda b, j: (0, 0, 0)),
                 pl.BlockSpec((KFEAT, HEADS * LANES), lambda b, j: (0, 0))]
    args += [wl["wk_t"], wl["wv"]]
    return pl.pallas_call(
        functools.partial(_attn_kernel, n_own, n_cache),
        grid=(batch, nq),
        in_specs=in_specs,
        out_specs=pl.BlockSpec((HEADS // 2, tq, LANES), lambda b, j: (0, b * nq + j, 0)),
        out_shape=jax.ShapeDtypeStruct((HEADS // 2, batch * n_own, LANES), BF16),
        scratch_shapes=[pltpu.VMEM((HEADS, HEAD_PAD, m), BF16), pltpu.VMEM((HEADS, m, LANES), BF16)],
        compiler_params=_cparams("arbitrary", "arbitrary"),
        name="attention_cache" if n_cache else "attention_ctx",
    )(*args)


def _mix_kernel(n_p_tiles, tiles_per_s_seq, alpha,
                xp_ref, xs_ref, atp_ref, ats_ref, cprev_ref, ccur_ref, cnext_ref, mod_ref,
                wsc_ref, wcc_ref, bcc_ref, gcf_ref, bcf_ref, wout_ref, g1_ref, b1_ref,
                wr_ref, br_ref,
                x1_ref, h2_ref, idx_ref, rank_ref, gate_ref, cnt_ref,
                ext_scr, shf_scr):
    i = pl.program_id(0)
    is_s = i >= n_p_tiles
    jj = (i - n_p_tiles) % tiles_per_s_seq
    has_prev = jnp.logical_and(is_s, jj > 0)
    has_next = jnp.logical_and(is_s, jj < tiles_per_s_seq - 1)

    ext_scr[0:HALO, :] = jnp.where(has_prev, cprev_ref[:, SC_DIM:], 0.0)
    ext_scr[HALO:HALO + TM, :] = ccur_ref[:, SC_DIM:]
    ext_scr[HALO + TM:, :] = jnp.where(has_next, cnext_ref[:, SC_DIM:], 0.0)

    sconv = jnp.zeros((TM, SC_DIM), F32)
    for k in range(SC_WIDTH):
        o = HALO - SC_WIDTH // 2 + k
        sconv = sconv + ext_scr[o:o + TM, 0:SC_DIM] * wsc_ref[k:k + 1, :]
    sc_out = ccur_ref[:, 0:SC_DIM] * sconv

    n_sh = TM + 2 * HALO - SUBLANES
    for s in range(1, SUBLANES):
        shf_scr[s - 1] = ext_scr[s:s + n_sh, SC_DIM:]
    cconv = jnp.zeros((TM, CF_DIM), F32)
    for k in range(CF_WIDTH):
        o = HALO - CF_WIDTH // 2 + k
        s, base = o % SUBLANES, o - o % SUBLANES
        rows = ext_scr[base:base + TM, SC_DIM:] if s == 0 else shf_scr[s - 1, base:base + TM, :]
        cconv = cconv + rows * wcc_ref[k:k + 1, :]
    cf = _ln_plain(cconv + bcc_ref[...]) * gcf_ref[...] + bcf_ref[...]
    cf = cf * jax.nn.sigmoid(cf)

    at = [jnp.where(is_s, ats_ref[p], atp_ref[p]) for p in range(HEADS // 2)]
    mix_in = jnp.concatenate(at + [sc_out.astype(BF16), cf.astype(BF16)], axis=1)
    mix = jnp.dot(mix_in, wout_ref[...], preferred_element_type=F32)

    mod = mod_ref[0]
    g1 = mod[:, 2 * D_MODEL:3 * D_MODEL]
    sh2 = mod[:, 3 * D_MODEL:4 * D_MODEL]
    sc2 = mod[:, 4 * D_MODEL:5 * D_MODEL]
    x = jnp.where(is_s, xs_ref[...], xp_ref[...])
    x1 = _ln_plain(alpha * x + g1 * mix) * g1_ref[...] + b1_ref[...]
    x1_ref[...] = x1
    h2 = _ln_plain(x1) * (1.0 + sc2) + sh2
    h2_hi = h2.astype(BF16)
    h2_ref[...] = h2_hi
    h2_lo = (h2 - h2_hi.astype(F32)).astype(BF16)
    hi_both = jnp.dot(h2_hi, wr_ref[...], preferred_element_type=F32)
    lo_hi = jnp.dot(h2_lo, wr_ref[:, 0:LANES], preferred_element_type=F32)
    logits = hi_both[:, 0:LANES] + hi_both[:, LANES:] + lo_hi + br_ref[...]
    lane = lax.broadcasted_iota(jnp.int32, (TM, LANES), 1)
    lane_f = lane.astype(F32)
    neg = jnp.float32(-jnp.inf)
    cur = jnp.where(lane < N_EXPERTS, logits, neg)
    vals, hots, idxs = [], [], []
    for _ in range(TOP_K):
        mx = jnp.max(cur, axis=-1, keepdims=True)
        ik = jnp.min(jnp.where(cur == mx, lane_f, float(LANES)), axis=-1, keepdims=True)
        ik = jnp.minimum(ik, float(N_EXPERTS - 1))
        hot = lane_f == ik
        cur = jnp.where(hot, neg, cur)
        vals.append(mx)
        hots.append(hot)
        idxs.append(ik.astype(jnp.int32))
    exps = [jnp.exp(v - vals[0]) for v in vals]
    denom = exps[0] + exps[1] + exps[2] + exps[3]
    sel = jnp.where(hots[0] | hots[1] | hots[2] | hots[3], 1.0, 0.0)

    r_i = lax.broadcasted_iota(jnp.int32, (TM, TM), 0)
    c_i = lax.broadcasted_iota(jnp.int32, (TM, TM), 1)
    tri = jnp.where(c_i < r_i, 1.0, 0.0).astype(BF16)
    rank_all = jnp.dot(tri, sel.astype(BF16), preferred_element_type=F32)
    cnt_ref[0] = jnp.broadcast_to(jnp.sum(sel, axis=0, keepdims=True), (SUBLANES, LANES))

    idx_o = jnp.zeros((TM, LANES), jnp.int32)
    rank_o = jnp.zeros((TM, LANES), jnp.int32)
    gate_o = jnp.zeros((TM, LANES), F32)
    for k in range(TOP_K):
        rk = jnp.sum(jnp.where(hots[k], rank_all, 0.0), axis=-1, keepdims=True).astype(jnp.int32)
        idx_o = jnp.where(lane == k, idxs[k], idx_o)
        rank_o = jnp.where(lane == k, rk, rank_o)
        gate_o = jnp.where(lane == k, exps[k] / denom, gate_o)
    idx_ref[...] = idx_o
    rank_ref[...] = rank_o
    gate_ref[...] = gate_o


def _mix(x_p, x_s, attn_p, attn_s, conv_in, mod, wl, n_p_tiles, tiles_per_s_seq, ctx_row, alpha):
    t = x_p.shape[0] + x_s.shape[0]
    nt = t // TM
    hpt = TM // HALO
    n_halo = t // HALO

    def mod_idx(i):
        return (jnp.where(i < n_p_tiles, ctx_row, (i - n_p_tiles) // tiles_per_s_seq), 0, 0)

    full = lambda shp: pl.BlockSpec(shp, lambda i: (0,) * len(shp))
    row = lambda w: pl.BlockSpec((TM, w), lambda i: (i, 0))
    return pl.pallas_call(
        functools.partial(_mix_kernel, n_p_tiles, tiles_per_s_seq, alpha),
        grid=(nt,),
        in_specs=_x_specs(n_p_tiles) + [
                  pl.BlockSpec((HEADS // 2, TM, LANES), lambda i: (0, jnp.minimum(i, n_p_tiles - 1), 0)),
                  pl.BlockSpec((HEADS // 2, TM, LANES), lambda i: (0, jnp.maximum(i - n_p_tiles, 0), 0)),
                  pl.BlockSpec((HALO, 3 * SC_DIM), lambda i: (jnp.maximum(i * hpt - 1, 0), 0)),
                  row(3 * SC_DIM),
                  pl.BlockSpec((HALO, 3 * SC_DIM), lambda i: (jnp.minimum((i + 1) * hpt, n_halo - 1), 0)),
                  pl.BlockSpec((1, 1, 6 * D_MODEL), mod_idx),
                  full((SC_WIDTH, SC_DIM)), full((CF_WIDTH, CF_DIM)), full((1, CF_DIM)),
                  full((1, CF_DIM)), full((1, CF_DIM)),
                  full((D_MODEL, D_MODEL)), full((1, D_MODEL)), full((1, D_MODEL)),
                  full((D_MODEL, 2 * LANES)), full((1, LANES))],
        out_specs=[row(D_MODEL), row(D_MODEL),
                   row(LANES), row(LANES), row(LANES),
                   pl.BlockSpec((1, SUBLANES, LANES), lambda i: (i, 0, 0))],
        out_shape=[jax.ShapeDtypeStruct((t, D_MODEL), F32),
                   jax.ShapeDtypeStruct((t, D_MODEL), BF16),
                   jax.ShapeDtypeStruct((t, LANES), jnp.int32),
                   jax.ShapeDtypeStruct((t, LANES), jnp.int32),
                   jax.ShapeDtypeStruct((t, LANES), F32),
                   jax.ShapeDtypeStruct((nt, SUBLANES, LANES), F32)],
        scratch_shapes=[pltpu.VMEM((TM + 2 * HALO, 2 * SC_DIM), F32),
                        pltpu.VMEM((SUBLANES - 1, TM + 2 * HALO - SUBLANES, CF_DIM), F32)],
        compiler_params=_cparams("arbitrary"),
        name="mix",
    )(x_p, x_s, attn_p, attn_s, conv_in, conv_in, conv_in, mod,
      wl["w_sconv"], wl["w_cconv"], wl["b_cconv"], wl["g_cf"], wl["b_cf"],
      wl["w_out"], wl["g_ln1"], wl["b_ln1"], wl["w_router"], wl["b_router"])


def _segment_copies(n, make_copy, start=True):
    pos = jnp.int32(0)
    for size in SEG_SIZES:
        take = n & size

        @pl.when(take != 0)
        def _(pos=pos, size=size):
            cp = make_copy(pos, size)
            cp.start() if start else cp.wait()

        pos = pos + take


def _dispatch_kernel(a_ref, off_ref, seg_ref, tot_ref, tail0_ref, tailn_ref, nu_ref,
                     h_ref, qt_ref, xs_hbm, buf, zbuf, sem, zsem):
    i = pl.program_id(0)
    last = pl.num_programs(0) - 1
    n_slots = buf.shape[0]
    slot = i % n_slots

    r = lax.broadcasted_iota(jnp.int32, (R_TILE, TM), 0)
    hit = r == qt_ref[0:1, :]
    for k in range(1, TOP_K):
        hit = jnp.logical_or(hit, r == qt_ref[k:k + 1, :])
    perm = jnp.where(hit, 1.0, 0.0).astype(BF16)
    buf[slot] = jnp.dot(perm, h_ref[...], preferred_element_type=F32)

    def wait_tile(j, s):
        n = pl.multiple_of(tot_ref[j], SEG)

        @pl.when(n > 0)
        def _():
            pltpu.make_async_copy(buf.at[s, pl.ds(0, n)], xs_hbm.at[pl.ds(0, n)], sem.at[s]).wait()

    @pl.when(i >= n_slots - 1)
    def _():
        wait_tile(i + 1 - n_slots, (i + 1) % n_slots)

    def per_expert(e, carry):
        n = a_ref[i * N_EXPERTS + e]
        src = off_ref[i * N_EXPERTS + e]
        dst = seg_ref[i * N_EXPERTS + e]
        _segment_copies(n, lambda pos, size: pltpu.make_async_copy(
            buf.at[slot, pl.ds(pl.multiple_of(src + pos, SEG), size)],
            xs_hbm.at[pl.ds(pl.multiple_of(dst + pos, SEG), size)], sem.at[slot]))
        return carry

    lax.fori_loop(0, N_EXPERTS, per_expert, 0)

    @pl.when(i == last)
    def _():
        zbuf[...] = jnp.zeros_like(zbuf)

        def tail(start):
            def per_tail(e, carry):
                dst = tail0_ref[e]
                _segment_copies(tailn_ref[e], lambda pos, size: pltpu.make_async_copy(
                    zbuf.at[pl.ds(0, size)], xs_hbm.at[pl.ds(pl.multiple_of(dst + pos, SEG), size)], zsem),
                    start=start)
                return carry
            lax.fori_loop(0, N_EXPERTS, per_tail, 0)

        def spare(b):
            return pltpu.make_async_copy(zbuf, xs_hbm.at[pl.ds(pl.multiple_of(b * MB, MB), MB)], zsem)

        def start_spare(b, carry):
            spare(b).start()
            return carry

        def wait_spare(b, carry):
            spare(b).wait()
            return carry

        n_blocks = xs_hbm.shape[0] // MB
        tail(True)
        lax.fori_loop(nu_ref[0], n_blocks, start_spare, 0)
        tail(False)
        lax.fori_loop(nu_ref[0], n_blocks, wait_spare, 0)
        for back in range(n_slots - 1):
            @pl.when(i >= back)
            def _(back=back):
                wait_tile(i - back, (i - back) % n_slots)


def _dispatch(tables, n_used, h2, q_t, p_rows):
    t = h2.shape[0]
    assert SEG_SIZES[0] == MB
    grid_spec = pltpu.PrefetchScalarGridSpec(
        num_scalar_prefetch=7,
        grid=(t // TM,),
        in_specs=[pl.BlockSpec((TM, D_MODEL), lambda i, *_: (i, 0)),
                  pl.BlockSpec((TOP_K, TM), lambda i, *_: (0, i))],
        out_specs=pl.BlockSpec(memory_space=pl.ANY),
        scratch_shapes=[pltpu.VMEM((DISPATCH_SLOTS, R_TILE, D_MODEL), F32), pltpu.VMEM((SEG_SIZES[0], D_MODEL), F32),
                        pltpu.SemaphoreType.DMA((DISPATCH_SLOTS,)), pltpu.SemaphoreType.DMA(())],
    )
    return pl.pallas_call(
        _dispatch_kernel,
        grid_spec=grid_spec,
        out_shape=jax.ShapeDtypeStruct((p_rows, D_MODEL), F32),
        compiler_params=_cparams("arbitrary"),
        name="dispatch",
    )(tables["a"], tables["off"], tables["seg"], tables["tot"], tables["tail0"], tables["tailn"], n_used, h2, q_t)


def _expert_kernel(be_ref, nu_ref, xs_ref, w1_ref, b1_ref, w2_ref, b2_ref, ys_ref, w1_scr, w2_scr):
    i = pl.program_id(0)
    used = i < nu_ref[0]

    @pl.when(jnp.logical_and(used, jnp.logical_or(i == 0, be_ref[i] != be_ref[jnp.maximum(i - 1, 0)])))
    def _():
        w1_scr[...] = w1_ref[0, 0].astype(BF16)
        w2_scr[...] = w2_ref[0, 0].astype(BF16)

    @pl.when(used)
    def _():
        z = jnp.dot(xs_ref[...].astype(BF16), w1_scr[...], preferred_element_type=F32) + b1_ref[0, 0]
        gate = jnp.minimum(z[:, :D_EXPERT], SWIGLU_LIMIT)
        up = jnp.clip(z[:, D_EXPERT:], -SWIGLU_LIMIT, SWIGLU_LIMIT)
        act = (up + 1.0) * gate * jax.nn.sigmoid(SWIGLU_ALPHA * gate)
        ys_ref[...] = jnp.dot(act.astype(BF16), w2_scr[...], preferred_element_type=F32) + b2_ref[0, 0]

    @pl.when(jnp.logical_not(used))
    def _():
        ys_ref[...] = jnp.zeros_like(ys_ref)


def _experts(blk_expert, n_used, xs, l, w_exp_in, b_exp_in, w_exp_out, b_exp_out):
    p_rows = xs.shape[0]
    nb = p_rows // MB
    depth = w_exp_in.shape[0]
    blk = lambda i, nu: jnp.maximum(jnp.minimum(i, nu[0] - 1), 0)
    wmap = lambda i, be, nu: (l, be[blk(i, nu)], 0, 0)
    xmap = lambda i, be, nu: (blk(i, nu), 0)
    grid_spec = pltpu.PrefetchScalarGridSpec(
        num_scalar_prefetch=2,
        grid=(nb,),
        in_specs=[pl.BlockSpec((MB, D_MODEL), xmap),
                  pl.BlockSpec((1, 1, D_MODEL, 2 * D_EXPERT), wmap),
                  pl.BlockSpec((1, 1, 1, 2 * D_EXPERT), wmap),
                  pl.BlockSpec((1, 1, D_EXPERT, D_MODEL), wmap),
                  pl.BlockSpec((1, 1, 1, D_MODEL), wmap)],
        out_specs=pl.BlockSpec((MB, D_MODEL), lambda i, be, nu: (i, 0)),
        scratch_shapes=[pltpu.VMEM((D_MODEL, 2 * D_EXPERT), BF16), pltpu.VMEM((D_EXPERT, D_MODEL), BF16)],
    )
    return pl.pallas_call(
        _expert_kernel,
        grid_spec=grid_spec,
        out_shape=jax.ShapeDtypeStruct((p_rows, D_MODEL), F32),
        compiler_params=_cparams("arbitrary"),
        name="experts",
    )(blk_expert, n_used, xs, w_exp_in, b_exp_in.reshape(depth, N_EXPERTS, 1, -1),
      w_exp_out, b_exp_out.reshape(depth, N_EXPERTS, 1, -1))


def _combine_kernel(alpha, n_p_tiles, a_ref, off_ref, seg_ref, tot_ref,
                    q_ref, gate_ref, x1_ref, mod_ref, g2_ref, b2_ref, ys_hbm, o_p_ref, o_s_ref, buf, sem):
    i = pl.program_id(0)
    slot = i % 2

    def fetch(j, s):
        def per_expert(e, carry):
            n = a_ref[j * N_EXPERTS + e]
            src = seg_ref[j * N_EXPERTS + e]
            dst = off_ref[j * N_EXPERTS + e]
            _segment_copies(n, lambda pos, size: pltpu.make_async_copy(
                ys_hbm.at[pl.ds(pl.multiple_of(src + pos, SEG), size)],
                buf.at[s, pl.ds(pl.multiple_of(dst + pos, SEG), size)], sem.at[s]))
            return carry

        lax.fori_loop(0, N_EXPERTS, per_expert, 0)

    @pl.when(i == 0)
    def _():
        buf[...] = jnp.zeros_like(buf)
        fetch(0, 0)

    @pl.when(i + 1 < pl.num_programs(0))
    def _():
        fetch(i + 1, 1 - slot)

    n = pl.multiple_of(tot_ref[i], SEG)

    @pl.when(n > 0)
    def _():
        pltpu.make_async_copy(ys_hbm.at[pl.ds(0, n)], buf.at[slot, pl.ds(0, n)], sem.at[slot]).wait()

    lane = lax.broadcasted_iota(jnp.int32, (TM, R_TILE), 1)
    q = q_ref[...]
    gates = gate_ref[...]
    g = jnp.zeros((TM, R_TILE), F32)
    for k in range(TOP_K):
        g = jnp.where(lane == q[:, k:k + 1], gates[:, k:k + 1], g)
    ffn = jnp.dot(g.astype(BF16), buf[slot].astype(BF16), preferred_element_type=F32)
    g2 = mod_ref[0][:, 5 * D_MODEL:6 * D_MODEL]
    res = _ln_plain(alpha * x1_ref[...] + g2 * ffn) * g2_ref[...] + b2_ref[...]
    @pl.when(i < n_p_tiles)
    def _():
        o_p_ref[...] = res

    @pl.when(i >= n_p_tiles)
    def _():
        o_s_ref[...] = res


def _combine(tables, q, gates, x1, mod, ys, wl, n_p_tiles, tiles_per_s_seq, ctx_row, alpha):
    t = x1.shape[0]
    nt = t // TM

    def mod_idx(i, *_):
        return (jnp.where(i < n_p_tiles, ctx_row, (i - n_p_tiles) // tiles_per_s_seq), 0, 0)

    out_specs = _x_specs(n_p_tiles)
    out_shape = [jax.ShapeDtypeStruct((n_p_tiles * TM, D_MODEL), F32),
                 jax.ShapeDtypeStruct((t - n_p_tiles * TM, D_MODEL), F32)]
    grid_spec = pltpu.PrefetchScalarGridSpec(
        num_scalar_prefetch=4,
        grid=(nt,),
        in_specs=[pl.BlockSpec((TM, TOP_K), lambda i, *_: (i, 0)),
                  pl.BlockSpec((TM, LANES), lambda i, *_: (i, 0)),
                  pl.BlockSpec((TM, D_MODEL), lambda i, *_: (i, 0)),
                  pl.BlockSpec((1, 1, 6 * D_MODEL), mod_idx),
                  pl.BlockSpec((1, D_MODEL), lambda i, *_: (0, 0)),
                  pl.BlockSpec((1, D_MODEL), lambda i, *_: (0, 0)),
                  pl.BlockSpec(memory_space=pl.ANY)],
        out_specs=out_specs,
        scratch_shapes=[pltpu.VMEM((2, R_TILE, D_MODEL), F32), pltpu.SemaphoreType.DMA((2,))],
    )
    return pl.pallas_call(
        functools.partial(_combine_kernel, alpha, n_p_tiles),
        grid_spec=grid_spec,
        out_shape=out_shape,
        compiler_params=_cparams("arbitrary"),
        name="combine",
    )(tables["a"], tables["off"], tables["seg"], tables["tot"], q, gates, x1, mod, wl["g_ln2"], wl["b_ln2"], ys)


def _routing_tables(counts, idx, lrank, n_blocks):
    cnt = counts[:, 0, :N_EXPERTS].astype(jnp.int32)
    a = (cnt + SEG - 1) // SEG * SEG
    off = jnp.cumsum(a, axis=1) - a
    acc = jnp.cumsum(a, axis=0) - a
    tot_e = jnp.sum(a, axis=0)
    padded = (tot_e + MB - 1) // MB * MB
    pend = jnp.cumsum(padded)
    pstart = pend - padded
    seg = pstart[None, :] + acc
    blk_row0 = jnp.arange(n_blocks, dtype=jnp.int32) * MB
    blk_expert = jnp.minimum(jnp.sum((pend[None, :] <= blk_row0[:, None]).astype(jnp.int32), axis=1), N_EXPERTS - 1)
    n_used = (pend[-1:] // MB).astype(jnp.int32)
    off_tok = jnp.repeat(off, TM, axis=0)
    hot = idx[:, :TOP_K, None] == jnp.arange(N_EXPERTS, dtype=jnp.int32)
    q = jnp.sum(jnp.where(hot, off_tok[:, None, :], 0), axis=-1) + lrank[:, :TOP_K]
    tables = {"a": a.reshape(-1), "off": off.reshape(-1), "seg": seg.reshape(-1), "tot": jnp.sum(a, axis=1),
              "tail0": pstart + tot_e, "tailn": padded - tot_e}
    return tables, q, blk_expert, n_used


def _pad_cols(w, width):
    return jnp.pad(w, ((0, 0), (0, width - w.shape[1])))


def _hi_lo(w):
    hi = w.astype(BF16)
    return jnp.concatenate([hi, (w - hi.astype(F32)).astype(BF16)], axis=1)


def _rot_cols(w):
    half = w.shape[1] // 2
    return jnp.concatenate([-w[:, half:], w[:, :half]], axis=1)


def _layer_weights(l, w_in, g_q_norm, w_uq, g_kv_norm, w_uk, w_uv, w_sconv, w_cconv, b_cconv,
                   g_cf_norm, b_cf_norm, w_out, g_ln1, b_ln1, g_ln2, b_ln2, w_router, b_router):
    wi = w_in[l]
    o_ckv, o_kr, o_bg = Q_RANK, Q_RANK + KV_RANK, Q_RANK + KV_RANK + QK_ROPE
    w_kr = wi[:, o_kr:o_bg]
    w_in_p = jnp.concatenate([
        _pad_cols(wi[:, :Q_RANK], COL_CKV - COL_CQ),
        wi[:, o_ckv:o_kr],
        _pad_cols(jnp.concatenate([w_kr, _rot_cols(w_kr)], axis=1), COL_BG - COL_KR),
        wi[:, o_bg:]], axis=1).astype(BF16)

    uq = jnp.pad(w_uq[l], ((0, Q_RANK_PAD - Q_RANK), (0, 0))).reshape(Q_RANK_PAD, HEADS, QK_DIM)
    zer = jnp.zeros((Q_RANK_PAD, HEADS, HEAD_PAD - QK_DIM), F32)
    wq_a = jnp.concatenate([uq, zer], axis=2).reshape(Q_RANK_PAD, HEADS * HEAD_PAD).astype(BF16)
    rot = jnp.concatenate([-uq[:, :, QK_NOPE + QK_ROPE // 2:], uq[:, :, QK_NOPE:QK_NOPE + QK_ROPE // 2]], axis=2)
    wq_b = jnp.concatenate([jnp.zeros((Q_RANK_PAD, HEADS, QK_NOPE), F32), rot, zer],
                           axis=2).reshape(Q_RANK_PAD, HEADS * HEAD_PAD).astype(BF16)

    uk_t = w_uk[l].reshape(KV_RANK, HEADS, QK_NOPE).transpose(1, 2, 0)
    top = jnp.pad(uk_t, ((0, 0), (0, 0), (0, KFEAT - KV_RANK)))
    eye = jnp.pad(jnp.eye(QK_ROPE, dtype=F32), ((0, 0), (KV_RANK, KFEAT - KV_RANK - QK_ROPE)))
    mid = jnp.broadcast_to(eye[None], (HEADS, QK_ROPE, KFEAT))
    wk_t = jnp.concatenate([top, mid, jnp.zeros((HEADS, HEAD_PAD - QK_DIM, KFEAT), F32)], axis=1).astype(BF16)

    uv = w_uv[l].reshape(KV_RANK, HEADS, V_DIM)
    zv = jnp.zeros((KV_RANK, HEADS, V_DIM), F32)
    odd = (jnp.arange(HEADS) % 2 == 1)[None, :, None]
    wv = jnp.concatenate([jnp.where(odd, zv, uv), jnp.where(odd, uv, zv)], axis=2).reshape(KV_RANK, HEADS * LANES)
    wv = jnp.pad(wv, ((0, KFEAT - KV_RANK), (0, 0)))
    ones_col = jnp.arange(HEADS) * LANES + jnp.where(jnp.arange(HEADS) % 2 == 1, 0, V_DIM)
    wv = wv.at[ONES_FEAT, ones_col].set(1.0).astype(BF16)

    r1 = lambda v: v.reshape(1, -1)
    return {
        "w_in": w_in_p,
        "g_q": jnp.pad(g_q_norm[l], (0, Q_RANK_PAD - Q_RANK)).reshape(1, -1),
        "g_kv": r1(g_kv_norm[l]),
        "wq_a": wq_a, "wq_b": wq_b, "wk_t": wk_t, "wv": wv,
        "w_sconv": w_sconv[l], "w_cconv": w_cconv[l], "b_cconv": r1(b_cconv[l]),
        "g_cf": r1(g_cf_norm[l]), "b_cf": r1(b_cf_norm[l]),
        "w_out": w_out[l].astype(BF16),
        "g_ln1": r1(g_ln1[l]), "b_ln1": r1(b_ln1[l]), "g_ln2": r1(g_ln2[l]), "b_ln2": r1(b_ln2[l]),
        "w_router": _hi_lo(_pad_cols(w_router[l], LANES)),
        "b_router": jnp.pad(b_router[l], (0, LANES - N_EXPERTS)).reshape(1, -1),
    }


def _rope_tables(n_lat):
    rows = n_lat // GRID_W
    row = jnp.repeat(jnp.arange(rows, dtype=F32), GRID_W)
    col = jnp.tile(jnp.arange(GRID_W, dtype=F32), rows)
    axis_dim = QK_ROPE // 2
    inv = ROPE_BASE ** (-jnp.arange(0, axis_dim, 2, dtype=F32) / axis_dim)
    ang = jnp.concatenate([row[:, None] * inv, col[:, None] * inv], axis=-1)
    cos, sin = jnp.cos(ang), jnp.sin(ang)
    one = jnp.ones((n_lat, QK_NOPE), F32)
    zq = jnp.zeros((n_lat, HEAD_PAD - QK_DIM), F32)
    zk = jnp.zeros((n_lat, LANES - QK_ROPE), F32)
    return (jnp.concatenate([one, cos, cos, zq], axis=1),
            jnp.concatenate([0.0 * one, sin, sin, zq], axis=1),
            jnp.concatenate([cos, cos, zk], axis=1),
            jnp.concatenate([sin, sin, zk], axis=1))


def kernel(x_prompt, x_sample, cache_ckv, cache_krope, c, c_ctx, w_mod, b_mod, w_in, g_q_norm, w_uq, g_kv_norm, w_uk, w_uv, w_sconv, w_cconv, b_cconv, g_cf_norm, b_cf_norm, w_out, g_ln1, b_ln1, g_ln2, b_ln2, w_router, b_router, w_exp_in, b_exp_in, w_exp_out, b_exp_out):
    bp, n_p, d = x_prompt.shape
    bs, n_s, _ = x_sample.shape
    depth = w_mod.shape[0]
    past = cache_ckv.shape[2]
    assert d == D_MODEL and n_p == TM and n_s % TM == 0 and n_s % GRID_W == 0
    t_p, t_s = bp * n_p, bs * n_s
    assert t_p % n_s == 0 and past % LANES == 0
    t = t_p + t_s
    n_p_tiles = t_p // TM
    tps = n_s // TM
    alpha = float((2 * depth) ** 0.25)

    ctx_row = bs
    rows = -(-(bs + 1) // SUBLANES) * SUBLANES
    c_rows = jnp.concatenate([c, c_ctx[None], jnp.zeros((rows - bs - 1, d), F32)], axis=0)
    mod_all = _modulation(c_rows, w_mod, b_mod)

    tabs = _rope_tables(n_s)
    x_p, x_s = x_prompt.reshape(t_p, d), x_sample.reshape(t_s, d)
    nt = t // TM
    n_blocks = -(-(t * TOP_K + nt * N_EXPERTS * (SEG - 1)) // MB) + N_EXPERTS
    p_rows = n_blocks * MB

    new_ckv, new_krope = [], []
    for l in range(depth):
        wl = _layer_weights(l, w_in, g_q_norm, w_uq, g_kv_norm, w_uk, w_uv, w_sconv, w_cconv, b_cconv,
                            g_cf_norm, b_cf_norm, w_out, g_ln1, b_ln1, g_ln2, b_ln2, w_router, b_router)
        mod = mod_all[l].reshape(rows, 1, 6 * d)
        q, ckv, krope, kfeat, conv_in = _in_proj(x_p, x_s, mod, wl, tabs, n_p_tiles, tps, ctx_row)
        new_ckv.append(ckv[:t_p].reshape(bp, n_p, KV_RANK))
        new_krope.append(krope[:t_p].reshape(bp, n_p, QK_ROPE))

        kcache = jnp.concatenate(
            [cache_ckv[:, l], cache_krope[:, l], jnp.ones((bs, past, 1), F32),
             jnp.zeros((bs, past, KFEAT - ONES_FEAT - 1), F32)], axis=-1).astype(BF16)
        attn_p = _attention(q, kfeat, None, wl, bp, n_p, 0)
        attn_s = _attention(q, kfeat, kcache, wl, bs, n_s, t_p)
        x1, h2, idx, lrank, gates, counts = _mix(x_p, x_s, attn_p, attn_s, conv_in, mod, wl, n_p_tiles, tps,
                                                 ctx_row, alpha)
        tables, q, blk_expert, n_used = _routing_tables(counts, idx, lrank, n_blocks)
        xs = _dispatch(tables, n_used, h2, q.T, p_rows)
        ys = _experts(blk_expert, n_used, xs, l, w_exp_in, b_exp_in, w_exp_out, b_exp_out)
        x_p, x_s = _combine(tables, q, gates, x1, mod, ys, wl, n_p_tiles, tps, ctx_row, alpha)

    return (x_p.reshape(bp, n_p, d), x_s.reshape(bs, n_s, d),
            jnp.stack(new_ckv, axis=1), jnp.stack(new_krope, axis=1))
```

```python
import functools
import math

import jax
import jax.numpy as jnp
from jax import lax
from jax.experimental import pallas as pl
from jax.experimental.pallas import tpu as pltpu

F32 = jnp.float32
BF16 = jnp.bfloat16

D_MODEL = 1024
GRID_W = 64
HEADS = 8
QK_NOPE = 64
QK_ROPE = 32
QK_DIM = QK_NOPE + QK_ROPE
V_DIM = 64
Q_RANK = 192
KV_RANK = 128
SC_DIM = 256
SC_WIDTH = 3
CF_DIM = 256
CF_WIDTH = 31
N_EXPERTS = 32
TOP_K = 4
D_EXPERT = 1024
SWIGLU_LIMIT = 7.0
SWIGLU_ALPHA = 1.702
ROPE_BASE = 10000.0
EPS = 1e-6

LANES = 128
SUBLANES = 8

TM = 256
TQ = 512
MB = 512
HALO = 16
SEG = SUBLANES
R_TILE = TM * TOP_K + N_EXPERTS * SEG
MXU_ROWSUM_MIN_KEYS = 2048
DISPATCH_SLOTS = 2
HEAD_PAD = LANES
Q_RANK_PAD = 256
KFEAT = 256
ONES_FEAT = KV_RANK + QK_ROPE
COL_CQ, COL_CKV, COL_KR, COL_BG, COL_CG, COL_XS, COL_GA, COL_GB, IN_COLS_PAD = (
    0, 256, 384, 512, 768, 1024, 1280, 1536, 1792)
VMEM_LIMIT = 60 * 1024 * 1024


def _cparams(*sem):
    return pltpu.CompilerParams(dimension_semantics=sem, vmem_limit_bytes=VMEM_LIMIT)


def _ln_plain(x):
    mu = jnp.mean(x, axis=-1, keepdims=True)
    xc = x - mu
    var = jnp.mean(xc * xc, axis=-1, keepdims=True)
    return xc * lax.rsqrt(var + EPS)


def _mod_kernel(c_ref, w_ref, b_ref, o_ref):
    c = c_ref[...]
    s = c * jax.nn.sigmoid(c)
    o_ref[0] = jnp.dot(s, w_ref[0], preferred_element_type=F32,
                       precision=lax.Precision.HIGHEST) + b_ref[0]


def _modulation(c_rows, w_mod, b_mod):
    depth = w_mod.shape[0]
    rows = c_rows.shape[0]
    nblk = w_mod.shape[2] // D_MODEL
    return pl.pallas_call(
        _mod_kernel,
        grid=(depth, nblk),
        in_specs=[pl.BlockSpec((rows, D_MODEL), lambda l, j: (0, 0)),
                  pl.BlockSpec((1, D_MODEL, D_MODEL), lambda l, j: (l, 0, j)),
                  pl.BlockSpec((1, 1, D_MODEL), lambda l, j: (l, 0, j))],
        out_specs=pl.BlockSpec((1, rows, D_MODEL), lambda l, j: (l, 0, j)),
        out_shape=jax.ShapeDtypeStruct((depth, rows, w_mod.shape[2]), F32),
        compiler_params=_cparams("arbitrary", "arbitrary"),
        name="modulation",
    )(c_rows, w_mod, b_mod.reshape(depth, 1, -1))


def _in_proj_kernel(n_p_tiles, xp_ref, xs_ref, mod_ref, win_ref, gq_ref, gkv_ref, wqa_ref, wqb_ref,
                    tcq_ref, tsq_ref, tck_ref, tsk_ref,
                    q_ref, ckv_ref, kr_ref, kf_ref, cv_ref):
    is_s = pl.program_id(0) >= n_p_tiles
    mod = mod_ref[0]
    x = jnp.where(is_s, xs_ref[...], xp_ref[...])
    h = _ln_plain(x) * (1.0 + mod[:, D_MODEL:2 * D_MODEL]) + mod[:, 0:D_MODEL]
    z = jnp.dot(h.astype(BF16), win_ref[...], preferred_element_type=F32)

    cq = z[:, COL_CQ:COL_CQ + Q_RANK_PAD]
    cqn = cq * lax.rsqrt(jnp.sum(cq * cq, axis=-1, keepdims=True) * (1.0 / Q_RANK) + EPS) * gq_ref[...]
    cqn = cqn.astype(BF16)
    qa = jnp.dot(cqn, wqa_ref[...], preferred_element_type=F32)
    qb = jnp.dot(cqn, wqb_ref[...], preferred_element_type=F32)
    cos_q = jnp.where(is_s, tcq_ref[...], 1.0)
    sin_q = jnp.where(is_s, tsq_ref[...], 0.0)
    scale = math.log2(math.e) / math.sqrt(QK_DIM)
    for hd in range(HEADS):
        sl = slice(hd * HEAD_PAD, (hd + 1) * HEAD_PAD)
        q_ref[hd] = ((qa[:, sl] * cos_q + qb[:, sl] * sin_q) * scale).astype(BF16)

    cr = z[:, COL_CKV:COL_CKV + KV_RANK]
    ckv = cr * lax.rsqrt(jnp.mean(cr * cr, axis=-1, keepdims=True) + EPS) * gkv_ref[...]
    ckv_ref[...] = ckv
    zk = z[:, COL_KR:COL_KR + LANES]
    kr_ref[...] = zk[:, 0:QK_ROPE]
    lane = lax.broadcasted_iota(jnp.int32, zk.shape, 1)
    raw = jnp.where(lane < QK_ROPE, zk, 0.0)
    roped = zk * tck_ref[...] + pltpu.roll(zk, LANES - QK_ROPE, 1) * tsk_ref[...]
    kf_ref[:, 0:KV_RANK] = ckv.astype(BF16)
    kf_ref[:, KV_RANK:KFEAT] = jnp.where(lane == ONES_FEAT - KV_RANK, 1.0, jnp.where(is_s, roped, raw)).astype(BF16)

    cv_ref[:, 0:SC_DIM] = z[:, COL_BG:COL_BG + SC_DIM]
    cv_ref[:, SC_DIM:2 * SC_DIM] = z[:, COL_CG:COL_CG + SC_DIM] * z[:, COL_XS:COL_XS + SC_DIM]
    cv_ref[:, 2 * SC_DIM:] = z[:, COL_GA:COL_GA + CF_DIM] * jax.nn.sigmoid(z[:, COL_GB:COL_GB + CF_DIM])


def _x_specs(n_p_tiles):
    return [pl.BlockSpec((TM, D_MODEL), lambda i, *_: (jnp.minimum(i, n_p_tiles - 1), 0)),
            pl.BlockSpec((TM, D_MODEL), lambda i, *_: (jnp.maximum(i - n_p_tiles, 0), 0))]


def _in_proj(x_p, x_s, mod, wl, tabs, n_p_tiles, tiles_per_s_seq, ctx_row):
    t = x_p.shape[0] + x_s.shape[0]
    nt = t // TM

    def mod_idx(i):
        return (jnp.where(i < n_p_tiles, ctx_row, (i - n_p_tiles) // tiles_per_s_seq), 0, 0)

    def pos_idx(i):
        return (jnp.where(i < n_p_tiles, 0, (i - n_p_tiles) % tiles_per_s_seq), 0)

    full = lambda shp: pl.BlockSpec(shp, lambda i: (0,) * len(shp))
    tab = pl.BlockSpec((TM, LANES), pos_idx)
    return pl.pallas_call(
        functools.partial(_in_proj_kernel, n_p_tiles),
        grid=(nt,),
        in_specs=_x_specs(n_p_tiles) + [
                  pl.BlockSpec((1, 1, 6 * D_MODEL), mod_idx),
                  full((D_MODEL, IN_COLS_PAD)),
                  full((1, Q_RANK_PAD)), full((1, KV_RANK)),
                  full((Q_RANK_PAD, HEADS * HEAD_PAD)), full((Q_RANK_PAD, HEADS * HEAD_PAD)),
                  tab, tab, tab, tab],
        out_specs=[pl.BlockSpec((HEADS, TM, HEAD_PAD), lambda i: (0, i, 0)),
                   pl.BlockSpec((TM, KV_RANK), lambda i: (i, 0)),
                   pl.BlockSpec((TM, QK_ROPE), lambda i: (i, 0)),
                   pl.BlockSpec((TM, KFEAT), lambda i: (i, 0)),
                   pl.BlockSpec((TM, 3 * SC_DIM), lambda i: (i, 0))],
        out_shape=[jax.ShapeDtypeStruct((HEADS, t, HEAD_PAD), BF16),
                   jax.ShapeDtypeStruct((t, KV_RANK), F32),
                   jax.ShapeDtypeStruct((t, QK_ROPE), F32),
                   jax.ShapeDtypeStruct((t, KFEAT), BF16),
                   jax.ShapeDtypeStruct((t, 3 * SC_DIM), F32)],
        compiler_params=_cparams("arbitrary"),
        name="in_proj",
    )(x_p, x_s, mod, wl["w_in"], wl["g_q"], wl["g_kv"], wl["wq_a"], wl["wq_b"], *tabs)


def _attn_kernel(n_own, n_cache, q_ref, kf_ref, *rest):
    if n_cache:
        kc_ref, wkt_ref, wv_ref, o_ref, kt_scr, v_scr = rest
    else:
        wkt_ref, wv_ref, o_ref, kt_scr, v_scr = rest

    @pl.when(pl.program_id(1) == 0)
    def _():
        def fill(feat, lo, n):
            for hd in range(HEADS):
                kt = lax.dot_general(wkt_ref[hd], feat, (((1,), (1,)), ((), ())),
                                     preferred_element_type=F32)
                kt_scr[hd, :, lo:lo + n] = kt.astype(BF16)
                v = jnp.dot(feat, wv_ref[:, hd * LANES:(hd + 1) * LANES], preferred_element_type=F32)
                v_scr[hd, lo:lo + n, :] = v.astype(BF16)

        fill(kf_ref[...], 0, n_own)
        if n_cache:
            fill(kc_ref[0], n_own, n_cache)

    mxu_rowsum = n_own + n_cache >= MXU_ROWSUM_MIN_KEYS
    lane = lax.broadcasted_iota(jnp.int32, o_ref.shape[1:], 1)
    for pair in range(HEADS // 2):
        outs, sums = [], []
        for hd in (2 * pair, 2 * pair + 1):
            s = jnp.dot(q_ref[hd], kt_scr[hd], preferred_element_type=F32)
            p = jnp.exp2(s - jnp.max(s, axis=-1, keepdims=True))
            o = jnp.dot(p.astype(BF16), v_scr[hd], preferred_element_type=F32)
            outs.append(o)
            ones_lane = V_DIM if hd % 2 == 0 else 0
            sums.append(o[:, ones_lane:ones_lane + 1] if mxu_rowsum else jnp.sum(p, axis=-1, keepdims=True))
        o_ref[pair] = jnp.where(lane < V_DIM, outs[0] / sums[0], outs[1] / sums[1]).astype(BF16)


def _attention(q, kfeat, kcache, wl, batch, n_own, row0):
    n_cache = 0 if kcache is None else kcache.shape[1]
    m = n_own + n_cache
    tq = min(TQ, n_own)
    nq = n_own // tq
    q0 = row0 // tq
    k0 = row0 // n_own
    in_specs = [pl.BlockSpec((HEADS, tq, HEAD_PAD), lambda b, j: (0, q0 + b * nq + j, 0)),
                pl.BlockSpec((n_own, KFEAT), lambda b, j: (k0 + b, 0))]
    args = [q, kfeat]
    if n_cache:
        in_specs.append(pl.BlockSpec((1, n_cache, KFEAT), lambda b, j: (b, 0, 0)))
        args.append(kcache)
    in_specs += [pl.BlockSpec((HEADS, HEAD_PAD, KFEAT), lambda b, j: (0, 0, 0)),
                 pl.BlockSpec((KFEAT, HEADS * LANES), lambda b, j: (0, 0))]
    args += [wl["wk_t"], wl["wv"]]
    return pl.pallas_call(
        functools.partial(_attn_kernel, n_own, n_cache),
        grid=(batch, nq),
        in_specs=in_specs,
        out_specs=pl.BlockSpec((HEADS // 2, tq, LANES), lambda b, j: (0, b * nq + j, 0)),
        out_shape=jax.ShapeDtypeStruct((HEADS // 2, batch * n_own, LANES), BF16),
        scratch_shapes=[pltpu.VMEM((HEADS, HEAD_PAD, m), BF16), pltpu.VMEM((HEADS, m, LANES), BF16)],
        compiler_params=_cparams("arbitrary", "arbitrary"),
        name="attention_cache" if n_cache else "attention_ctx",
    )(*args)


def _mix_kernel(n_p_tiles, tiles_per_s_seq, alpha,
                xp_ref, xs_ref, atp_ref, ats_ref, cprev_ref, ccur_ref, cnext_ref, mod_ref,
                wsc_ref, wcc_ref, bcc_ref, gcf_ref, bcf_ref, wout_ref, g1_ref, b1_ref,
                wr_ref, br_ref,
                x1_ref, h2_ref, idx_ref, rank_ref, gate_ref, cnt_ref,
                ext_scr, shf_scr):
    i = pl.program_id(0)
    is_s = i >= n_p_tiles
    jj = (i - n_p_tiles) % tiles_per_s_seq
    has_prev = jnp.logical_and(is_s, jj > 0)
    has_next = jnp.logical_and(is_s, jj < tiles_per_s_seq - 1)

    ext_scr[0:HALO, :] = jnp.where(has_prev, cprev_ref[:, SC_DIM:], 0.0)
    ext_scr[HALO:HALO + TM, :] = ccur_ref[:, SC_DIM:]
    ext_scr[HALO + TM:, :] = jnp.where(has_next, cnext_ref[:, SC_DIM:], 0.0)

    sconv = jnp.zeros((TM, SC_DIM), F32)
    for k in range(SC_WIDTH):
        o = HALO - SC_WIDTH // 2 + k
        sconv = sconv + ext_scr[o:o + TM, 0:SC_DIM] * wsc_ref[k:k + 1, :]
    sc_out = ccur_ref[:, 0:SC_DIM] * sconv

    n_sh = TM + 2 * HALO - SUBLANES
    for s in range(1, SUBLANES):
        shf_scr[s - 1] = ext_scr[s:s + n_sh, SC_DIM:]
    cconv = jnp.zeros((TM, CF_DIM), F32)
    for k in range(CF_WIDTH):
        o = HALO - CF_WIDTH // 2 + k
        s, base = o % SUBLANES, o - o % SUBLANES
        rows = ext_scr[base:base + TM, SC_DIM:] if s == 0 else shf_scr[s - 1, base:base + TM, :]
        cconv = cconv + rows * wcc_ref[k:k + 1, :]
    cf = _ln_plain(cconv + bcc_ref[...]) * gcf_ref[...] + bcf_ref[...]
    cf = cf * jax.nn.sigmoid(cf)

    at = [jnp.where(is_s, ats_ref[p], atp_ref[p]) for p in range(HEADS // 2)]
    mix_in = jnp.concatenate(at + [sc_out.astype(BF16), cf.astype(BF16)], axis=1)
    mix = jnp.dot(mix_in, wout_ref[...], preferred_element_type=F32)

    mod = mod_ref[0]
    g1 = mod[:, 2 * D_MODEL:3 * D_MODEL]
    sh2 = mod[:, 3 * D_MODEL:4 * D_MODEL]
    sc2 = mod[:, 4 * D_MODEL:5 * D_MODEL]
    x = jnp.where(is_s, xs_ref[...], xp_ref[...])
    x1 = _ln_plain(alpha * x + g1 * mix) * g1_ref[...] + b1_ref[...]
    x1_ref[...] = x1
    h2 = _ln_plain(x1) * (1.0 + sc2) + sh2
    h2_hi = h2.astype(BF16)
    h2_ref[...] = h2_hi
    h2_lo = (h2 - h2_hi.astype(F32)).astype(BF16)
    hi_both = jnp.dot(h2_hi, wr_ref[...], preferred_element_type=F32)
    lo_hi = jnp.dot(h2_lo, wr_ref[:, 0:LANES], preferred_element_type=F32)
    logits = hi_both[:, 0:LANES] + hi_both[:, LANES:] + lo_hi + br_ref[...]
    lane = lax.broadcasted_iota(jnp.int32, (TM, LANES), 1)
    lane_f = lane.astype(F32)
    neg = jnp.float32(-jnp.inf)
    cur = jnp.where(lane < N_EXPERTS, logits, neg)
    vals, hots, idxs = [], [], []
    for _ in range(TOP_K):
        mx = jnp.max(cur, axis=-1, keepdims=True)
        ik = jnp.min(jnp.where(cur == mx, lane_f, float(LANES)), axis=-1, keepdims=True)
        ik = jnp.minimum(ik, float(N_EXPERTS - 1))
        hot = lane_f == ik
        cur = jnp.where(hot, neg, cur)
        vals.append(mx)
        hots.append(hot)
        idxs.append(ik.astype(jnp.int32))
    exps = [jnp.exp(v - vals[0]) for v in vals]
    denom = exps[0] + exps[1] + exps[2] + exps[3]
    sel = jnp.where(hots[0] | hots[1] | hots[2] | hots[3], 1.0, 0.0)

    r_i = lax.broadcasted_iota(jnp.int32, (TM, TM), 0)
    c_i = lax.broadcasted_iota(jnp.int32, (TM, TM), 1)
    tri = jnp.where(c_i < r_i, 1.0, 0.0).astype(BF16)
    rank_all = jnp.dot(tri, sel.astype(BF16), preferred_element_type=F32)
    cnt_ref[0] = jnp.broadcast_to(jnp.sum(sel, axis=0, keepdims=True), (SUBLANES, LANES))

    idx_o = jnp.zeros((TM, LANES), jnp.int32)
    rank_o = jnp.zeros((TM, LANES), jnp.int32)
    gate_o = jnp.zeros((TM, LANES), F32)
    for k in range(TOP_K):
        rk = jnp.sum(jnp.where(hots[k], rank_all, 0.0), axis=-1, keepdims=True).astype(jnp.int32)
        idx_o = jnp.where(lane == k, idxs[k], idx_o)
        rank_o = jnp.where(lane == k, rk, rank_o)
        gate_o = jnp.where(lane == k, exps[k] / denom, gate_o)
    idx_ref[...] = idx_o
    rank_ref[...] = rank_o
    gate_ref[...] = gate_o


def _mix(x_p, x_s, attn_p, attn_s, conv_in, mod, wl, n_p_tiles, tiles_per_s_seq, ctx_row, alpha):
    t = x_p.shape[0] + x_s.shape[0]
    nt = t // TM
    hpt = TM // HALO
    n_halo = t // HALO

    def mod_idx(i):
        return (jnp.where(i < n_p_tiles, ctx_row, (i - n_p_tiles) // tiles_per_s_seq), 0, 0)

    full = lambda shp: pl.BlockSpec(shp, lambda i: (0,) * len(shp))
    row = lambda w: pl.BlockSpec((TM, w), lambda i: (i, 0))
    return pl.pallas_call(
        functools.partial(_mix_kernel, n_p_tiles, tiles_per_s_seq, alpha),
        grid=(nt,),
        in_specs=_x_specs(n_p_tiles) + [
                  pl.BlockSpec((HEADS // 2, TM, LANES), lambda i: (0, jnp.minimum(i, n_p_tiles - 1), 0)),
                  pl.BlockSpec((HEADS // 2, TM, LANES), lambda i: (0, jnp.maximum(i - n_p_tiles, 0), 0)),
                  pl.BlockSpec((HALO, 3 * SC_DIM), lambda i: (jnp.maximum(i * hpt - 1, 0), 0)),
                  row(3 * SC_DIM),
                  pl.BlockSpec((HALO, 3 * SC_DIM), lambda i: (jnp.minimum((i + 1) * hpt, n_halo - 1), 0)),
                  pl.BlockSpec((1, 1, 6 * D_MODEL), mod_idx),
                  full((SC_WIDTH, SC_DIM)), full((CF_WIDTH, CF_DIM)), full((1, CF_DIM)),
                  full((1, CF_DIM)), full((1, CF_DIM)),
                  full((D_MODEL, D_MODEL)), full((1, D_MODEL)), full((1, D_MODEL)),
                  full((D_MODEL, 2 * LANES)), full((1, LANES))],
        out_specs=[row(D_MODEL), row(D_MODEL),
                   row(LANES), row(LANES), row(LANES),
                   pl.BlockSpec((1, SUBLANES, LANES), lambda i: (i, 0, 0))],
        out_shape=[jax.ShapeDtypeStruct((t, D_MODEL), F32),
                   jax.ShapeDtypeStruct((t, D_MODEL), BF16),
                   jax.ShapeDtypeStruct((t, LANES), jnp.int32),
                   jax.ShapeDtypeStruct((t, LANES), jnp.int32),
                   jax.ShapeDtypeStruct((t, LANES), F32),
                   jax.ShapeDtypeStruct((nt, SUBLANES, LANES), F32)],
        scratch_shapes=[pltpu.VMEM((TM + 2 * HALO, 2 * SC_DIM), F32),
                        pltpu.VMEM((SUBLANES - 1, TM + 2 * HALO - SUBLANES, CF_DIM), F32)],
        compiler_params=_cparams("arbitrary"),
        name="mix",
    )(x_p, x_s, attn_p, attn_s, conv_in, conv_in, conv_in, mod,
      wl["w_sconv"], wl["w_cconv"], wl["b_cconv"], wl["g_cf"], wl["b_cf"],
      wl["w_out"], wl["g_ln1"], wl["b_ln1"], wl["w_router"], wl["b_router"])


def _segment_copies(n, max_rows, make_copy, start=True):
    pos = jnp.int32(0)
    size = max_rows
    while size >= SEG:
        take = n & size

        @pl.when(take != 0)
        def _(pos=pos, size=size):
            cp = make_copy(pos, size)
            cp.start() if start else cp.wait()

        pos = pos + take
        size //= 2


def _dispatch_kernel(a_ref, off_ref, seg_ref, tot_ref, tail0_ref, tailn_ref, nu_ref,
                     h_ref, qt_ref, xs_hbm, buf, zbuf, sem, zsem):
    i = pl.program_id(0)
    last = pl.num_programs(0) - 1
    n_slots = buf.shape[0]
    slot = i % n_slots

    r = lax.broadcasted_iota(jnp.int32, (R_TILE, TM), 0)
    hit = r == qt_ref[0:1, :]
    for k in range(1, TOP_K):
        hit = jnp.logical_or(hit, r == qt_ref[k:k + 1, :])
    perm = jnp.where(hit, 1.0, 0.0).astype(BF16)
    buf[slot] = jnp.dot(perm, h_ref[...], preferred_element_type=F32)

    def wait_tile(j, s):
        n = pl.multiple_of(tot_ref[j], SEG)

        @pl.when(n > 0)
        def _():
            pltpu.make_async_copy(buf.at[s, pl.ds(0, n)], xs_hbm.at[pl.ds(0, n)], sem.at[s]).wait()

    @pl.when(i >= n_slots - 1)
    def _():
        wait_tile(i + 1 - n_slots, (i + 1) % n_slots)

    def per_expert(e, carry):
        n = a_ref[i * N_EXPERTS + e]
        src = off_ref[i * N_EXPERTS + e]
        dst = seg_ref[i * N_EXPERTS + e]
        _segment_copies(n, TM, lambda pos, size: pltpu.make_async_copy(
            buf.at[slot, pl.ds(pl.multiple_of(src + pos, SEG), size)],
            xs_hbm.at[pl.ds(pl.multiple_of(dst + pos, SEG), size)], sem.at[slot]))
        return carry

    lax.fori_loop(0, N_EXPERTS, per_expert, 0)

    @pl.when(i == last)
    def _():
        zbuf[...] = jnp.zeros_like(zbuf)

        def tail(start):
            def per_tail(e, carry):
                dst = tail0_ref[e]
                _segment_copies(tailn_ref[e], MB // 2, lambda pos, size: pltpu.make_async_copy(
                    zbuf.at[pl.ds(0, size)], xs_hbm.at[pl.ds(pl.multiple_of(dst + pos, SEG), size)], zsem),
                    start=start)
                return carry
            lax.fori_loop(0, N_EXPERTS, per_tail, 0)

        def spare(b):
            return pltpu.make_async_copy(zbuf, xs_hbm.at[pl.ds(pl.multiple_of(b * MB, MB), MB)], zsem)

        def start_spare(b, carry):
            spare(b).start()
            return carry

        def wait_spare(b, carry):
            spare(b).wait()
            return carry

        n_blocks = xs_hbm.shape[0] // MB
        tail(True)
        lax.fori_loop(nu_ref[0], n_blocks, start_spare, 0)
        tail(False)
        lax.fori_loop(nu_ref[0], n_blocks, wait_spare, 0)
        for back in range(n_slots - 1):
            @pl.when(i >= back)
            def _(back=back):
                wait_tile(i - back, (i - back) % n_slots)


def _dispatch(tables, n_used, h2, q_t, p_rows):
    t = h2.shape[0]
    grid_spec = pltpu.PrefetchScalarGridSpec(
        num_scalar_prefetch=7,
        grid=(t // TM,),
        in_specs=[pl.BlockSpec((TM, D_MODEL), lambda i, *_: (i, 0)),
                  pl.BlockSpec((TOP_K, TM), lambda i, *_: (0, i))],
        out_specs=pl.BlockSpec(memory_space=pl.ANY),
        scratch_shapes=[pltpu.VMEM((DISPATCH_SLOTS, R_TILE, D_MODEL), F32), pltpu.VMEM((MB, D_MODEL), F32),
                        pltpu.SemaphoreType.DMA((DISPATCH_SLOTS,)), pltpu.SemaphoreType.DMA(())],
    )
    return pl.pallas_call(
        _dispatch_kernel,
        grid_spec=grid_spec,
        out_shape=jax.ShapeDtypeStruct((p_rows, D_MODEL), F32),
        compiler_params=_cparams("arbitrary"),
        name="dispatch",
    )(tables["a"], tables["off"], tables["seg"], tables["tot"], tables["tail0"], tables["tailn"], n_used, h2, q_t)


def _expert_kernel(be_ref, nu_ref, br_ref, xs_ref, w1_ref, b1_ref, w2_ref, b2_ref, ys_ref, w1_scr, w2_scr):
    i = pl.program_id(0)
    used = i < nu_ref[0]
    half = MB // 2
    short = br_ref[i] <= half

    @pl.when(jnp.logical_and(used, jnp.logical_or(i == 0, be_ref[i] != be_ref[jnp.maximum(i - 1, 0)])))
    def _():
        w1_scr[...] = w1_ref[0, 0].astype(BF16)
        w2_scr[...] = w2_ref[0, 0].astype(BF16)

    def ffn(rows):
        z = jnp.dot(xs_ref[0:rows, :].astype(BF16), w1_scr[...], preferred_element_type=F32) + b1_ref[0, 0]
        gate = jnp.minimum(z[:, :D_EXPERT], SWIGLU_LIMIT)
        up = jnp.clip(z[:, D_EXPERT:], -SWIGLU_LIMIT, SWIGLU_LIMIT)
        act = (up + 1.0) * gate * jax.nn.sigmoid(SWIGLU_ALPHA * gate)
        ys_ref[0:rows, :] = jnp.dot(act.astype(BF16), w2_scr[...], preferred_element_type=F32) + b2_ref[0, 0]

    @pl.when(jnp.logical_and(used, jnp.logical_not(short)))
    def _():
        ffn(MB)

    @pl.when(jnp.logical_and(used, short))
    def _():
        ffn(half)
        ys_ref[half:, :] = jnp.zeros((MB - half, D_MODEL), F32)

    @pl.when(jnp.logical_not(used))
    def _():
        ys_ref[...] = jnp.zeros_like(ys_ref)


def _experts(blk_expert, n_used, blk_rows, xs, l, w_exp_in, b_exp_in, w_exp_out, b_exp_out):
    p_rows = xs.shape[0]
    nb = p_rows // MB
    depth = w_exp_in.shape[0]
    blk = lambda i, nu: jnp.maximum(jnp.minimum(i, nu[0] - 1), 0)
    wmap = lambda i, be, nu, br: (l, be[blk(i, nu)], 0, 0)
    xmap = lambda i, be, nu, br: (blk(i, nu), 0)
    grid_spec = pltpu.PrefetchScalarGridSpec(
        num_scalar_prefetch=3,
        grid=(nb,),
        in_specs=[pl.BlockSpec((MB, D_MODEL), xmap),
                  pl.BlockSpec((1, 1, D_MODEL, 2 * D_EXPERT), wmap),
                  pl.BlockSpec((1, 1, 1, 2 * D_EXPERT), wmap),
                  pl.BlockSpec((1, 1, D_EXPERT, D_MODEL), wmap),
                  pl.BlockSpec((1, 1, 1, D_MODEL), wmap)],
        out_specs=pl.BlockSpec((MB, D_MODEL), lambda i, be, nu, br: (i, 0)),
        scratch_shapes=[pltpu.VMEM((D_MODEL, 2 * D_EXPERT), BF16), pltpu.VMEM((D_EXPERT, D_MODEL), BF16)],
    )
    return pl.pallas_call(
        _expert_kernel,
        grid_spec=grid_spec,
        out_shape=jax.ShapeDtypeStruct((p_rows, D_MODEL), F32),
        compiler_params=_cparams("arbitrary"),
        name="experts",
    )(blk_expert, n_used, blk_rows, xs, w_exp_in, b_exp_in.reshape(depth, N_EXPERTS, 1, -1),
      w_exp_out, b_exp_out.reshape(depth, N_EXPERTS, 1, -1))


def _combine_kernel(alpha, n_p_tiles, a_ref, off_ref, seg_ref, tot_ref,
                    q_ref, gate_ref, x1_ref, mod_ref, g2_ref, b2_ref, ys_hbm, o_p_ref, o_s_ref, buf, sem):
    i = pl.program_id(0)
    slot = i % 2

    def fetch(j, s):
        def per_expert(e, carry):
            n = a_ref[j * N_EXPERTS + e]
            src = seg_ref[j * N_EXPERTS + e]
            dst = off_ref[j * N_EXPERTS + e]
            _segment_copies(n, TM, lambda pos, size: pltpu.make_async_copy(
                ys_hbm.at[pl.ds(pl.multiple_of(src + pos, SEG), size)],
                buf.at[s, pl.ds(pl.multiple_of(dst + pos, SEG), size)], sem.at[s]))
            return carry

        lax.fori_loop(0, N_EXPERTS, per_expert, 0)

    @pl.when(i == 0)
    def _():
        buf[...] = jnp.zeros_like(buf)
        fetch(0, 0)

    @pl.when(i + 1 < pl.num_programs(0))
    def _():
        fetch(i + 1, 1 - slot)

    n = pl.multiple_of(tot_ref[i], SEG)

    @pl.when(n > 0)
    def _():
        pltpu.make_async_copy(ys_hbm.at[pl.ds(0, n)], buf.at[slot, pl.ds(0, n)], sem.at[slot]).wait()

    lane = lax.broadcasted_iota(jnp.int32, (TM, R_TILE), 1)
    q = q_ref[...]
    gates = gate_ref[...]
    g = jnp.zeros((TM, R_TILE), F32)
    for k in range(TOP_K):
        g = jnp.where(lane == q[:, k:k + 1], gates[:, k:k + 1], g)
    ffn = jnp.dot(g.astype(BF16), buf[slot].astype(BF16), preferred_element_type=F32)
    g2 = mod_ref[0][:, 5 * D_MODEL:6 * D_MODEL]
    res = _ln_plain(alpha * x1_ref[...] + g2 * ffn) * g2_ref[...] + b2_ref[...]
    @pl.when(i < n_p_tiles)
    def _():
        o_p_ref[...] = res

    @pl.when(i >= n_p_tiles)
    def _():
        o_s_ref[...] = res


def _combine(tables, q, gates, x1, mod, ys, wl, n_p_tiles, tiles_per_s_seq, ctx_row, alpha):
    t = x1.shape[0]
    nt = t // TM

    def mod_idx(i, *_):
        return (jnp.where(i < n_p_tiles, ctx_row, (i - n_p_tiles) // tiles_per_s_seq), 0, 0)

    out_specs = _x_specs(n_p_tiles)
    out_shape = [jax.ShapeDtypeStruct((n_p_tiles * TM, D_MODEL), F32),
                 jax.ShapeDtypeStruct((t - n_p_tiles * TM, D_MODEL), F32)]
    grid_spec = pltpu.PrefetchScalarGridSpec(
        num_scalar_prefetch=4,
        grid=(nt,),
        in_specs=[pl.BlockSpec((TM, TOP_K), lambda i, *_: (i, 0)),
                  pl.BlockSpec((TM, LANES), lambda i, *_: (i, 0)),
                  pl.BlockSpec((TM, D_MODEL), lambda i, *_: (i, 0)),
                  pl.BlockSpec((1, 1, 6 * D_MODEL), mod_idx),
                  pl.BlockSpec((1, D_MODEL), lambda i, *_: (0, 0)),
                  pl.BlockSpec((1, D_MODEL), lambda i, *_: (0, 0)),
                  pl.BlockSpec(memory_space=pl.ANY)],
        out_specs=out_specs,
        scratch_shapes=[pltpu.VMEM((2, R_TILE, D_MODEL), F32), pltpu.SemaphoreType.DMA((2,))],
    )
    return pl.pallas_call(
        functools.partial(_combine_kernel, alpha, n_p_tiles),
        grid_spec=grid_spec,
        out_shape=out_shape,
        compiler_params=_cparams("arbitrary"),
        name="combine",
    )(tables["a"], tables["off"], tables["seg"], tables["tot"], q, gates, x1, mod, wl["g_ln2"], wl["b_ln2"], ys)


def _routing_tables(counts, idx, lrank, n_blocks):
    cnt = counts[:, 0, :N_EXPERTS].astype(jnp.int32)
    a = (cnt + SEG - 1) // SEG * SEG
    off = jnp.cumsum(a, axis=1) - a
    acc = jnp.cumsum(a, axis=0) - a
    tot_e = jnp.sum(a, axis=0)
    padded = (tot_e + MB - 1) // MB * MB
    pend = jnp.cumsum(padded)
    pstart = pend - padded
    seg = pstart[None, :] + acc
    blk_row0 = jnp.arange(n_blocks, dtype=jnp.int32) * MB
    blk_expert = jnp.minimum(jnp.sum((pend[None, :] <= blk_row0[:, None]).astype(jnp.int32), axis=1), N_EXPERTS - 1)
    n_used = (pend[-1:] // MB).astype(jnp.int32)
    row_end = jnp.sum(jnp.where(blk_expert[:, None] == jnp.arange(N_EXPERTS, dtype=jnp.int32),
                                (pstart + tot_e)[None, :], 0), axis=1)
    blk_rows = jnp.clip(row_end - blk_row0, 0, MB).astype(jnp.int32)
    off_tok = jnp.repeat(off, TM, axis=0)
    hot = idx[:, :TOP_K, None] == jnp.arange(N_EXPERTS, dtype=jnp.int32)
    q = jnp.sum(jnp.where(hot, off_tok[:, None, :], 0), axis=-1) + lrank[:, :TOP_K]
    tables = {"a": a.reshape(-1), "off": off.reshape(-1), "seg": seg.reshape(-1), "tot": jnp.sum(a, axis=1),
              "tail0": pstart + tot_e, "tailn": padded - tot_e}
    return tables, q, blk_expert, n_used, blk_rows


def _pad_cols(w, width):
    return jnp.pad(w, ((0, 0), (0, width - w.shape[1])))


def _hi_lo(w):
    hi = w.astype(BF16)
    return jnp.concatenate([hi, (w - hi.astype(F32)).astype(BF16)], axis=1)


def _rot_cols(w):
    half = w.shape[1] // 2
    return jnp.concatenate([-w[:, half:], w[:, :half]], axis=1)


def _layer_weights(l, w_in, g_q_norm, w_uq, g_kv_norm, w_uk, w_uv, w_sconv, w_cconv, b_cconv,
                   g_cf_norm, b_cf_norm, w_out, g_ln1, b_ln1, g_ln2, b_ln2, w_router, b_router):
    wi = w_in[l]
    o_ckv, o_kr, o_bg = Q_RANK, Q_RANK + KV_RANK, Q_RANK + KV_RANK + QK_ROPE
    w_kr = wi[:, o_kr:o_bg]
    w_in_p = jnp.concatenate([
        _pad_cols(wi[:, :Q_RANK], COL_CKV - COL_CQ),
        wi[:, o_ckv:o_kr],
        _pad_cols(jnp.concatenate([w_kr, _rot_cols(w_kr)], axis=1), COL_BG - COL_KR),
        wi[:, o_bg:]], axis=1).astype(BF16)

    uq = jnp.pad(w_uq[l], ((0, Q_RANK_PAD - Q_RANK), (0, 0))).reshape(Q_RANK_PAD, HEADS, QK_DIM)
    zer = jnp.zeros((Q_RANK_PAD, HEADS, HEAD_PAD - QK_DIM), F32)
    wq_a = jnp.concatenate([uq, zer], axis=2).reshape(Q_RANK_PAD, HEADS * HEAD_PAD).astype(BF16)
    rot = jnp.concatenate([-uq[:, :, QK_NOPE + QK_ROPE // 2:], uq[:, :, QK_NOPE:QK_NOPE + QK_ROPE // 2]], axis=2)
    wq_b = jnp.concatenate([jnp.zeros((Q_RANK_PAD, HEADS, QK_NOPE), F32), rot, zer],
                           axis=2).reshape(Q_RANK_PAD, HEADS * HEAD_PAD).astype(BF16)

    uk_t = w_uk[l].reshape(KV_RANK, HEADS, QK_NOPE).transpose(1, 2, 0)
    top = jnp.pad(uk_t, ((0, 0), (0, 0), (0, KFEAT - KV_RANK)))
    eye = jnp.pad(jnp.eye(QK_ROPE, dtype=F32), ((0, 0), (KV_RANK, KFEAT - KV_RANK - QK_ROPE)))
    mid = jnp.broadcast_to(eye[None], (HEADS, QK_ROPE, KFEAT))
    wk_t = jnp.concatenate([top, mid, jnp.zeros((HEADS, HEAD_PAD - QK_DIM, KFEAT), F32)], axis=1).astype(BF16)

    uv = w_uv[l].reshape(KV_RANK, HEADS, V_DIM)
    zv = jnp.zeros((KV_RANK, HEADS, V_DIM), F32)
    odd = (jnp.arange(HEADS) % 2 == 1)[None, :, None]
    wv = jnp.concatenate([jnp.where(odd, zv, uv), jnp.where(odd, uv, zv)], axis=2).reshape(KV_RANK, HEADS * LANES)
    wv = jnp.pad(wv, ((0, KFEAT - KV_RANK), (0, 0)))
    ones_col = jnp.arange(HEADS) * LANES + jnp.where(jnp.arange(HEADS) % 2 == 1, 0, V_DIM)
    wv = wv.at[ONES_FEAT, ones_col].set(1.0).astype(BF16)

    r1 = lambda v: v.reshape(1, -1)
    return {
        "w_in": w_in_p,
        "g_q": jnp.pad(g_q_norm[l], (0, Q_RANK_PAD - Q_RANK)).reshape(1, -1),
        "g_kv": r1(g_kv_norm[l]),
        "wq_a": wq_a, "wq_b": wq_b, "wk_t": wk_t, "wv": wv,
        "w_sconv": w_sconv[l], "w_cconv": w_cconv[l], "b_cconv": r1(b_cconv[l]),
        "g_cf": r1(g_cf_norm[l]), "b_cf": r1(b_cf_norm[l]),
        "w_out": w_out[l].astype(BF16),
        "g_ln1": r1(g_ln1[l]), "b_ln1": r1(b_ln1[l]), "g_ln2": r1(g_ln2[l]), "b_ln2": r1(b_ln2[l]),
        "w_router": _hi_lo(_pad_cols(w_router[l], LANES)),
        "b_router": jnp.pad(b_router[l], (0, LANES - N_EXPERTS)).reshape(1, -1),
    }


def _rope_tables(n_lat):
    rows = n_lat // GRID_W
    row = jnp.repeat(jnp.arange(rows, dtype=F32), GRID_W)
    col = jnp.tile(jnp.arange(GRID_W, dtype=F32), rows)
    axis_dim = QK_ROPE // 2
    inv = ROPE_BASE ** (-jnp.arange(0, axis_dim, 2, dtype=F32) / axis_dim)
    ang = jnp.concatenate([row[:, None] * inv, col[:, None] * inv], axis=-1)
    cos, sin = jnp.cos(ang), jnp.sin(ang)
    one = jnp.ones((n_lat, QK_NOPE), F32)
    zq = jnp.zeros((n_lat, HEAD_PAD - QK_DIM), F32)
    zk = jnp.zeros((n_lat, LANES - QK_ROPE), F32)
    return (jnp.concatenate([one, cos, cos, zq], axis=1),
            jnp.concatenate([0.0 * one, sin, sin, zq], axis=1),
            jnp.concatenate([cos, cos, zk], axis=1),
            jnp.concatenate([sin, sin, zk], axis=1))


def kernel(x_prompt, x_sample, cache_ckv, cache_krope, c, c_ctx, w_mod, b_mod, w_in, g_q_norm, w_uq, g_kv_norm, w_uk, w_uv, w_sconv, w_cconv, b_cconv, g_cf_norm, b_cf_norm, w_out, g_ln1, b_ln1, g_ln2, b_ln2, w_router, b_router, w_exp_in, b_exp_in, w_exp_out, b_exp_out):
    bp, n_p, d = x_prompt.shape
    bs, n_s, _ = x_sample.shape
    depth = w_mod.shape[0]
    past = cache_ckv.shape[2]
    assert d == D_MODEL and n_p == TM and n_s % TM == 0 and n_s % GRID_W == 0
    t_p, t_s = bp * n_p, bs * n_s
    assert t_p % n_s == 0 and past % LANES == 0
    t = t_p + t_s
    n_p_tiles = t_p // TM
    tps = n_s // TM
    alpha = float((2 * depth) ** 0.25)

    ctx_row = bs
    rows = -(-(bs + 1) // SUBLANES) * SUBLANES
    c_rows = jnp.concatenate([c, c_ctx[None], jnp.zeros((rows - bs - 1, d), F32)], axis=0)
    mod_all = _modulation(c_rows, w_mod, b_mod)

    tabs = _rope_tables(n_s)
    x_p, x_s = x_prompt.reshape(t_p, d), x_sample.reshape(t_s, d)
    nt = t // TM
    n_blocks = -(-(t * TOP_K + nt * N_EXPERTS * (SEG - 1)) // MB) + N_EXPERTS
    p_rows = n_blocks * MB

    new_ckv, new_krope = [], []
    for l in range(depth):
        wl = _layer_weights(l, w_in, g_q_norm, w_uq, g_kv_norm, w_uk, w_uv, w_sconv, w_cconv, b_cconv,
                            g_cf_norm, b_cf_norm, w_out, g_ln1, b_ln1, g_ln2, b_ln2, w_router, b_router)
        mod = mod_all[l].reshape(rows, 1, 6 * d)
        q, ckv, krope, kfeat, conv_in = _in_proj(x_p, x_s, mod, wl, tabs, n_p_tiles, tps, ctx_row)
        new_ckv.append(ckv[:t_p].reshape(bp, n_p, KV_RANK))
        new_krope.append(krope[:t_p].reshape(bp, n_p, QK_ROPE))

        kcache = jnp.concatenate(
            [cache_ckv[:, l], cache_krope[:, l], jnp.ones((bs, past, 1), F32),
             jnp.zeros((bs, past, KFEAT - ONES_FEAT - 1), F32)], axis=-1).astype(BF16)
        attn_p = _attention(q, kfeat, None, wl, bp, n_p, 0)
        attn_s = _attention(q, kfeat, kcache, wl, bs, n_s, t_p)
        x1, h2, idx, lrank, gates, counts = _mix(x_p, x_s, attn_p, attn_s, conv_in, mod, wl, n_p_tiles, tps,
                                                 ctx_row, alpha)
        tables, q, blk_expert, n_used, blk_rows = _routing_tables(counts, idx, lrank, n_blocks)
        xs = _dispatch(tables, n_used, h2, q.T, p_rows)
        ys = _experts(blk_expert, n_used, blk_rows, xs, l, w_exp_in, b_exp_in, w_exp_out, b_exp_out)
        x_p, x_s = _combine(tables, q, gates, x1, mod, ys, wl, n_p_tiles, tps, ctx_row, alpha)

    return (x_p.reshape(bp, n_p, d), x_s.reshape(bs, n_s, d),
            jnp.stack(new_ckv, axis=1), jnp.stack(new_krope, axis=1))
```

```python
import functools
import math

import jax
import jax.numpy as jnp
from jax import lax
from jax.experimental import pallas as pl
from jax.experimental.pallas import tpu as pltpu

F32 = jnp.float32
BF16 = jnp.bfloat16

D_MODEL = 1024
GRID_W = 64
HEADS = 8
QK_NOPE = 64
QK_ROPE = 32
QK_DIM = QK_NOPE + QK_ROPE
V_DIM = 64
Q_RANK = 192
KV_RANK = 128
SC_DIM = 256
SC_WIDTH = 3
CF_DIM = 256
CF_WIDTH = 31
N_EXPERTS = 32
TOP_K = 4
D_EXPERT = 1024
SWIGLU_LIMIT = 7.0
SWIGLU_ALPHA = 1.702
ROPE_BASE = 10000.0
EPS = 1e-6

LANES = 128
SUBLANES = 8

TM = 256
TQ = 512
MB = 512
HALO = 16
SEG = SUBLANES
TD = 512
R_TILE = TD * TOP_K + N_EXPERTS * SEG
MXU_ROWSUM_MIN_KEYS = 2048
DISPATCH_SLOTS = 2
HEAD_PAD = LANES
Q_RANK_PAD = 256
KFEAT = 256
ONES_FEAT = KV_RANK + QK_ROPE
COL_CQ, COL_CKV, COL_KR, COL_BG, COL_CG, COL_XS, COL_GA, COL_GB, IN_COLS_PAD = (
    0, 256, 384, 512, 768, 1024, 1280, 1536, 1792)
VMEM_LIMIT = 60 * 1024 * 1024


def _cparams(*sem):
    return pltpu.CompilerParams(dimension_semantics=sem, vmem_limit_bytes=VMEM_LIMIT)


def _ln_plain(x):
    mu = jnp.mean(x, axis=-1, keepdims=True)
    xc = x - mu
    var = jnp.mean(xc * xc, axis=-1, keepdims=True)
    return xc * lax.rsqrt(var + EPS)


def _mod_kernel(c_ref, w_ref, b_ref, o_ref):
    c = c_ref[...]
    s = c * jax.nn.sigmoid(c)
    o_ref[0] = jnp.dot(s, w_ref[0], preferred_element_type=F32,
                       precision=lax.Precision.HIGHEST) + b_ref[0]


def _modulation(c_rows, w_mod, b_mod):
    depth = w_mod.shape[0]
    rows = c_rows.shape[0]
    nblk = w_mod.shape[2] // D_MODEL
    return pl.pallas_call(
        _mod_kernel,
        grid=(depth, nblk),
        in_specs=[pl.BlockSpec((rows, D_MODEL), lambda l, j: (0, 0)),
                  pl.BlockSpec((1, D_MODEL, D_MODEL), lambda l, j: (l, 0, j)),
                  pl.BlockSpec((1, 1, D_MODEL), lambda l, j: (l, 0, j))],
        out_specs=pl.BlockSpec((1, rows, D_MODEL), lambda l, j: (l, 0, j)),
        out_shape=jax.ShapeDtypeStruct((depth, rows, w_mod.shape[2]), F32),
        compiler_params=_cparams("arbitrary", "arbitrary"),
        name="modulation",
    )(c_rows, w_mod, b_mod.reshape(depth, 1, -1))


def _in_proj_kernel(n_p_tiles, xp_ref, xs_ref, mod_ref, win_ref, gq_ref, gkv_ref, wqa_ref, wqb_ref,
                    tcq_ref, tsq_ref, tck_ref, tsk_ref,
                    q_ref, ckv_ref, kr_ref, kf_ref, cv_ref):
    is_s = pl.program_id(0) >= n_p_tiles
    mod = mod_ref[0]
    x = jnp.where(is_s, xs_ref[...], xp_ref[...])
    h = _ln_plain(x) * (1.0 + mod[:, D_MODEL:2 * D_MODEL]) + mod[:, 0:D_MODEL]
    z = jnp.dot(h.astype(BF16), win_ref[...], preferred_element_type=F32)

    cq = z[:, COL_CQ:COL_CQ + Q_RANK_PAD]
    cqn = cq * lax.rsqrt(jnp.sum(cq * cq, axis=-1, keepdims=True) * (1.0 / Q_RANK) + EPS) * gq_ref[...]
    cqn = cqn.astype(BF16)
    qa = jnp.dot(cqn, wqa_ref[...], preferred_element_type=F32)
    qb = jnp.dot(cqn, wqb_ref[...], preferred_element_type=F32)
    cos_q = jnp.where(is_s, tcq_ref[...], 1.0)
    sin_q = jnp.where(is_s, tsq_ref[...], 0.0)
    scale = math.log2(math.e) / math.sqrt(QK_DIM)
    for hd in range(HEADS):
        sl = slice(hd * HEAD_PAD, (hd + 1) * HEAD_PAD)
        q_ref[hd] = ((qa[:, sl] * cos_q + qb[:, sl] * sin_q) * scale).astype(BF16)

    cr = z[:, COL_CKV:COL_CKV + KV_RANK]
    ckv = cr * lax.rsqrt(jnp.mean(cr * cr, axis=-1, keepdims=True) + EPS) * gkv_ref[...]
    ckv_ref[...] = ckv
    zk = z[:, COL_KR:COL_KR + LANES]
    kr_ref[...] = zk[:, 0:QK_ROPE]
    lane = lax.broadcasted_iota(jnp.int32, zk.shape, 1)
    raw = jnp.where(lane < QK_ROPE, zk, 0.0)
    roped = zk * tck_ref[...] + pltpu.roll(zk, LANES - QK_ROPE, 1) * tsk_ref[...]
    kf_ref[:, 0:KV_RANK] = ckv.astype(BF16)
    kf_ref[:, KV_RANK:KFEAT] = jnp.where(lane == ONES_FEAT - KV_RANK, 1.0, jnp.where(is_s, roped, raw)).astype(BF16)

    cv_ref[:, 0:SC_DIM] = z[:, COL_BG:COL_BG + SC_DIM]
    cv_ref[:, SC_DIM:2 * SC_DIM] = z[:, COL_CG:COL_CG + SC_DIM] * z[:, COL_XS:COL_XS + SC_DIM]
    cv_ref[:, 2 * SC_DIM:] = z[:, COL_GA:COL_GA + CF_DIM] * jax.nn.sigmoid(z[:, COL_GB:COL_GB + CF_DIM])


def _x_specs(n_p_tiles, rows=TM):
    return [pl.BlockSpec((rows, D_MODEL), lambda i, *_: (jnp.minimum(i, n_p_tiles - 1), 0)),
            pl.BlockSpec((rows, D_MODEL), lambda i, *_: (jnp.maximum(i - n_p_tiles, 0), 0))]


def _in_proj(x_p, x_s, mod, wl, tabs, n_p_tiles, tiles_per_s_seq, ctx_row):
    t = x_p.shape[0] + x_s.shape[0]
    nt = t // TM

    def mod_idx(i):
        return (jnp.where(i < n_p_tiles, ctx_row, (i - n_p_tiles) // tiles_per_s_seq), 0, 0)

    def pos_idx(i):
        return (jnp.where(i < n_p_tiles, 0, (i - n_p_tiles) % tiles_per_s_seq), 0)

    full = lambda shp: pl.BlockSpec(shp, lambda i: (0,) * len(shp))
    tab = pl.BlockSpec((TM, LANES), pos_idx)
    return pl.pallas_call(
        functools.partial(_in_proj_kernel, n_p_tiles),
        grid=(nt,),
        in_specs=_x_specs(n_p_tiles) + [
                  pl.BlockSpec((1, 1, 6 * D_MODEL), mod_idx),
                  full((D_MODEL, IN_COLS_PAD)),
                  full((1, Q_RANK_PAD)), full((1, KV_RANK)),
                  full((Q_RANK_PAD, HEADS * HEAD_PAD)), full((Q_RANK_PAD, HEADS * HEAD_PAD)),
                  tab, tab, tab, tab],
        out_specs=[pl.BlockSpec((HEADS, TM, HEAD_PAD), lambda i: (0, i, 0)),
                   pl.BlockSpec((TM, KV_RANK), lambda i: (i, 0)),
                   pl.BlockSpec((TM, QK_ROPE), lambda i: (i, 0)),
                   pl.BlockSpec((TM, KFEAT), lambda i: (i, 0)),
                   pl.BlockSpec((TM, 3 * SC_DIM), lambda i: (i, 0))],
        out_shape=[jax.ShapeDtypeStruct((HEADS, t, HEAD_PAD), BF16),
                   jax.ShapeDtypeStruct((t, KV_RANK), F32),
                   jax.ShapeDtypeStruct((t, QK_ROPE), F32),
                   jax.ShapeDtypeStruct((t, KFEAT), BF16),
                   jax.ShapeDtypeStruct((t, 3 * SC_DIM), F32)],
        compiler_params=_cparams("arbitrary"),
        name="in_proj",
    )(x_p, x_s, mod, wl["w_in"], wl["g_q"], wl["g_kv"], wl["wq_a"], wl["wq_b"], *tabs)


def _attn_kernel(n_own, n_cache, q_ref, kf_ref, *rest):
    if n_cache:
        kc_ref, wkt_ref, wv_ref, o_ref, kt_scr, v_scr = rest
    else:
        wkt_ref, wv_ref, o_ref, kt_scr, v_scr = rest

    @pl.when(pl.program_id(1) == 0)
    def _():
        def fill(feat, lo, n):
            for hd in range(HEADS):
                kt = lax.dot_general(wkt_ref[hd], feat, (((1,), (1,)), ((), ())),
                                     preferred_element_type=F32)
                kt_scr[hd, :, lo:lo + n] = kt.astype(BF16)
                v = jnp.dot(feat, wv_ref[:, hd * LANES:(hd + 1) * LANES], preferred_element_type=F32)
                v_scr[hd, lo:lo + n, :] = v.astype(BF16)

        fill(kf_ref[...], 0, n_own)
        if n_cache:
            fill(kc_ref[0], n_own, n_cache)

    mxu_rowsum = n_own + n_cache >= MXU_ROWSUM_MIN_KEYS
    lane = lax.broadcasted_iota(jnp.int32, o_ref.shape[1:], 1)
    for pair in range(HEADS // 2):
        outs, sums = [], []
        for hd in (2 * pair, 2 * pair + 1):
            s = jnp.dot(q_ref[hd], kt_scr[hd], preferred_element_type=F32)
            p = jnp.exp2(s - jnp.max(s, axis=-1, keepdims=True))
            o = jnp.dot(p.astype(BF16), v_scr[hd], preferred_element_type=F32)
            outs.append(o)
            ones_lane = V_DIM if hd % 2 == 0 else 0
            sums.append(o[:, ones_lane:ones_lane + 1] if mxu_rowsum else jnp.sum(p, axis=-1, keepdims=True))
        o_ref[pair] = jnp.where(lane < V_DIM, outs[0] / sums[0], outs[1] / sums[1]).astype(BF16)


def _attention(q, kfeat, kcache, wl, batch, n_own, row0):
    n_cache = 0 if kcache is None else kcache.shape[1]
    m = n_own + n_cache
    tq = min(TQ, n_own)
    nq = n_own // tq
    q0 = row0 // tq
    k0 = row0 // n_own
    in_specs = [pl.BlockSpec((HEADS, tq, HEAD_PAD), lambda b, j: (0, q0 + b * nq + j, 0)),
                pl.BlockSpec((n_own, KFEAT), lambda b, j: (k0 + b, 0))]
    args = [q, kfeat]
    if n_cache:
        in_specs.append(pl.BlockSpec((1, n_cache, KFEAT), lambda b, j: (b, 0, 0)))
        args.append(kcache)
    in_specs += [pl.BlockSpec((HEADS, HEAD_PAD, KFEAT), lambda b, j: (0, 0, 0)),
                 pl.BlockSpec((KFEAT, HEADS * LANES), lambda b, j: (0, 0))]
    args += [wl["wk_t"], wl["wv"]]
    return pl.pallas_call(
        functools.partial(_attn_kernel, n_own, n_cache),
        grid=(batch, nq),
        in_specs=in_specs,
        out_specs=pl.BlockSpec((HEADS // 2, tq, LANES), lambda b, j: (0, b * nq + j, 0)),
        out_shape=jax.ShapeDtypeStruct((HEADS // 2, batch * n_own, LANES), BF16),
        scratch_shapes=[pltpu.VMEM((HEADS, HEAD_PAD, m), BF16), pltpu.VMEM((HEADS, m, LANES), BF16)],
        compiler_params=_cparams("arbitrary", "arbitrary"),
        name="attention_cache" if n_cache else "attention_ctx",
    )(*args)


def _mix_kernel(n_p_tiles, tiles_per_s_seq, alpha,
                xp_ref, xs_ref, atp_ref, ats_ref, cprev_ref, ccur_ref, cnext_ref, mod_ref,
                wsc_ref, wcc_ref, bcc_ref, gcf_ref, bcf_ref, wout_ref, g1_ref, b1_ref,
                wr_ref, br_ref,
                x1_ref, h2_ref, idx_ref, rank_ref, gate_ref, cnt_ref,
                ext_scr, shf_scr):
    i = pl.program_id(0)
    is_s = i >= n_p_tiles
    jj = (i - n_p_tiles) % tiles_per_s_seq
    has_prev = jnp.logical_and(is_s, jj > 0)
    has_next = jnp.logical_and(is_s, jj < tiles_per_s_seq - 1)

    ext_scr[0:HALO, :] = jnp.where(has_prev, cprev_ref[:, SC_DIM:], 0.0)
    ext_scr[HALO:HALO + TM, :] = ccur_ref[:, SC_DIM:]
    ext_scr[HALO + TM:, :] = jnp.where(has_next, cnext_ref[:, SC_DIM:], 0.0)

    sconv = jnp.zeros((TM, SC_DIM), F32)
    for k in range(SC_WIDTH):
        o = HALO - SC_WIDTH // 2 + k
        sconv = sconv + ext_scr[o:o + TM, 0:SC_DIM] * wsc_ref[k:k + 1, :]
    sc_out = ccur_ref[:, 0:SC_DIM] * sconv

    n_sh = TM + 2 * HALO - SUBLANES
    for s in range(1, SUBLANES):
        shf_scr[s - 1] = ext_scr[s:s + n_sh, SC_DIM:]
    cconv = jnp.zeros((TM, CF_DIM), F32)
    for k in range(CF_WIDTH):
        o = HALO - CF_WIDTH // 2 + k
        s, base = o % SUBLANES, o - o % SUBLANES
        rows = ext_scr[base:base + TM, SC_DIM:] if s == 0 else shf_scr[s - 1, base:base + TM, :]
        cconv = cconv + rows * wcc_ref[k:k + 1, :]
    cf = _ln_plain(cconv + bcc_ref[...]) * gcf_ref[...] + bcf_ref[...]
    cf = cf * jax.nn.sigmoid(cf)

    at = [jnp.where(is_s, ats_ref[p], atp_ref[p]) for p in range(HEADS // 2)]
    mix_in = jnp.concatenate(at + [sc_out.astype(BF16), cf.astype(BF16)], axis=1)
    mix = jnp.dot(mix_in, wout_ref[...], preferred_element_type=F32)

    mod = mod_ref[0]
    g1 = mod[:, 2 * D_MODEL:3 * D_MODEL]
    sh2 = mod[:, 3 * D_MODEL:4 * D_MODEL]
    sc2 = mod[:, 4 * D_MODEL:5 * D_MODEL]
    x = jnp.where(is_s, xs_ref[...], xp_ref[...])
    x1 = _ln_plain(alpha * x + g1 * mix) * g1_ref[...] + b1_ref[...]
    x1_ref[...] = x1
    h2 = _ln_plain(x1) * (1.0 + sc2) + sh2
    h2_hi = h2.astype(BF16)
    h2_ref[...] = h2_hi
    h2_lo = (h2 - h2_hi.astype(F32)).astype(BF16)
    hi_both = jnp.dot(h2_hi, wr_ref[...], preferred_element_type=F32)
    lo_hi = jnp.dot(h2_lo, wr_ref[:, 0:LANES], preferred_element_type=F32)
    logits = hi_both[:, 0:LANES] + hi_both[:, LANES:] + lo_hi + br_ref[...]
    lane = lax.broadcasted_iota(jnp.int32, (TM, LANES), 1)
    lane_f = lane.astype(F32)
    neg = jnp.float32(-jnp.inf)
    cur = jnp.where(lane < N_EXPERTS, logits, neg)
    vals, hots, idxs = [], [], []
    for _ in range(TOP_K):
        mx = jnp.max(cur, axis=-1, keepdims=True)
        ik = jnp.min(jnp.where(cur == mx, lane_f, float(LANES)), axis=-1, keepdims=True)
        ik = jnp.minimum(ik, float(N_EXPERTS - 1))
        hot = lane_f == ik
        cur = jnp.where(hot, neg, cur)
        vals.append(mx)
        hots.append(hot)
        idxs.append(ik.astype(jnp.int32))
    exps = [jnp.exp(v - vals[0]) for v in vals]
    denom = exps[0] + exps[1] + exps[2] + exps[3]
    sel = jnp.where(hots[0] | hots[1] | hots[2] | hots[3], 1.0, 0.0)

    r_i = lax.broadcasted_iota(jnp.int32, (TM, TM), 0)
    c_i = lax.broadcasted_iota(jnp.int32, (TM, TM), 1)
    tri = jnp.where(c_i < r_i, 1.0, 0.0).astype(BF16)
    rank_all = jnp.dot(tri, sel.astype(BF16), preferred_element_type=F32)
    cnt_ref[0] = jnp.broadcast_to(jnp.sum(sel, axis=0, keepdims=True), (SUBLANES, LANES))

    idx_o = jnp.zeros((TM, LANES), jnp.int32)
    rank_o = jnp.zeros((TM, LANES), jnp.int32)
    gate_o = jnp.zeros((TM, LANES), F32)
    for k in range(TOP_K):
        rk = jnp.sum(jnp.where(hots[k], rank_all, 0.0), axis=-1, keepdims=True).astype(jnp.int32)
        idx_o = jnp.where(lane == k, idxs[k], idx_o)
        rank_o = jnp.where(lane == k, rk, rank_o)
        gate_o = jnp.where(lane == k, exps[k] / denom, gate_o)
    idx_ref[...] = idx_o
    rank_ref[...] = rank_o
    gate_ref[...] = gate_o


def _mix(x_p, x_s, attn_p, attn_s, conv_in, mod, wl, n_p_tiles, tiles_per_s_seq, ctx_row, alpha):
    t = x_p.shape[0] + x_s.shape[0]
    nt = t // TM
    hpt = TM // HALO
    n_halo = t // HALO

    def mod_idx(i):
        return (jnp.where(i < n_p_tiles, ctx_row, (i - n_p_tiles) // tiles_per_s_seq), 0, 0)

    full = lambda shp: pl.BlockSpec(shp, lambda i: (0,) * len(shp))
    row = lambda w: pl.BlockSpec((TM, w), lambda i: (i, 0))
    return pl.pallas_call(
        functools.partial(_mix_kernel, n_p_tiles, tiles_per_s_seq, alpha),
        grid=(nt,),
        in_specs=_x_specs(n_p_tiles) + [
                  pl.BlockSpec((HEADS // 2, TM, LANES), lambda i: (0, jnp.minimum(i, n_p_tiles - 1), 0)),
                  pl.BlockSpec((HEADS // 2, TM, LANES), lambda i: (0, jnp.maximum(i - n_p_tiles, 0), 0)),
                  pl.BlockSpec((HALO, 3 * SC_DIM), lambda i: (jnp.maximum(i * hpt - 1, 0), 0)),
                  row(3 * SC_DIM),
                  pl.BlockSpec((HALO, 3 * SC_DIM), lambda i: (jnp.minimum((i + 1) * hpt, n_halo - 1), 0)),
                  pl.BlockSpec((1, 1, 6 * D_MODEL), mod_idx),
                  full((SC_WIDTH, SC_DIM)), full((CF_WIDTH, CF_DIM)), full((1, CF_DIM)),
                  full((1, CF_DIM)), full((1, CF_DIM)),
                  full((D_MODEL, D_MODEL)), full((1, D_MODEL)), full((1, D_MODEL)),
                  full((D_MODEL, 2 * LANES)), full((1, LANES))],
        out_specs=[row(D_MODEL), row(D_MODEL),
                   row(LANES), row(LANES), row(LANES),
                   pl.BlockSpec((1, SUBLANES, LANES), lambda i: (i, 0, 0))],
        out_shape=[jax.ShapeDtypeStruct((t, D_MODEL), F32),
                   jax.ShapeDtypeStruct((t, D_MODEL), BF16),
                   jax.ShapeDtypeStruct((t, LANES), jnp.int32),
                   jax.ShapeDtypeStruct((t, LANES), jnp.int32),
                   jax.ShapeDtypeStruct((t, LANES), F32),
                   jax.ShapeDtypeStruct((nt, SUBLANES, LANES), F32)],
        scratch_shapes=[pltpu.VMEM((TM + 2 * HALO, 2 * SC_DIM), F32),
                        pltpu.VMEM((SUBLANES - 1, TM + 2 * HALO - SUBLANES, CF_DIM), F32)],
        compiler_params=_cparams("arbitrary"),
        name="mix",
    )(x_p, x_s, attn_p, attn_s, conv_in, conv_in, conv_in, mod,
      wl["w_sconv"], wl["w_cconv"], wl["b_cconv"], wl["g_cf"], wl["b_cf"],
      wl["w_out"], wl["g_ln1"], wl["b_ln1"], wl["w_router"], wl["b_router"])


def _segment_copies(n, max_rows, make_copy, start=True):
    pos = jnp.int32(0)
    size = max_rows
    while size >= SEG:
        take = n & size

        @pl.when(take != 0)
        def _(pos=pos, size=size):
            cp = make_copy(pos, size)
            cp.start() if start else cp.wait()

        pos = pos + take
        size //= 2


def _dispatch_kernel(a_ref, off_ref, seg_ref, tot_ref, tail0_ref, tailn_ref, nu_ref,
                     h_ref, qt_ref, xs_hbm, buf, zbuf, sem, zsem):
    i = pl.program_id(0)
    last = pl.num_programs(0) - 1
    n_slots = buf.shape[0]
    slot = i % n_slots

    r = lax.broadcasted_iota(jnp.int32, (R_TILE, TD), 0)
    hit = r == qt_ref[0:1, :]
    for k in range(1, TOP_K):
        hit = jnp.logical_or(hit, r == qt_ref[k:k + 1, :])
    perm = jnp.where(hit, 1.0, 0.0).astype(BF16)
    buf[slot] = jnp.dot(perm, h_ref[...], preferred_element_type=F32)

    def wait_tile(j, s):
        n = pl.multiple_of(tot_ref[j], SEG)

        @pl.when(n > 0)
        def _():
            pltpu.make_async_copy(buf.at[s, pl.ds(0, n)], xs_hbm.at[pl.ds(0, n)], sem.at[s]).wait()

    @pl.when(i >= n_slots - 1)
    def _():
        wait_tile(i + 1 - n_slots, (i + 1) % n_slots)

    def per_expert(e, carry):
        n = a_ref[i * N_EXPERTS + e]
        src = off_ref[i * N_EXPERTS + e]
        dst = seg_ref[i * N_EXPERTS + e]
        _segment_copies(n, TD, lambda pos, size: pltpu.make_async_copy(
            buf.at[slot, pl.ds(pl.multiple_of(src + pos, SEG), size)],
            xs_hbm.at[pl.ds(pl.multiple_of(dst + pos, SEG), size)], sem.at[slot]))
        return carry

    lax.fori_loop(0, N_EXPERTS, per_expert, 0)

    @pl.when(i == last)
    def _():
        zbuf[...] = jnp.zeros_like(zbuf)

        def tail(start):
            def per_tail(e, carry):
                dst = tail0_ref[e]
                _segment_copies(tailn_ref[e], MB // 2, lambda pos, size: pltpu.make_async_copy(
                    zbuf.at[pl.ds(0, size)], xs_hbm.at[pl.ds(pl.multiple_of(dst + pos, SEG), size)], zsem),
                    start=start)
                return carry
            lax.fori_loop(0, N_EXPERTS, per_tail, 0)

        def spare(b):
            return pltpu.make_async_copy(zbuf, xs_hbm.at[pl.ds(pl.multiple_of(b * MB, MB), MB)], zsem)

        def start_spare(b, carry):
            spare(b).start()
            return carry

        def wait_spare(b, carry):
            spare(b).wait()
            return carry

        n_blocks = xs_hbm.shape[0] // MB
        tail(True)
        lax.fori_loop(nu_ref[0], n_blocks, start_spare, 0)
        tail(False)
        lax.fori_loop(nu_ref[0], n_blocks, wait_spare, 0)
        for back in range(n_slots - 1):
            @pl.when(i >= back)
            def _(back=back):
                wait_tile(i - back, (i - back) % n_slots)


def _dispatch(tables, n_used, h2, q_t, p_rows):
    t = h2.shape[0]
    grid_spec = pltpu.PrefetchScalarGridSpec(
        num_scalar_prefetch=7,
        grid=(t // TD,),
        in_specs=[pl.BlockSpec((TD, D_MODEL), lambda i, *_: (i, 0)),
                  pl.BlockSpec((TOP_K, TD), lambda i, *_: (0, i))],
        out_specs=pl.BlockSpec(memory_space=pl.ANY),
        scratch_shapes=[pltpu.VMEM((DISPATCH_SLOTS, R_TILE, D_MODEL), F32), pltpu.VMEM((MB, D_MODEL), F32),
                        pltpu.SemaphoreType.DMA((DISPATCH_SLOTS,)), pltpu.SemaphoreType.DMA(())],
    )
    return pl.pallas_call(
        _dispatch_kernel,
        grid_spec=grid_spec,
        out_shape=jax.ShapeDtypeStruct((p_rows, D_MODEL), F32),
        compiler_params=_cparams("arbitrary"),
        name="dispatch",
    )(tables["a"], tables["off"], tables["seg"], tables["tot"], tables["tail0"], tables["tailn"], n_used, h2, q_t)


def _expert_kernel(be_ref, nu_ref, br_ref, xs_ref, w1_ref, b1_ref, w2_ref, b2_ref, ys_ref, w1_scr, w2_scr):
    i = pl.program_id(0)
    used = i < nu_ref[0]
    half = MB // 2
    short = br_ref[i] <= half

    @pl.when(jnp.logical_and(used, jnp.logical_or(i == 0, be_ref[i] != be_ref[jnp.maximum(i - 1, 0)])))
    def _():
        w1_scr[...] = w1_ref[0, 0].astype(BF16)
        w2_scr[...] = w2_ref[0, 0].astype(BF16)

    def ffn(rows):
        z = jnp.dot(xs_ref[0:rows, :].astype(BF16), w1_scr[...], preferred_element_type=F32) + b1_ref[0, 0]
        gate = jnp.minimum(z[:, :D_EXPERT], SWIGLU_LIMIT)
        up = jnp.clip(z[:, D_EXPERT:], -SWIGLU_LIMIT, SWIGLU_LIMIT)
        act = (up + 1.0) * gate * jax.nn.sigmoid(SWIGLU_ALPHA * gate)
        ys_ref[0:rows, :] = jnp.dot(act.astype(BF16), w2_scr[...], preferred_element_type=F32) + b2_ref[0, 0]

    @pl.when(jnp.logical_and(used, jnp.logical_not(short)))
    def _():
        ffn(MB)

    @pl.when(jnp.logical_and(used, short))
    def _():
        ffn(half)
        ys_ref[half:, :] = jnp.zeros((MB - half, D_MODEL), F32)

    @pl.when(jnp.logical_not(used))
    def _():
        ys_ref[...] = jnp.zeros_like(ys_ref)


def _experts(blk_expert, n_used, blk_rows, xs, l, w_exp_in, b_exp_in, w_exp_out, b_exp_out):
    p_rows = xs.shape[0]
    nb = p_rows // MB
    depth = w_exp_in.shape[0]
    blk = lambda i, nu: jnp.maximum(jnp.minimum(i, nu[0] - 1), 0)
    wmap = lambda i, be, nu, br: (l, be[blk(i, nu)], 0, 0)
    xmap = lambda i, be, nu, br: (blk(i, nu), 0)
    grid_spec = pltpu.PrefetchScalarGridSpec(
        num_scalar_prefetch=3,
        grid=(nb,),
        in_specs=[pl.BlockSpec((MB, D_MODEL), xmap),
                  pl.BlockSpec((1, 1, D_MODEL, 2 * D_EXPERT), wmap),
                  pl.BlockSpec((1, 1, 1, 2 * D_EXPERT), wmap),
                  pl.BlockSpec((1, 1, D_EXPERT, D_MODEL), wmap),
                  pl.BlockSpec((1, 1, 1, D_MODEL), wmap)],
        out_specs=pl.BlockSpec((MB, D_MODEL), lambda i, be, nu, br: (i, 0)),
        scratch_shapes=[pltpu.VMEM((D_MODEL, 2 * D_EXPERT), BF16), pltpu.VMEM((D_EXPERT, D_MODEL), BF16)],
    )
    return pl.pallas_call(
        _expert_kernel,
        grid_spec=grid_spec,
        out_shape=jax.ShapeDtypeStruct((p_rows, D_MODEL), F32),
        compiler_params=_cparams("arbitrary"),
        name="experts",
    )(blk_expert, n_used, blk_rows, xs, w_exp_in, b_exp_in.reshape(depth, N_EXPERTS, 1, -1),
      w_exp_out, b_exp_out.reshape(depth, N_EXPERTS, 1, -1))


def _combine_kernel(alpha, n_p_tiles, a_ref, off_ref, seg_ref, tot_ref,
                    q_ref, gate_ref, x1_ref, mod_ref, g2_ref, b2_ref, ys_hbm, o_p_ref, o_s_ref, buf, sem):
    i = pl.program_id(0)
    slot = i % 2

    def fetch(j, s):
        def per_expert(e, carry):
            n = a_ref[j * N_EXPERTS + e]
            src = seg_ref[j * N_EXPERTS + e]
            dst = off_ref[j * N_EXPERTS + e]
            _segment_copies(n, TD, lambda pos, size: pltpu.make_async_copy(
                ys_hbm.at[pl.ds(pl.multiple_of(src + pos, SEG), size)],
                buf.at[s, pl.ds(pl.multiple_of(dst + pos, SEG), size)], sem.at[s]))
            return carry

        lax.fori_loop(0, N_EXPERTS, per_expert, 0)

    @pl.when(i == 0)
    def _():
        buf[...] = jnp.zeros_like(buf)
        fetch(0, 0)

    @pl.when(i + 1 < pl.num_programs(0))
    def _():
        fetch(i + 1, 1 - slot)

    n = pl.multiple_of(tot_ref[i], SEG)

    @pl.when(n > 0)
    def _():
        pltpu.make_async_copy(ys_hbm.at[pl.ds(0, n)], buf.at[slot, pl.ds(0, n)], sem.at[slot]).wait()

    lane = lax.broadcasted_iota(jnp.int32, (TD, R_TILE), 1)
    q = q_ref[...]
    gates = gate_ref[...]
    g = jnp.zeros((TD, R_TILE), F32)
    for k in range(TOP_K):
        g = jnp.where(lane == q[:, k:k + 1], gates[:, k:k + 1], g)
    ffn = jnp.dot(g.astype(BF16), buf[slot].astype(BF16), preferred_element_type=F32)
    g2 = mod_ref[0][:, 5 * D_MODEL:6 * D_MODEL]
    res = _ln_plain(alpha * x1_ref[...] + g2 * ffn) * g2_ref[...] + b2_ref[...]
    @pl.when(i < n_p_tiles)
    def _():
        o_p_ref[...] = res

    @pl.when(i >= n_p_tiles)
    def _():
        o_s_ref[...] = res


def _combine(tables, q, gates, x1, mod, ys, wl, n_p_tiles, tiles_per_s_seq, ctx_row, alpha):
    t = x1.shape[0]
    nt = t // TD

    def mod_idx(i, *_):
        return (jnp.where(i < n_p_tiles, ctx_row, (i - n_p_tiles) // tiles_per_s_seq), 0, 0)

    out_specs = _x_specs(n_p_tiles, TD)
    out_shape = [jax.ShapeDtypeStruct((n_p_tiles * TD, D_MODEL), F32),
                 jax.ShapeDtypeStruct((t - n_p_tiles * TD, D_MODEL), F32)]
    grid_spec = pltpu.PrefetchScalarGridSpec(
        num_scalar_prefetch=4,
        grid=(nt,),
        in_specs=[pl.BlockSpec((TD, TOP_K), lambda i, *_: (i, 0)),
                  pl.BlockSpec((TD, LANES), lambda i, *_: (i, 0)),
                  pl.BlockSpec((TD, D_MODEL), lambda i, *_: (i, 0)),
                  pl.BlockSpec((1, 1, 6 * D_MODEL), mod_idx),
                  pl.BlockSpec((1, D_MODEL), lambda i, *_: (0, 0)),
                  pl.BlockSpec((1, D_MODEL), lambda i, *_: (0, 0)),
                  pl.BlockSpec(memory_space=pl.ANY)],
        out_specs=out_specs,
        scratch_shapes=[pltpu.VMEM((2, R_TILE, D_MODEL), F32), pltpu.SemaphoreType.DMA((2,))],
    )
    return pl.pallas_call(
        functools.partial(_combine_kernel, alpha, n_p_tiles),
        grid_spec=grid_spec,
        out_shape=out_shape,
        compiler_params=_cparams("arbitrary"),
        name="combine",
    )(tables["a"], tables["off"], tables["seg"], tables["tot"], q, gates, x1, mod, wl["g_ln2"], wl["b_ln2"], ys)


def _routing_tables(counts, idx, lrank, n_blocks):
    sub = TD // TM
    cnt_sub = counts[:, 0, :N_EXPERTS].astype(jnp.int32).reshape(-1, sub, N_EXPERTS)
    before = (jnp.cumsum(cnt_sub, axis=1) - cnt_sub).reshape(-1, N_EXPERTS)
    cnt = jnp.sum(cnt_sub, axis=1)
    a = (cnt + SEG - 1) // SEG * SEG
    off = jnp.cumsum(a, axis=1) - a
    acc = jnp.cumsum(a, axis=0) - a
    tot_e = jnp.sum(a, axis=0)
    padded = (tot_e + MB - 1) // MB * MB
    pend = jnp.cumsum(padded)
    pstart = pend - padded
    seg = pstart[None, :] + acc
    blk_row0 = jnp.arange(n_blocks, dtype=jnp.int32) * MB
    blk_expert = jnp.minimum(jnp.sum((pend[None, :] <= blk_row0[:, None]).astype(jnp.int32), axis=1), N_EXPERTS - 1)
    n_used = (pend[-1:] // MB).astype(jnp.int32)
    row_end = jnp.sum(jnp.where(blk_expert[:, None] == jnp.arange(N_EXPERTS, dtype=jnp.int32),
                                (pstart + tot_e)[None, :], 0), axis=1)
    blk_rows = jnp.clip(row_end - blk_row0, 0, MB).astype(jnp.int32)
    off_tok = jnp.repeat(off, TD, axis=0) + jnp.repeat(before, TM, axis=0)
    hot = idx[:, :TOP_K, None] == jnp.arange(N_EXPERTS, dtype=jnp.int32)
    q = jnp.sum(jnp.where(hot, off_tok[:, None, :], 0), axis=-1) + lrank[:, :TOP_K]
    tables = {"a": a.reshape(-1), "off": off.reshape(-1), "seg": seg.reshape(-1), "tot": jnp.sum(a, axis=1),
              "tail0": pstart + tot_e, "tailn": padded - tot_e}
    return tables, q, blk_expert, n_used, blk_rows


def _pad_cols(w, width):
    return jnp.pad(w, ((0, 0), (0, width - w.shape[1])))


def _hi_lo(w):
    hi = w.astype(BF16)
    return jnp.concatenate([hi, (w - hi.astype(F32)).astype(BF16)], axis=1)


def _rot_cols(w):
    half = w.shape[1] // 2
    return jnp.concatenate([-w[:, half:], w[:, :half]], axis=1)


def _layer_weights(l, w_in, g_q_norm, w_uq, g_kv_norm, w_uk, w_uv, w_sconv, w_cconv, b_cconv,
                   g_cf_norm, b_cf_norm, w_out, g_ln1, b_ln1, g_ln2, b_ln2, w_router, b_router):
    wi = w_in[l]
    o_ckv, o_kr, o_bg = Q_RANK, Q_RANK + KV_RANK, Q_RANK + KV_RANK + QK_ROPE
    w_kr = wi[:, o_kr:o_bg]
    w_in_p = jnp.concatenate([
        _pad_cols(wi[:, :Q_RANK], COL_CKV - COL_CQ),
        wi[:, o_ckv:o_kr],
        _pad_cols(jnp.concatenate([w_kr, _rot_cols(w_kr)], axis=1), COL_BG - COL_KR),
        wi[:, o_bg:]], axis=1).astype(BF16)

    uq = jnp.pad(w_uq[l], ((0, Q_RANK_PAD - Q_RANK), (0, 0))).reshape(Q_RANK_PAD, HEADS, QK_DIM)
    zer = jnp.zeros((Q_RANK_PAD, HEADS, HEAD_PAD - QK_DIM), F32)
    wq_a = jnp.concatenate([uq, zer], axis=2).reshape(Q_RANK_PAD, HEADS * HEAD_PAD).astype(BF16)
    rot = jnp.concatenate([-uq[:, :, QK_NOPE + QK_ROPE // 2:], uq[:, :, QK_NOPE:QK_NOPE + QK_ROPE // 2]], axis=2)
    wq_b = jnp.concatenate([jnp.zeros((Q_RANK_PAD, HEADS, QK_NOPE), F32), rot, zer],
                           axis=2).reshape(Q_RANK_PAD, HEADS * HEAD_PAD).astype(BF16)

    uk_t = w_uk[l].reshape(KV_RANK, HEADS, QK_NOPE).transpose(1, 2, 0)
    top = jnp.pad(uk_t, ((0, 0), (0, 0), (0, KFEAT - KV_RANK)))
    eye = jnp.pad(jnp.eye(QK_ROPE, dtype=F32), ((0, 0), (KV_RANK, KFEAT - KV_RANK - QK_ROPE)))
    mid = jnp.broadcast_to(eye[None], (HEADS, QK_ROPE, KFEAT))
    wk_t = jnp.concatenate([top, mid, jnp.zeros((HEADS, HEAD_PAD - QK_DIM, KFEAT), F32)], axis=1).astype(BF16)

    uv = w_uv[l].reshape(KV_RANK, HEADS, V_DIM)
    zv = jnp.zeros((KV_RANK, HEADS, V_DIM), F32)
    odd = (jnp.arange(HEADS) % 2 == 1)[None, :, None]
    wv = jnp.concatenate([jnp.where(odd, zv, uv), jnp.where(odd, uv, zv)], axis=2).reshape(KV_RANK, HEADS * LANES)
    wv = jnp.pad(wv, ((0, KFEAT - KV_RANK), (0, 0)))
    ones_col = jnp.arange(HEADS) * LANES + jnp.where(jnp.arange(HEADS) % 2 == 1, 0, V_DIM)
    wv = wv.at[ONES_FEAT, ones_col].set(1.0).astype(BF16)

    r1 = lambda v: v.reshape(1, -1)
    return {
        "w_in": w_in_p,
        "g_q": jnp.pad(g_q_norm[l], (0, Q_RANK_PAD - Q_RANK)).reshape(1, -1),
        "g_kv": r1(g_kv_norm[l]),
        "wq_a": wq_a, "wq_b": wq_b, "wk_t": wk_t, "wv": wv,
        "w_sconv": w_sconv[l], "w_cconv": w_cconv[l], "b_cconv": r1(b_cconv[l]),
        "g_cf": r1(g_cf_norm[l]), "b_cf": r1(b_cf_norm[l]),
        "w_out": w_out[l].astype(BF16),
        "g_ln1": r1(g_ln1[l]), "b_ln1": r1(b_ln1[l]), "g_ln2": r1(g_ln2[l]), "b_ln2": r1(b_ln2[l]),
        "w_router": _hi_lo(_pad_cols(w_router[l], LANES)),
        "b_router": jnp.pad(b_router[l], (0, LANES - N_EXPERTS)).reshape(1, -1),
    }


def _rope_tables(n_lat):
    rows = n_lat // GRID_W
    row = jnp.repeat(jnp.arange(rows, dtype=F32), GRID_W)
    col = jnp.tile(jnp.arange(GRID_W, dtype=F32), rows)
    axis_dim = QK_ROPE // 2
    inv = ROPE_BASE ** (-jnp.arange(0, axis_dim, 2, dtype=F32) / axis_dim)
    ang = jnp.concatenate([row[:, None] * inv, col[:, None] * inv], axis=-1)
    cos, sin = jnp.cos(ang), jnp.sin(ang)
    one = jnp.ones((n_lat, QK_NOPE), F32)
    zq = jnp.zeros((n_lat, HEAD_PAD - QK_DIM), F32)
    zk = jnp.zeros((n_lat, LANES - QK_ROPE), F32)
    return (jnp.concatenate([one, cos, cos, zq], axis=1),
            jnp.concatenate([0.0 * one, sin, sin, zq], axis=1),
            jnp.concatenate([cos, cos, zk], axis=1),
            jnp.concatenate([sin, sin, zk], axis=1))


def kernel(x_prompt, x_sample, cache_ckv, cache_krope, c, c_ctx, w_mod, b_mod, w_in, g_q_norm, w_uq, g_kv_norm, w_uk, w_uv, w_sconv, w_cconv, b_cconv, g_cf_norm, b_cf_norm, w_out, g_ln1, b_ln1, g_ln2, b_ln2, w_router, b_router, w_exp_in, b_exp_in, w_exp_out, b_exp_out):
    bp, n_p, d = x_prompt.shape
    bs, n_s, _ = x_sample.shape
    depth = w_mod.shape[0]
    past = cache_ckv.shape[2]
    assert d == D_MODEL and n_p == TM and n_s % TM == 0 and n_s % GRID_W == 0
    t_p, t_s = bp * n_p, bs * n_s
    assert t_p % n_s == 0 and past % LANES == 0
    t = t_p + t_s
    n_p_tiles = t_p // TM
    tps = n_s // TM
    alpha = float((2 * depth) ** 0.25)

    ctx_row = bs
    rows = -(-(bs + 1) // SUBLANES) * SUBLANES
    c_rows = jnp.concatenate([c, c_ctx[None], jnp.zeros((rows - bs - 1, d), F32)], axis=0)
    mod_all = _modulation(c_rows, w_mod, b_mod)

    tabs = _rope_tables(n_s)
    x_p, x_s = x_prompt.reshape(t_p, d), x_sample.reshape(t_s, d)
    assert t_p % TD == 0 and n_s % TD == 0 and TD % TM == 0
    n_blocks = -(-(t * TOP_K + t // TD * N_EXPERTS * (SEG - 1)) // MB) + N_EXPERTS
    p_rows = n_blocks * MB

    new_ckv, new_krope = [], []
    for l in range(depth):
        wl = _layer_weights(l, w_in, g_q_norm, w_uq, g_kv_norm, w_uk, w_uv, w_sconv, w_cconv, b_cconv,
                            g_cf_norm, b_cf_norm, w_out, g_ln1, b_ln1, g_ln2, b_ln2, w_router, b_router)
        mod = mod_all[l].reshape(rows, 1, 6 * d)
        q, ckv, krope, kfeat, conv_in = _in_proj(x_p, x_s, mod, wl, tabs, n_p_tiles, tps, ctx_row)
        new_ckv.append(ckv[:t_p].reshape(bp, n_p, KV_RANK))
        new_krope.append(krope[:t_p].reshape(bp, n_p, QK_ROPE))

        kcache = jnp.concatenate(
            [cache_ckv[:, l], cache_krope[:, l], jnp.ones((bs, past, 1), F32),
             jnp.zeros((bs, past, KFEAT - ONES_FEAT - 1), F32)], axis=-1).astype(BF16)
        attn_p = _attention(q, kfeat, None, wl, bp, n_p, 0)
        attn_s = _attention(q, kfeat, kcache, wl, bs, n_s, t_p)
        x1, h2, idx, lrank, gates, counts = _mix(x_p, x_s, attn_p, attn_s, conv_in, mod, wl, n_p_tiles, tps,
                                                 ctx_row, alpha)
        tables, q, blk_expert, n_used, blk_rows = _routing_tables(counts, idx, lrank, n_blocks)
        xs = _dispatch(tables, n_used, h2, q.T, p_rows)
        ys = _experts(blk_expert, n_used, blk_rows, xs, l, w_exp_in, b_exp_in, w_exp_out, b_exp_out)
        x_p, x_s = _combine(tables, q, gates, x1, mod, ys, wl, t_p // TD, n_s // TD, ctx_row, alpha)

    return (x_p.reshape(bp, n_p, d), x_s.reshape(bs, n_s, d),
            jnp.stack(new_ckv, axis=1), jnp.stack(new_krope, axis=1))
```

```python
import functools
import math

import jax
import jax.numpy as jnp
from jax import lax
from jax.experimental import pallas as pl
from jax.experimental.pallas import tpu as pltpu

F32 = jnp.float32
BF16 = jnp.bfloat16

D_MODEL = 1024
GRID_W = 64
HEADS = 8
QK_NOPE = 64
QK_ROPE = 32
QK_DIM = QK_NOPE + QK_ROPE
V_DIM = 64
Q_RANK = 192
KV_RANK = 128
SC_DIM = 256
SC_WIDTH = 3
CF_DIM = 256
CF_WIDTH = 31
N_EXPERTS = 32
TOP_K = 4
D_EXPERT = 1024
SWIGLU_LIMIT = 7.0
SWIGLU_ALPHA = 1.702
ROPE_BASE = 10000.0
EPS = 1e-6

LANES = 128
SUBLANES = 8

TM = 256
TQ = 512
MB = 512
HALO = 16
SEG = SUBLANES
TI = 512
TD = 512
R_TILE = TD * TOP_K + N_EXPERTS * SEG
MXU_ROWSUM_MIN_KEYS = 2048
DISPATCH_SLOTS = 2
HEAD_PAD = LANES
Q_RANK_PAD = 256
KFEAT = 256
ONES_FEAT = KV_RANK + QK_ROPE
COL_CQ, COL_CKV, COL_KR, COL_BG, COL_CG, COL_XS, COL_GA, COL_GB, IN_COLS_PAD = (
    0, 256, 384, 512, 768, 1024, 1280, 1536, 1792)
VMEM_LIMIT = 60 * 1024 * 1024


def _cparams(*sem):
    return pltpu.CompilerParams(dimension_semantics=sem, vmem_limit_bytes=VMEM_LIMIT)


def _ln_plain(x):
    mu = jnp.mean(x, axis=-1, keepdims=True)
    xc = x - mu
    var = jnp.mean(xc * xc, axis=-1, keepdims=True)
    return xc * lax.rsqrt(var + EPS)


def _mod_kernel(c_ref, w_ref, b_ref, o_ref):
    c = c_ref[...]
    s = c * jax.nn.sigmoid(c)
    o_ref[0] = jnp.dot(s, w_ref[0], preferred_element_type=F32,
                       precision=lax.Precision.HIGHEST) + b_ref[0]


def _modulation(c_rows, w_mod, b_mod):
    depth = w_mod.shape[0]
    rows = c_rows.shape[0]
    nblk = w_mod.shape[2] // D_MODEL
    return pl.pallas_call(
        _mod_kernel,
        grid=(depth, nblk),
        in_specs=[pl.BlockSpec((rows, D_MODEL), lambda l, j: (0, 0)),
                  pl.BlockSpec((1, D_MODEL, D_MODEL), lambda l, j: (l, 0, j)),
                  pl.BlockSpec((1, 1, D_MODEL), lambda l, j: (l, 0, j))],
        out_specs=pl.BlockSpec((1, rows, D_MODEL), lambda l, j: (l, 0, j)),
        out_shape=jax.ShapeDtypeStruct((depth, rows, w_mod.shape[2]), F32),
        compiler_params=_cparams("arbitrary", "arbitrary"),
        name="modulation",
    )(c_rows, w_mod, b_mod.reshape(depth, 1, -1))


def _in_proj_kernel(n_p_tiles, xp_ref, xs_ref, mod_ref, win_ref, gq_ref, gkv_ref, wqa_ref, wqb_ref,
                    tcq_ref, tsq_ref, tck_ref, tsk_ref,
                    q_ref, ckv_ref, kr_ref, kf_ref, cv_ref):
    is_s = pl.program_id(0) >= n_p_tiles
    mod = mod_ref[0]
    x = jnp.where(is_s, xs_ref[...], xp_ref[...])
    h = _ln_plain(x) * (1.0 + mod[:, D_MODEL:2 * D_MODEL]) + mod[:, 0:D_MODEL]
    z = jnp.dot(h.astype(BF16), win_ref[...], preferred_element_type=F32)

    cq = z[:, COL_CQ:COL_CQ + Q_RANK_PAD]
    cqn = cq * lax.rsqrt(jnp.sum(cq * cq, axis=-1, keepdims=True) * (1.0 / Q_RANK) + EPS) * gq_ref[...]
    cqn = cqn.astype(BF16)
    qa = jnp.dot(cqn, wqa_ref[...], preferred_element_type=F32)
    qb = jnp.dot(cqn, wqb_ref[...], preferred_element_type=F32)
    cos_q = jnp.where(is_s, tcq_ref[...], 1.0)
    sin_q = jnp.where(is_s, tsq_ref[...], 0.0)
    scale = math.log2(math.e) / math.sqrt(QK_DIM)
    for hd in range(HEADS):
        sl = slice(hd * HEAD_PAD, (hd + 1) * HEAD_PAD)
        q_ref[hd] = ((qa[:, sl] * cos_q + qb[:, sl] * sin_q) * scale).astype(BF16)

    cr = z[:, COL_CKV:COL_CKV + KV_RANK]
    ckv = cr * lax.rsqrt(jnp.mean(cr * cr, axis=-1, keepdims=True) + EPS) * gkv_ref[...]
    ckv_ref[...] = ckv
    zk = z[:, COL_KR:COL_KR + LANES]
    kr_ref[...] = zk[:, 0:QK_ROPE]
    lane = lax.broadcasted_iota(jnp.int32, zk.shape, 1)
    raw = jnp.where(lane < QK_ROPE, zk, 0.0)
    roped = zk * tck_ref[...] + pltpu.roll(zk, LANES - QK_ROPE, 1) * tsk_ref[...]
    kf_ref[:, 0:KV_RANK] = ckv.astype(BF16)
    kf_ref[:, KV_RANK:KFEAT] = jnp.where(lane == ONES_FEAT - KV_RANK, 1.0, jnp.where(is_s, roped, raw)).astype(BF16)

    cv_ref[:, 0:SC_DIM] = z[:, COL_BG:COL_BG + SC_DIM]
    cv_ref[:, SC_DIM:2 * SC_DIM] = z[:, COL_CG:COL_CG + SC_DIM] * z[:, COL_XS:COL_XS + SC_DIM]
    cv_ref[:, 2 * SC_DIM:] = z[:, COL_GA:COL_GA + CF_DIM] * jax.nn.sigmoid(z[:, COL_GB:COL_GB + CF_DIM])


def _x_specs(n_p_tiles, rows=TM):
    return [pl.BlockSpec((rows, D_MODEL), lambda i, *_: (jnp.minimum(i, n_p_tiles - 1), 0)),
            pl.BlockSpec((rows, D_MODEL), lambda i, *_: (jnp.maximum(i - n_p_tiles, 0), 0))]


def _in_proj(x_p, x_s, mod, wl, tabs, n_p_tiles, tiles_per_s_seq, ctx_row):
    t = x_p.shape[0] + x_s.shape[0]
    nt = t // TI

    def mod_idx(i):
        return (jnp.where(i < n_p_tiles, ctx_row, (i - n_p_tiles) // tiles_per_s_seq), 0, 0)

    def pos_idx(i):
        return (jnp.where(i < n_p_tiles, 0, (i - n_p_tiles) % tiles_per_s_seq), 0)

    full = lambda shp: pl.BlockSpec(shp, lambda i: (0,) * len(shp))
    tab = pl.BlockSpec((TI, LANES), pos_idx)
    return pl.pallas_call(
        functools.partial(_in_proj_kernel, n_p_tiles),
        grid=(nt,),
        in_specs=_x_specs(n_p_tiles, TI) + [
                  pl.BlockSpec((1, 1, 6 * D_MODEL), mod_idx),
                  full((D_MODEL, IN_COLS_PAD)),
                  full((1, Q_RANK_PAD)), full((1, KV_RANK)),
                  full((Q_RANK_PAD, HEADS * HEAD_PAD)), full((Q_RANK_PAD, HEADS * HEAD_PAD)),
                  tab, tab, tab, tab],
        out_specs=[pl.BlockSpec((HEADS, TI, HEAD_PAD), lambda i: (0, i, 0)),
                   pl.BlockSpec((TI, KV_RANK), lambda i: (i, 0)),
                   pl.BlockSpec((TI, QK_ROPE), lambda i: (i, 0)),
                   pl.BlockSpec((TI, KFEAT), lambda i: (i, 0)),
                   pl.BlockSpec((TI, 3 * SC_DIM), lambda i: (i, 0))],
        out_shape=[jax.ShapeDtypeStruct((HEADS, t, HEAD_PAD), BF16),
                   jax.ShapeDtypeStruct((t, KV_RANK), F32),
                   jax.ShapeDtypeStruct((t, QK_ROPE), F32),
                   jax.ShapeDtypeStruct((t, KFEAT), BF16),
                   jax.ShapeDtypeStruct((t, 3 * SC_DIM), F32)],
        compiler_params=_cparams("arbitrary"),
        name="in_proj",
    )(x_p, x_s, mod, wl["w_in"], wl["g_q"], wl["g_kv"], wl["wq_a"], wl["wq_b"], *tabs)


def _attn_kernel(n_own, n_cache, q_ref, kf_ref, *rest):
    if n_cache:
        kc_ref, wkt_ref, wv_ref, o_ref, kt_scr, v_scr = rest
    else:
        wkt_ref, wv_ref, o_ref, kt_scr, v_scr = rest

    @pl.when(pl.program_id(1) == 0)
    def _():
        def fill(feat, lo, n):
            for hd in range(HEADS):
                kt = lax.dot_general(wkt_ref[hd], feat, (((1,), (1,)), ((), ())),
                                     preferred_element_type=F32)
                kt_scr[hd, :, lo:lo + n] = kt.astype(BF16)
                v = jnp.dot(feat, wv_ref[:, hd * LANES:(hd + 1) * LANES], preferred_element_type=F32)
                v_scr[hd, lo:lo + n, :] = v.astype(BF16)

        fill(kf_ref[...], 0, n_own)
        if n_cache:
            fill(kc_ref[0], n_own, n_cache)

    mxu_rowsum = n_own + n_cache >= MXU_ROWSUM_MIN_KEYS
    lane = lax.broadcasted_iota(jnp.int32, o_ref.shape[1:], 1)
    for pair in range(HEADS // 2):
        outs, sums = [], []
        for hd in (2 * pair, 2 * pair + 1):
            s = jnp.dot(q_ref[hd], kt_scr[hd], preferred_element_type=F32)
            p = jnp.exp2(s - jnp.max(s, axis=-1, keepdims=True))
            o = jnp.dot(p.astype(BF16), v_scr[hd], preferred_element_type=F32)
            outs.append(o)
            ones_lane = V_DIM if hd % 2 == 0 else 0
            sums.append(o[:, ones_lane:ones_lane + 1] if mxu_rowsum else jnp.sum(p, axis=-1, keepdims=True))
        o_ref[pair] = jnp.where(lane < V_DIM, outs[0] / sums[0], outs[1] / sums[1]).astype(BF16)


def _attention(q, kfeat, kcache, wl, batch, n_own, row0):
    n_cache = 0 if kcache is None else kcache.shape[1]
    m = n_own + n_cache
    tq = min(TQ, n_own)
    nq = n_own // tq
    q0 = row0 // tq
    k0 = row0 // n_own
    in_specs = [pl.BlockSpec((HEADS, tq, HEAD_PAD), lambda b, j: (0, q0 + b * nq + j, 0)),
                pl.BlockSpec((n_own, KFEAT), lambda b, j: (k0 + b, 0))]
    args = [q, kfeat]
    if n_cache:
        in_specs.append(pl.BlockSpec((1, n_cache, KFEAT), lambda b, j: (b, 0, 0)))
        args.append(kcache)
    in_specs += [pl.BlockSpec((HEADS, HEAD_PAD, KFEAT), lambda b, j: (0, 0, 0)),
                 pl.BlockSpec((KFEAT, HEADS * LANES), lambda b, j: (0, 0))]
    args += [wl["wk_t"], wl["wv"]]
    return pl.pallas_call(
        functools.partial(_attn_kernel, n_own, n_cache),
        grid=(batch, nq),
        in_specs=in_specs,
        out_specs=pl.BlockSpec((HEADS // 2, tq, LANES), lambda b, j: (0, b * nq + j, 0)),
        out_shape=jax.ShapeDtypeStruct((HEADS // 2, batch * n_own, LANES), BF16),
        scratch_shapes=[pltpu.VMEM((HEADS, HEAD_PAD, m), BF16), pltpu.VMEM((HEADS, m, LANES), BF16)],
        compiler_params=_cparams("arbitrary", "arbitrary"),
        name="attention_cache" if n_cache else "attention_ctx",
    )(*args)


def _mix_kernel(n_p_tiles, tiles_per_s_seq, alpha,
                xp_ref, xs_ref, atp_ref, ats_ref, cprev_ref, ccur_ref, cnext_ref, mod_ref,
                wsc_ref, wcc_ref, bcc_ref, gcf_ref, bcf_ref, wout_ref, g1_ref, b1_ref,
                wr_ref, br_ref,
                x1_ref, h2_ref, idx_ref, rank_ref, gate_ref, cnt_ref,
                ext_scr, shf_scr):
    i = pl.program_id(0)
    is_s = i >= n_p_tiles
    jj = (i - n_p_tiles) % tiles_per_s_seq
    has_prev = jnp.logical_and(is_s, jj > 0)
    has_next = jnp.logical_and(is_s, jj < tiles_per_s_seq - 1)

    ext_scr[0:HALO, :] = jnp.where(has_prev, cprev_ref[:, SC_DIM:], 0.0)
    ext_scr[HALO:HALO + TM, :] = ccur_ref[:, SC_DIM:]
    ext_scr[HALO + TM:, :] = jnp.where(has_next, cnext_ref[:, SC_DIM:], 0.0)

    sconv = jnp.zeros((TM, SC_DIM), F32)
    for k in range(SC_WIDTH):
        o = HALO - SC_WIDTH // 2 + k
        sconv = sconv + ext_scr[o:o + TM, 0:SC_DIM] * wsc_ref[k:k + 1, :]
    sc_out = ccur_ref[:, 0:SC_DIM] * sconv

    n_sh = TM + 2 * HALO - SUBLANES
    for s in range(1, SUBLANES):
        shf_scr[s - 1] = ext_scr[s:s + n_sh, SC_DIM:]
    cconv = jnp.zeros((TM, CF_DIM), F32)
    for k in range(CF_WIDTH):
        o = HALO - CF_WIDTH // 2 + k
        s, base = o % SUBLANES, o - o % SUBLANES
        rows = ext_scr[base:base + TM, SC_DIM:] if s == 0 else shf_scr[s - 1, base:base + TM, :]
        cconv = cconv + rows * wcc_ref[k:k + 1, :]
    cf = _ln_plain(cconv + bcc_ref[...]) * gcf_ref[...] + bcf_ref[...]
    cf = cf * jax.nn.sigmoid(cf)

    at = [jnp.where(is_s, ats_ref[p], atp_ref[p]) for p in range(HEADS // 2)]
    mix_in = jnp.concatenate(at + [sc_out.astype(BF16), cf.astype(BF16)], axis=1)
    mix = jnp.dot(mix_in, wout_ref[...], preferred_element_type=F32)

    mod = mod_ref[0]
    g1 = mod[:, 2 * D_MODEL:3 * D_MODEL]
    sh2 = mod[:, 3 * D_MODEL:4 * D_MODEL]
    sc2 = mod[:, 4 * D_MODEL:5 * D_MODEL]
    x = jnp.where(is_s, xs_ref[...], xp_ref[...])
    x1 = _ln_plain(alpha * x + g1 * mix) * g1_ref[...] + b1_ref[...]
    x1_ref[...] = x1
    h2 = _ln_plain(x1) * (1.0 + sc2) + sh2
    h2_hi = h2.astype(BF16)
    h2_ref[...] = h2_hi
    h2_lo = (h2 - h2_hi.astype(F32)).astype(BF16)
    hi_both = jnp.dot(h2_hi, wr_ref[...], preferred_element_type=F32)
    lo_hi = jnp.dot(h2_lo, wr_ref[:, 0:LANES], preferred_element_type=F32)
    logits = hi_both[:, 0:LANES] + hi_both[:, LANES:] + lo_hi + br_ref[...]
    lane = lax.broadcasted_iota(jnp.int32, (TM, LANES), 1)
    lane_f = lane.astype(F32)
    neg = jnp.float32(-jnp.inf)
    cur = jnp.where(lane < N_EXPERTS, logits, neg)
    vals, hots, idxs = [], [], []
    for _ in range(TOP_K):
        mx = jnp.max(cur, axis=-1, keepdims=True)
        ik = jnp.min(jnp.where(cur == mx, lane_f, float(LANES)), axis=-1, keepdims=True)
        ik = jnp.minimum(ik, float(N_EXPERTS - 1))
        hot = lane_f == ik
        cur = jnp.where(hot, neg, cur)
        vals.append(mx)
        hots.append(hot)
        idxs.append(ik.astype(jnp.int32))
    exps = [jnp.exp(v - vals[0]) for v in vals]
    denom = exps[0] + exps[1] + exps[2] + exps[3]
    sel = jnp.where(hots[0] | hots[1] | hots[2] | hots[3], 1.0, 0.0)

    r_i = lax.broadcasted_iota(jnp.int32, (TM, TM), 0)
    c_i = lax.broadcasted_iota(jnp.int32, (TM, TM), 1)
    tri = jnp.where(c_i < r_i, 1.0, 0.0).astype(BF16)
    rank_all = jnp.dot(tri, sel.astype(BF16), preferred_element_type=F32)
    cnt_ref[0] = jnp.broadcast_to(jnp.sum(sel, axis=0, keepdims=True), (SUBLANES, LANES))

    idx_o = jnp.zeros((TM, LANES), jnp.int32)
    rank_o = jnp.zeros((TM, LANES), jnp.int32)
    gate_o = jnp.zeros((TM, LANES), F32)
    for k in range(TOP_K):
        rk = jnp.sum(jnp.where(hots[k], rank_all, 0.0), axis=-1, keepdims=True).astype(jnp.int32)
        idx_o = jnp.where(lane == k, idxs[k], idx_o)
        rank_o = jnp.where(lane == k, rk, rank_o)
        gate_o = jnp.where(lane == k, exps[k] / denom, gate_o)
    idx_ref[...] = idx_o
    rank_ref[...] = rank_o
    gate_ref[...] = gate_o


def _mix(x_p, x_s, attn_p, attn_s, conv_in, mod, wl, n_p_tiles, tiles_per_s_seq, ctx_row, alpha):
    t = x_p.shape[0] + x_s.shape[0]
    nt = t // TM
    hpt = TM // HALO
    n_halo = t // HALO

    def mod_idx(i):
        return (jnp.where(i < n_p_tiles, ctx_row, (i - n_p_tiles) // tiles_per_s_seq), 0, 0)

    full = lambda shp: pl.BlockSpec(shp, lambda i: (0,) * len(shp))
    row = lambda w: pl.BlockSpec((TM, w), lambda i: (i, 0))
    return pl.pallas_call(
        functools.partial(_mix_kernel, n_p_tiles, tiles_per_s_seq, alpha),
        grid=(nt,),
        in_specs=_x_specs(n_p_tiles) + [
                  pl.BlockSpec((HEADS // 2, TM, LANES), lambda i: (0, jnp.minimum(i, n_p_tiles - 1), 0)),
                  pl.BlockSpec((HEADS // 2, TM, LANES), lambda i: (0, jnp.maximum(i - n_p_tiles, 0), 0)),
                  pl.BlockSpec((HALO, 3 * SC_DIM), lambda i: (jnp.maximum(i * hpt - 1, 0), 0)),
                  row(3 * SC_DIM),
                  pl.BlockSpec((HALO, 3 * SC_DIM), lambda i: (jnp.minimum((i + 1) * hpt, n_halo - 1), 0)),
                  pl.BlockSpec((1, 1, 6 * D_MODEL), mod_idx),
                  full((SC_WIDTH, SC_DIM)), full((CF_WIDTH, CF_DIM)), full((1, CF_DIM)),
                  full((1, CF_DIM)), full((1, CF_DIM)),
                  full((D_MODEL, D_MODEL)), full((1, D_MODEL)), full((1, D_MODEL)),
                  full((D_MODEL, 2 * LANES)), full((1, LANES))],
        out_specs=[row(D_MODEL), row(D_MODEL),
                   row(LANES), row(LANES), row(LANES),
                   pl.BlockSpec((1, SUBLANES, LANES), lambda i: (i, 0, 0))],
        out_shape=[jax.ShapeDtypeStruct((t, D_MODEL), F32),
                   jax.ShapeDtypeStruct((t, D_MODEL), BF16),
                   jax.ShapeDtypeStruct((t, LANES), jnp.int32),
                   jax.ShapeDtypeStruct((t, LANES), jnp.int32),
                   jax.ShapeDtypeStruct((t, LANES), F32),
                   jax.ShapeDtypeStruct((nt, SUBLANES, LANES), F32)],
        scratch_shapes=[pltpu.VMEM((TM + 2 * HALO, 2 * SC_DIM), F32),
                        pltpu.VMEM((SUBLANES - 1, TM + 2 * HALO - SUBLANES, CF_DIM), F32)],
        compiler_params=_cparams("arbitrary"),
        name="mix",
    )(x_p, x_s, attn_p, attn_s, conv_in, conv_in, conv_in, mod,
      wl["w_sconv"], wl["w_cconv"], wl["b_cconv"], wl["g_cf"], wl["b_cf"],
      wl["w_out"], wl["g_ln1"], wl["b_ln1"], wl["w_router"], wl["b_router"])


def _segment_copies(n, max_rows, make_copy, start=True):
    pos = jnp.int32(0)
    size = max_rows
    while size >= SEG:
        take = n & size

        @pl.when(take != 0)
        def _(pos=pos, size=size):
            cp = make_copy(pos, size)
            cp.start() if start else cp.wait()

        pos = pos + take
        size //= 2


def _dispatch_kernel(a_ref, off_ref, seg_ref, tot_ref, tail0_ref, tailn_ref, nu_ref,
                     h_ref, qt_ref, xs_hbm, buf, zbuf, sem, zsem):
    i = pl.program_id(0)
    last = pl.num_programs(0) - 1
    n_slots = buf.shape[0]
    slot = i % n_slots

    r = lax.broadcasted_iota(jnp.int32, (R_TILE, TD), 0)
    hit = r == qt_ref[0:1, :]
    for k in range(1, TOP_K):
        hit = jnp.logical_or(hit, r == qt_ref[k:k + 1, :])
    perm = jnp.where(hit, 1.0, 0.0).astype(BF16)
    buf[slot] = jnp.dot(perm, h_ref[...], preferred_element_type=F32)

    def wait_tile(j, s):
        n = pl.multiple_of(tot_ref[j], SEG)

        @pl.when(n > 0)
        def _():
            pltpu.make_async_copy(buf.at[s, pl.ds(0, n)], xs_hbm.at[pl.ds(0, n)], sem.at[s]).wait()

    @pl.when(i >= n_slots - 1)
    def _():
        wait_tile(i + 1 - n_slots, (i + 1) % n_slots)

    def per_expert(e, carry):
        n = a_ref[i * N_EXPERTS + e]
        src = off_ref[i * N_EXPERTS + e]
        dst = seg_ref[i * N_EXPERTS + e]
        _segment_copies(n, TD, lambda pos, size: pltpu.make_async_copy(
            buf.at[slot, pl.ds(pl.multiple_of(src + pos, SEG), size)],
            xs_hbm.at[pl.ds(pl.multiple_of(dst + pos, SEG), size)], sem.at[slot]))
        return carry

    lax.fori_loop(0, N_EXPERTS, per_expert, 0)

    @pl.when(i == last)
    def _():
        zbuf[...] = jnp.zeros_like(zbuf)

        def tail(start):
            def per_tail(e, carry):
                dst = tail0_ref[e]
                _segment_copies(tailn_ref[e], MB // 2, lambda pos, size: pltpu.make_async_copy(
                    zbuf.at[pl.ds(0, size)], xs_hbm.at[pl.ds(pl.multiple_of(dst + pos, SEG), size)], zsem),
                    start=start)
                return carry
            lax.fori_loop(0, N_EXPERTS, per_tail, 0)

        def spare(b):
            return pltpu.make_async_copy(zbuf, xs_hbm.at[pl.ds(pl.multiple_of(b * MB, MB), MB)], zsem)

        def start_spare(b, carry):
            spare(b).start()
            return carry

        def wait_spare(b, carry):
            spare(b).wait()
            return carry

        n_blocks = xs_hbm.shape[0] // MB
        tail(True)
        lax.fori_loop(nu_ref[0], n_blocks, start_spare, 0)
        tail(False)
        lax.fori_loop(nu_ref[0], n_blocks, wait_spare, 0)
        for back in range(n_slots - 1):
            @pl.when(i >= back)
            def _(back=back):
                wait_tile(i - back, (i - back) % n_slots)


def _dispatch(tables, n_used, h2, q_t, p_rows):
    t = h2.shape[0]
    grid_spec = pltpu.PrefetchScalarGridSpec(
        num_scalar_prefetch=7,
        grid=(t // TD,),
        in_specs=[pl.BlockSpec((TD, D_MODEL), lambda i, *_: (i, 0)),
                  pl.BlockSpec((TOP_K, TD), lambda i, *_: (0, i))],
        out_specs=pl.BlockSpec(memory_space=pl.ANY),
        scratch_shapes=[pltpu.VMEM((DISPATCH_SLOTS, R_TILE, D_MODEL), F32), pltpu.VMEM((MB, D_MODEL), F32),
                        pltpu.SemaphoreType.DMA((DISPATCH_SLOTS,)), pltpu.SemaphoreType.DMA(())],
    )
    return pl.pallas_call(
        _dispatch_kernel,
        grid_spec=grid_spec,
        out_shape=jax.ShapeDtypeStruct((p_rows, D_MODEL), F32),
        compiler_params=_cparams("arbitrary"),
        name="dispatch",
    )(tables["a"], tables["off"], tables["seg"], tables["tot"], tables["tail0"], tables["tailn"], n_used, h2, q_t)


def _expert_kernel(be_ref, nu_ref, xs_ref, w1_ref, b1_ref, w2_ref, b2_ref, ys_ref, w1_scr, w2_scr):
    i = pl.program_id(0)
    used = i < nu_ref[0]

    @pl.when(jnp.logical_and(used, jnp.logical_or(i == 0, be_ref[i] != be_ref[jnp.maximum(i - 1, 0)])))
    def _():
        w1_scr[...] = w1_ref[0, 0].astype(BF16)
        w2_scr[...] = w2_ref[0, 0].astype(BF16)

    @pl.when(used)
    def _():
        z = jnp.dot(xs_ref[...].astype(BF16), w1_scr[...], preferred_element_type=F32) + b1_ref[0, 0]
        gate = jnp.minimum(z[:, :D_EXPERT], SWIGLU_LIMIT)
        up = jnp.clip(z[:, D_EXPERT:], -SWIGLU_LIMIT, SWIGLU_LIMIT)
        act = (up + 1.0) * gate * jax.nn.sigmoid(SWIGLU_ALPHA * gate)
        ys_ref[...] = jnp.dot(act.astype(BF16), w2_scr[...], preferred_element_type=F32) + b2_ref[0, 0]

    @pl.when(jnp.logical_not(used))
    def _():
        ys_ref[...] = jnp.zeros_like(ys_ref)


def _experts(blk_expert, n_used, xs, l, w_exp_in, b_exp_in, w_exp_out, b_exp_out):
    p_rows = xs.shape[0]
    nb = p_rows // MB
    depth = w_exp_in.shape[0]
    blk = lambda i, nu: jnp.maximum(jnp.minimum(i, nu[0] - 1), 0)
    wmap = lambda i, be, nu: (l, be[blk(i, nu)], 0, 0)
    xmap = lambda i, be, nu: (blk(i, nu), 0)
    grid_spec = pltpu.PrefetchScalarGridSpec(
        num_scalar_prefetch=2,
        grid=(nb,),
        in_specs=[pl.BlockSpec((MB, D_MODEL), xmap),
                  pl.BlockSpec((1, 1, D_MODEL, 2 * D_EXPERT), wmap),
                  pl.BlockSpec((1, 1, 1, 2 * D_EXPERT), wmap),
                  pl.BlockSpec((1, 1, D_EXPERT, D_MODEL), wmap),
                  pl.BlockSpec((1, 1, 1, D_MODEL), wmap)],
        out_specs=pl.BlockSpec((MB, D_MODEL), lambda i, be, nu: (i, 0)),
        scratch_shapes=[pltpu.VMEM((D_MODEL, 2 * D_EXPERT), BF16), pltpu.VMEM((D_EXPERT, D_MODEL), BF16)],
    )
    return pl.pallas_call(
        _expert_kernel,
        grid_spec=grid_spec,
        out_shape=jax.ShapeDtypeStruct((p_rows, D_MODEL), F32),
        compiler_params=_cparams("arbitrary"),
        name="experts",
    )(blk_expert, n_used, xs, w_exp_in, b_exp_in.reshape(depth, N_EXPERTS, 1, -1),
      w_exp_out, b_exp_out.reshape(depth, N_EXPERTS, 1, -1))


def _combine_kernel(alpha, n_p_tiles, a_ref, off_ref, seg_ref, tot_ref,
                    q_ref, gate_ref, x1_ref, mod_ref, g2_ref, b2_ref, ys_hbm, o_p_ref, o_s_ref, buf, sem):
    i = pl.program_id(0)
    slot = i % 2

    def fetch(j, s):
        def per_expert(e, carry):
            n = a_ref[j * N_EXPERTS + e]
            src = seg_ref[j * N_EXPERTS + e]
            dst = off_ref[j * N_EXPERTS + e]
            _segment_copies(n, TD, lambda pos, size: pltpu.make_async_copy(
                ys_hbm.at[pl.ds(pl.multiple_of(src + pos, SEG), size)],
                buf.at[s, pl.ds(pl.multiple_of(dst + pos, SEG), size)], sem.at[s]))
            return carry

        lax.fori_loop(0, N_EXPERTS, per_expert, 0)

    @pl.when(i == 0)
    def _():
        buf[...] = jnp.zeros_like(buf)
        fetch(0, 0)

    @pl.when(i + 1 < pl.num_programs(0))
    def _():
        fetch(i + 1, 1 - slot)

    n = pl.multiple_of(tot_ref[i], SEG)

    @pl.when(n > 0)
    def _():
        pltpu.make_async_copy(ys_hbm.at[pl.ds(0, n)], buf.at[slot, pl.ds(0, n)], sem.at[slot]).wait()

    lane = lax.broadcasted_iota(jnp.int32, (TD, R_TILE), 1)
    q = q_ref[...]
    gates = gate_ref[...]
    g = jnp.zeros((TD, R_TILE), F32)
    for k in range(TOP_K):
        g = jnp.where(lane == q[:, k:k + 1], gates[:, k:k + 1], g)
    ffn = jnp.dot(g.astype(BF16), buf[slot].astype(BF16), preferred_element_type=F32)
    g2 = mod_ref[0][:, 5 * D_MODEL:6 * D_MODEL]
    res = _ln_plain(alpha * x1_ref[...] + g2 * ffn) * g2_ref[...] + b2_ref[...]
    @pl.when(i < n_p_tiles)
    def _():
        o_p_ref[...] = res

    @pl.when(i >= n_p_tiles)
    def _():
        o_s_ref[...] = res


def _combine(tables, q, gates, x1, mod, ys, wl, n_p_tiles, tiles_per_s_seq, ctx_row, alpha):
    t = x1.shape[0]
    nt = t // TD

    def mod_idx(i, *_):
        return (jnp.where(i < n_p_tiles, ctx_row, (i - n_p_tiles) // tiles_per_s_seq), 0, 0)

    out_specs = _x_specs(n_p_tiles, TD)
    out_shape = [jax.ShapeDtypeStruct((n_p_tiles * TD, D_MODEL), F32),
                 jax.ShapeDtypeStruct((t - n_p_tiles * TD, D_MODEL), F32)]
    grid_spec = pltpu.PrefetchScalarGridSpec(
        num_scalar_prefetch=4,
        grid=(nt,),
        in_specs=[pl.BlockSpec((TD, TOP_K), lambda i, *_: (i, 0)),
                  pl.BlockSpec((TD, LANES), lambda i, *_: (i, 0)),
                  pl.BlockSpec((TD, D_MODEL), lambda i, *_: (i, 0)),
                  pl.BlockSpec((1, 1, 6 * D_MODEL), mod_idx),
                  pl.BlockSpec((1, D_MODEL), lambda i, *_: (0, 0)),
                  pl.BlockSpec((1, D_MODEL), lambda i, *_: (0, 0)),
                  pl.BlockSpec(memory_space=pl.ANY)],
        out_specs=out_specs,
        scratch_shapes=[pltpu.VMEM((2, R_TILE, D_MODEL), F32), pltpu.SemaphoreType.DMA((2,))],
    )
    return pl.pallas_call(
        functools.partial(_combine_kernel, alpha, n_p_tiles),
        grid_spec=grid_spec,
        out_shape=out_shape,
        compiler_params=_cparams("arbitrary"),
        name="combine",
    )(tables["a"], tables["off"], tables["seg"], tables["tot"], q, gates, x1, mod, wl["g_ln2"], wl["b_ln2"], ys)


def _routing_tables(counts, idx, lrank, n_blocks):
    sub = TD // TM
    cnt_sub = counts[:, 0, :N_EXPERTS].astype(jnp.int32).reshape(-1, sub, N_EXPERTS)
    before = (jnp.cumsum(cnt_sub, axis=1) - cnt_sub).reshape(-1, N_EXPERTS)
    cnt = jnp.sum(cnt_sub, axis=1)
    a = (cnt + SEG - 1) // SEG * SEG
    off = jnp.cumsum(a, axis=1) - a
    acc = jnp.cumsum(a, axis=0) - a
    tot_e = jnp.sum(a, axis=0)
    padded = (tot_e + MB - 1) // MB * MB
    pend = jnp.cumsum(padded)
    pstart = pend - padded
    seg = pstart[None, :] + acc
    blk_row0 = jnp.arange(n_blocks, dtype=jnp.int32) * MB
    blk_expert = jnp.minimum(jnp.sum((pend[None, :] <= blk_row0[:, None]).astype(jnp.int32), axis=1), N_EXPERTS - 1)
    n_used = (pend[-1:] // MB).astype(jnp.int32)
    off_tok = jnp.repeat(off, TD, axis=0) + jnp.repeat(before, TM, axis=0)
    hot = idx[:, :TOP_K, None] == jnp.arange(N_EXPERTS, dtype=jnp.int32)
    q = jnp.sum(jnp.where(hot, off_tok[:, None, :], 0), axis=-1) + lrank[:, :TOP_K]
    tables = {"a": a.reshape(-1), "off": off.reshape(-1), "seg": seg.reshape(-1), "tot": jnp.sum(a, axis=1),
              "tail0": pstart + tot_e, "tailn": padded - tot_e}
    return tables, q, blk_expert, n_used


def _pad_cols(w, width):
    return jnp.pad(w, ((0, 0), (0, width - w.shape[1])))


def _hi_lo(w):
    hi = w.astype(BF16)
    return jnp.concatenate([hi, (w - hi.astype(F32)).astype(BF16)], axis=1)


def _rot_cols(w):
    half = w.shape[1] // 2
    return jnp.concatenate([-w[:, half:], w[:, :half]], axis=1)


def _layer_weights(l, w_in, g_q_norm, w_uq, g_kv_norm, w_uk, w_uv, w_sconv, w_cconv, b_cconv,
                   g_cf_norm, b_cf_norm, w_out, g_ln1, b_ln1, g_ln2, b_ln2, w_router, b_router):
    wi = w_in[l]
    o_ckv, o_kr, o_bg = Q_RANK, Q_RANK + KV_RANK, Q_RANK + KV_RANK + QK_ROPE
    w_kr = wi[:, o_kr:o_bg]
    w_in_p = jnp.concatenate([
        _pad_cols(wi[:, :Q_RANK], COL_CKV - COL_CQ),
        wi[:, o_ckv:o_kr],
        _pad_cols(jnp.concatenate([w_kr, _rot_cols(w_kr)], axis=1), COL_BG - COL_KR),
        wi[:, o_bg:]], axis=1).astype(BF16)

    uq = jnp.pad(w_uq[l], ((0, Q_RANK_PAD - Q_RANK), (0, 0))).reshape(Q_RANK_PAD, HEADS, QK_DIM)
    zer = jnp.zeros((Q_RANK_PAD, HEADS, HEAD_PAD - QK_DIM), F32)
    wq_a = jnp.concatenate([uq, zer], axis=2).reshape(Q_RANK_PAD, HEADS * HEAD_PAD).astype(BF16)
    rot = jnp.concatenate([-uq[:, :, QK_NOPE + QK_ROPE // 2:], uq[:, :, QK_NOPE:QK_NOPE + QK_ROPE // 2]], axis=2)
    wq_b = jnp.concatenate([jnp.zeros((Q_RANK_PAD, HEADS, QK_NOPE), F32), rot, zer],
                           axis=2).reshape(Q_RANK_PAD, HEADS * HEAD_PAD).astype(BF16)

    uk_t = w_uk[l].reshape(KV_RANK, HEADS, QK_NOPE).transpose(1, 2, 0)
    top = jnp.pad(uk_t, ((0, 0), (0, 0), (0, KFEAT - KV_RANK)))
    eye = jnp.pad(jnp.eye(QK_ROPE, dtype=F32), ((0, 0), (KV_RANK, KFEAT - KV_RANK - QK_ROPE)))
    mid = jnp.broadcast_to(eye[None], (HEADS, QK_ROPE, KFEAT))
    wk_t = jnp.concatenate([top, mid, jnp.zeros((HEADS, HEAD_PAD - QK_DIM, KFEAT), F32)], axis=1).astype(BF16)

    uv = w_uv[l].reshape(KV_RANK, HEADS, V_DIM)
    zv = jnp.zeros((KV_RANK, HEADS, V_DIM), F32)
    odd = (jnp.arange(HEADS) % 2 == 1)[None, :, None]
    wv = jnp.concatenate([jnp.where(odd, zv, uv), jnp.where(odd, uv, zv)], axis=2).reshape(KV_RANK, HEADS * LANES)
    wv = jnp.pad(wv, ((0, KFEAT - KV_RANK), (0, 0)))
    ones_col = jnp.arange(HEADS) * LANES + jnp.where(jnp.arange(HEADS) % 2 == 1, 0, V_DIM)
    wv = wv.at[ONES_FEAT, ones_col].set(1.0).astype(BF16)

    r1 = lambda v: v.reshape(1, -1)
    return {
        "w_in": w_in_p,
        "g_q": jnp.pad(g_q_norm[l], (0, Q_RANK_PAD - Q_RANK)).reshape(1, -1),
        "g_kv": r1(g_kv_norm[l]),
        "wq_a": wq_a, "wq_b": wq_b, "wk_t": wk_t, "wv": wv,
        "w_sconv": w_sconv[l], "w_cconv": w_cconv[l], "b_cconv": r1(b_cconv[l]),
        "g_cf": r1(g_cf_norm[l]), "b_cf": r1(b_cf_norm[l]),
        "w_out": w_out[l].astype(BF16),
        "g_ln1": r1(g_ln1[l]), "b_ln1": r1(b_ln1[l]), "g_ln2": r1(g_ln2[l]), "b_ln2": r1(b_ln2[l]),
        "w_router": _hi_lo(_pad_cols(w_router[l], LANES)),
        "b_router": jnp.pad(b_router[l], (0, LANES - N_EXPERTS)).reshape(1, -1),
    }


def _rope_tables(n_lat):
    rows = n_lat // GRID_W
    row = jnp.repeat(jnp.arange(rows, dtype=F32), GRID_W)
    col = jnp.tile(jnp.arange(GRID_W, dtype=F32), rows)
    axis_dim = QK_ROPE // 2
    inv = ROPE_BASE ** (-jnp.arange(0, axis_dim, 2, dtype=F32) / axis_dim)
    ang = jnp.concatenate([row[:, None] * inv, col[:, None] * inv], axis=-1)
    cos, sin = jnp.cos(ang), jnp.sin(ang)
    one = jnp.ones((n_lat, QK_NOPE), F32)
    zq = jnp.zeros((n_lat, HEAD_PAD - QK_DIM), F32)
    zk = jnp.zeros((n_lat, LANES - QK_ROPE), F32)
    return (jnp.concatenate([one, cos, cos, zq], axis=1),
            jnp.concatenate([0.0 * one, sin, sin, zq], axis=1),
            jnp.concatenate([cos, cos, zk], axis=1),
            jnp.concatenate([sin, sin, zk], axis=1))


def kernel(x_prompt, x_sample, cache_ckv, cache_krope, c, c_ctx, w_mod, b_mod, w_in, g_q_norm, w_uq, g_kv_norm, w_uk, w_uv, w_sconv, w_cconv, b_cconv, g_cf_norm, b_cf_norm, w_out, g_ln1, b_ln1, g_ln2, b_ln2, w_router, b_router, w_exp_in, b_exp_in, w_exp_out, b_exp_out):
    bp, n_p, d = x_prompt.shape
    bs, n_s, _ = x_sample.shape
    depth = w_mod.shape[0]
    past = cache_ckv.shape[2]
    assert d == D_MODEL and n_p == TM and n_s % TM == 0 and n_s % GRID_W == 0
    t_p, t_s = bp * n_p, bs * n_s
    assert t_p % n_s == 0 and past % LANES == 0
    t = t_p + t_s
    n_p_tiles = t_p // TM
    tps = n_s // TM
    alpha = float((2 * depth) ** 0.25)

    ctx_row = bs
    rows = -(-(bs + 1) // SUBLANES) * SUBLANES
    c_rows = jnp.concatenate([c, c_ctx[None], jnp.zeros((rows - bs - 1, d), F32)], axis=0)
    mod_all = _modulation(c_rows, w_mod, b_mod)

    tabs = _rope_tables(n_s)
    x_p, x_s = x_prompt.reshape(t_p, d), x_sample.reshape(t_s, d)
    assert t_p % TD == 0 and n_s % TD == 0 and TD % TM == 0 and t_p % TI == 0 and n_s % TI == 0
    n_blocks = -(-(t * TOP_K + t // TD * N_EXPERTS * (SEG - 1)) // MB) + N_EXPERTS
    p_rows = n_blocks * MB

    new_ckv, new_krope = [], []
    for l in range(depth):
        wl = _layer_weights(l, w_in, g_q_norm, w_uq, g_kv_norm, w_uk, w_uv, w_sconv, w_cconv, b_cconv,
                            g_cf_norm, b_cf_norm, w_out, g_ln1, b_ln1, g_ln2, b_ln2, w_router, b_router)
        mod = mod_all[l].reshape(rows, 1, 6 * d)
        q, ckv, krope, kfeat, conv_in = _in_proj(x_p, x_s, mod, wl, tabs, t_p // TI, n_s // TI, ctx_row)
        new_ckv.append(ckv[:t_p].reshape(bp, n_p, KV_RANK))
        new_krope.append(krope[:t_p].reshape(bp, n_p, QK_ROPE))

        kcache = jnp.concatenate(
            [cache_ckv[:, l], cache_krope[:, l], jnp.ones((bs, past, 1), F32),
             jnp.zeros((bs, past, KFEAT - ONES_FEAT - 1), F32)], axis=-1).astype(BF16)
        attn_p = _attention(q, kfeat, None, wl, bp, n_p, 0)
        attn_s = _attention(q, kfeat, kcache, wl, bs, n_s, t_p)
        x1, h2, idx, lrank, gates, counts = _mix(x_p, x_s, attn_p, attn_s, conv_in, mod, wl, n_p_tiles, tps,
                                                 ctx_row, alpha)
        tables, q, blk_expert, n_used = _routing_tables(counts, idx, lrank, n_blocks)
        xs = _dispatch(tables, n_used, h2, q.T, p_rows)
        ys = _experts(blk_expert, n_used, xs, l, w_exp_in, b_exp_in, w_exp_out, b_exp_out)
        x_p, x_s = _combine(tables, q, gates, x1, mod, ys, wl, t_p // TD, n_s // TD, ctx_row, alpha)

    return (x_p.reshape(bp, n_p, d), x_s.reshape(bs, n_s, d),
            jnp.stack(new_ckv, axis=1), jnp.stack(new_krope, axis=1))
```

```python
import functools
import math

import jax
import jax.numpy as jnp
from jax import lax
from jax.experimental import pallas as pl
from jax.experimental.pallas import tpu as pltpu

F32 = jnp.float32
BF16 = jnp.bfloat16

D_MODEL = 1024
GRID_W = 64
HEADS = 8
QK_NOPE = 64
QK_ROPE = 32
QK_DIM = QK_NOPE + QK_ROPE
V_DIM = 64
Q_RANK = 192
KV_RANK = 128
SC_DIM = 256
SC_WIDTH = 3
CF_DIM = 256
CF_WIDTH = 31
N_EXPERTS = 32
TOP_K = 4
D_EXPERT = 1024
SWIGLU_LIMIT = 7.0
SWIGLU_ALPHA = 1.702
ROPE_BASE = 10000.0
EPS = 1e-6

LANES = 128
SUBLANES = 8

TM = 256
TQ = 512
MB = 512
HALO = 16
SEG = SUBLANES
TI = 1024
TD = 512
R_TILE = TD * TOP_K + N_EXPERTS * SEG
MXU_ROWSUM_MIN_KEYS = 2048
DISPATCH_SLOTS = 2
HEAD_PAD = LANES
Q_RANK_PAD = 256
KFEAT = 256
ONES_FEAT = KV_RANK + QK_ROPE
COL_CQ, COL_CKV, COL_KR, COL_BG, COL_CG, COL_XS, COL_GA, COL_GB, IN_COLS_PAD = (
    0, 256, 384, 512, 768, 1024, 1280, 1536, 1792)
VMEM_LIMIT = 60 * 1024 * 1024


def _cparams(*sem):
    return pltpu.CompilerParams(dimension_semantics=sem, vmem_limit_bytes=VMEM_LIMIT)


def _ln_plain(x):
    mu = jnp.mean(x, axis=-1, keepdims=True)
    xc = x - mu
    var = jnp.mean(xc * xc, axis=-1, keepdims=True)
    return xc * lax.rsqrt(var + EPS)


def _mod_kernel(c_ref, w_ref, b_ref, o_ref):
    c = c_ref[...]
    s = c * jax.nn.sigmoid(c)
    o_ref[0] = jnp.dot(s, w_ref[0], preferred_element_type=F32,
                       precision=lax.Precision.HIGHEST) + b_ref[0]


def _modulation(c_rows, w_mod, b_mod):
    depth = w_mod.shape[0]
    rows = c_rows.shape[0]
    nblk = w_mod.shape[2] // D_MODEL
    return pl.pallas_call(
        _mod_kernel,
        grid=(depth, nblk),
        in_specs=[pl.BlockSpec((rows, D_MODEL), lambda l, j: (0, 0)),
                  pl.BlockSpec((1, D_MODEL, D_MODEL), lambda l, j: (l, 0, j)),
                  pl.BlockSpec((1, 1, D_MODEL), lambda l, j: (l, 0, j))],
        out_specs=pl.BlockSpec((1, rows, D_MODEL), lambda l, j: (l, 0, j)),
        out_shape=jax.ShapeDtypeStruct((depth, rows, w_mod.shape[2]), F32),
        compiler_params=_cparams("arbitrary", "arbitrary"),
        name="modulation",
    )(c_rows, w_mod, b_mod.reshape(depth, 1, -1))


def _in_proj_kernel(n_p_tiles, xp_ref, xs_ref, mod_ref, win_ref, gq_ref, gkv_ref, wqa_ref, wqb_ref,
                    tcq_ref, tsq_ref, tck_ref, tsk_ref,
                    q_ref, ckv_ref, kr_ref, kf_ref, cv_ref):
    is_s = pl.program_id(0) >= n_p_tiles
    mod = mod_ref[0]
    x = jnp.where(is_s, xs_ref[...], xp_ref[...])
    h = _ln_plain(x) * (1.0 + mod[:, D_MODEL:2 * D_MODEL]) + mod[:, 0:D_MODEL]
    z = jnp.dot(h.astype(BF16), win_ref[...], preferred_element_type=F32)

    cq = z[:, COL_CQ:COL_CQ + Q_RANK_PAD]
    cqn = cq * lax.rsqrt(jnp.sum(cq * cq, axis=-1, keepdims=True) * (1.0 / Q_RANK) + EPS) * gq_ref[...]
    cqn = cqn.astype(BF16)
    qa = jnp.dot(cqn, wqa_ref[...], preferred_element_type=F32)
    qb = jnp.dot(cqn, wqb_ref[...], preferred_element_type=F32)
    cos_q = jnp.where(is_s, tcq_ref[...], 1.0)
    sin_q = jnp.where(is_s, tsq_ref[...], 0.0)
    scale = math.log2(math.e) / math.sqrt(QK_DIM)
    for hd in range(HEADS):
        sl = slice(hd * HEAD_PAD, (hd + 1) * HEAD_PAD)
        q_ref[hd] = ((qa[:, sl] * cos_q + qb[:, sl] * sin_q) * scale).astype(BF16)

    cr = z[:, COL_CKV:COL_CKV + KV_RANK]
    ckv = cr * lax.rsqrt(jnp.mean(cr * cr, axis=-1, keepdims=True) + EPS) * gkv_ref[...]
    ckv_ref[...] = ckv
    zk = z[:, COL_KR:COL_KR + LANES]
    kr_ref[...] = zk[:, 0:QK_ROPE]
    lane = lax.broadcasted_iota(jnp.int32, zk.shape, 1)
    raw = jnp.where(lane < QK_ROPE, zk, 0.0)
    roped = zk * tck_ref[...] + pltpu.roll(zk, LANES - QK_ROPE, 1) * tsk_ref[...]
    kf_ref[:, 0:KV_RANK] = ckv.astype(BF16)
    kf_ref[:, KV_RANK:KFEAT] = jnp.where(lane == ONES_FEAT - KV_RANK, 1.0, jnp.where(is_s, roped, raw)).astype(BF16)

    cv_ref[:, 0:SC_DIM] = z[:, COL_BG:COL_BG + SC_DIM]
    cv_ref[:, SC_DIM:2 * SC_DIM] = z[:, COL_CG:COL_CG + SC_DIM] * z[:, COL_XS:COL_XS + SC_DIM]
    cv_ref[:, 2 * SC_DIM:] = z[:, COL_GA:COL_GA + CF_DIM] * jax.nn.sigmoid(z[:, COL_GB:COL_GB + CF_DIM])


def _x_specs(n_p_tiles, rows=TM):
    return [pl.BlockSpec((rows, D_MODEL), lambda i, *_: (jnp.minimum(i, n_p_tiles - 1), 0)),
            pl.BlockSpec((rows, D_MODEL), lambda i, *_: (jnp.maximum(i - n_p_tiles, 0), 0))]


def _in_proj(x_p, x_s, mod, wl, tabs, n_p_tiles, tiles_per_s_seq, ctx_row):
    t = x_p.shape[0] + x_s.shape[0]
    nt = t // TI

    def mod_idx(i):
        return (jnp.where(i < n_p_tiles, ctx_row, (i - n_p_tiles) // tiles_per_s_seq), 0, 0)

    def pos_idx(i):
        return (jnp.where(i < n_p_tiles, 0, (i - n_p_tiles) % tiles_per_s_seq), 0)

    full = lambda shp: pl.BlockSpec(shp, lambda i: (0,) * len(shp))
    tab = pl.BlockSpec((TI, LANES), pos_idx)
    return pl.pallas_call(
        functools.partial(_in_proj_kernel, n_p_tiles),
        grid=(nt,),
        in_specs=_x_specs(n_p_tiles, TI) + [
                  pl.BlockSpec((1, 1, 6 * D_MODEL), mod_idx),
                  full((D_MODEL, IN_COLS_PAD)),
                  full((1, Q_RANK_PAD)), full((1, KV_RANK)),
                  full((Q_RANK_PAD, HEADS * HEAD_PAD)), full((Q_RANK_PAD, HEADS * HEAD_PAD)),
                  tab, tab, tab, tab],
        out_specs=[pl.BlockSpec((HEADS, TI, HEAD_PAD), lambda i: (0, i, 0)),
                   pl.BlockSpec((TI, KV_RANK), lambda i: (i, 0)),
                   pl.BlockSpec((TI, QK_ROPE), lambda i: (i, 0)),
                   pl.BlockSpec((TI, KFEAT), lambda i: (i, 0)),
                   pl.BlockSpec((TI, 3 * SC_DIM), lambda i: (i, 0))],
        out_shape=[jax.ShapeDtypeStruct((HEADS, t, HEAD_PAD), BF16),
                   jax.ShapeDtypeStruct((t, KV_RANK), F32),
                   jax.ShapeDtypeStruct((t, QK_ROPE), F32),
                   jax.ShapeDtypeStruct((t, KFEAT), BF16),
                   jax.ShapeDtypeStruct((t, 3 * SC_DIM), F32)],
        compiler_params=_cparams("arbitrary"),
        name="in_proj",
    )(x_p, x_s, mod, wl["w_in"], wl["g_q"], wl["g_kv"], wl["wq_a"], wl["wq_b"], *tabs)


def _attn_kernel(n_own, n_cache, q_ref, kf_ref, *rest):
    if n_cache:
        kc_ref, wkt_ref, wv_ref, o_ref, kt_scr, v_scr = rest
    else:
        wkt_ref, wv_ref, o_ref, kt_scr, v_scr = rest

    @pl.when(pl.program_id(1) == 0)
    def _():
        def fill(feat, lo, n):
            for hd in range(HEADS):
                kt = lax.dot_general(wkt_ref[hd], feat, (((1,), (1,)), ((), ())),
                                     preferred_element_type=F32)
                kt_scr[hd, :, lo:lo + n] = kt.astype(BF16)
                v = jnp.dot(feat, wv_ref[:, hd * LANES:(hd + 1) * LANES], preferred_element_type=F32)
                v_scr[hd, lo:lo + n, :] = v.astype(BF16)

        fill(kf_ref[...], 0, n_own)
        if n_cache:
            fill(kc_ref[0], n_own, n_cache)

    mxu_rowsum = n_own + n_cache >= MXU_ROWSUM_MIN_KEYS
    lane = lax.broadcasted_iota(jnp.int32, o_ref.shape[1:], 1)
    for pair in range(HEADS // 2):
        outs, sums = [], []
        for hd in (2 * pair, 2 * pair + 1):
            s = jnp.dot(q_ref[hd], kt_scr[hd], preferred_element_type=F32)
            p = jnp.exp2(s - jnp.max(s, axis=-1, keepdims=True))
            o = jnp.dot(p.astype(BF16), v_scr[hd], preferred_element_type=F32)
            outs.append(o)
            ones_lane = V_DIM if hd % 2 == 0 else 0
            sums.append(o[:, ones_lane:ones_lane + 1] if mxu_rowsum else jnp.sum(p, axis=-1, keepdims=True))
        o_ref[pair] = jnp.where(lane < V_DIM, outs[0] / sums[0], outs[1] / sums[1]).astype(BF16)


def _attention(q, kfeat, kcache, wl, batch, n_own, row0):
    n_cache = 0 if kcache is None else kcache.shape[1]
    m = n_own + n_cache
    tq = min(TQ, n_own)
    nq = n_own // tq
    q0 = row0 // tq
    k0 = row0 // n_own
    in_specs = [pl.BlockSpec((HEADS, tq, HEAD_PAD), lambda b, j: (0, q0 + b * nq + j, 0)),
                pl.BlockSpec((n_own, KFEAT), lambda b, j: (k0 + b, 0))]
    args = [q, kfeat]
    if n_cache:
        in_specs.append(pl.BlockSpec((1, n_cache, KFEAT), lambda b, j: (b, 0, 0)))
        args.append(kcache)
    in_specs += [pl.BlockSpec((HEADS, HEAD_PAD, KFEAT), lambda b, j: (0, 0, 0)),
                 pl.BlockSpec((KFEAT, HEADS * LANES), lambda b, j: (0, 0))]
    args += [wl["wk_t"], wl["wv"]]
    return pl.pallas_call(
        functools.partial(_attn_kernel, n_own, n_cache),
        grid=(batch, nq),
        in_specs=in_specs,
        out_specs=pl.BlockSpec((HEADS // 2, tq, LANES), lambda b, j: (0, b * nq + j, 0)),
        out_shape=jax.ShapeDtypeStruct((HEADS // 2, batch * n_own, LANES), BF16),
        scratch_shapes=[pltpu.VMEM((HEADS, HEAD_PAD, m), BF16), pltpu.VMEM((HEADS, m, LANES), BF16)],
        compiler_params=_cparams("arbitrary", "arbitrary"),
        name="attention_cache" if n_cache else "attention_ctx",
    )(*args)


def _mix_kernel(n_p_tiles, tiles_per_s_seq, alpha,
                xp_ref, xs_ref, atp_ref, ats_ref, cprev_ref, ccur_ref, cnext_ref, mod_ref,
                wsc_ref, wcc_ref, bcc_ref, gcf_ref, bcf_ref, wout_ref, g1_ref, b1_ref,
                wr_ref, br_ref,
                x1_ref, h2_ref, idx_ref, rank_ref, gate_ref, cnt_ref,
                ext_scr, shf_scr):
    i = pl.program_id(0)
    is_s = i >= n_p_tiles
    jj = (i - n_p_tiles) % tiles_per_s_seq
    has_prev = jnp.logical_and(is_s, jj > 0)
    has_next = jnp.logical_and(is_s, jj < tiles_per_s_seq - 1)

    ext_scr[0:HALO, :] = jnp.where(has_prev, cprev_ref[:, SC_DIM:], 0.0)
    ext_scr[HALO:HALO + TM, :] = ccur_ref[:, SC_DIM:]
    ext_scr[HALO + TM:, :] = jnp.where(has_next, cnext_ref[:, SC_DIM:], 0.0)

    sconv = jnp.zeros((TM, SC_DIM), F32)
    for k in range(SC_WIDTH):
        o = HALO - SC_WIDTH // 2 + k
        sconv = sconv + ext_scr[o:o + TM, 0:SC_DIM] * wsc_ref[k:k + 1, :]
    sc_out = ccur_ref[:, 0:SC_DIM] * sconv

    n_sh = TM + 2 * HALO - SUBLANES
    for s in range(1, SUBLANES):
        shf_scr[s - 1] = ext_scr[s:s + n_sh, SC_DIM:]
    cconv = jnp.zeros((TM, CF_DIM), F32)
    for k in range(CF_WIDTH):
        o = HALO - CF_WIDTH // 2 + k
        s, base = o % SUBLANES, o - o % SUBLANES
        rows = ext_scr[base:base + TM, SC_DIM:] if s == 0 else shf_scr[s - 1, base:base + TM, :]
        cconv = cconv + rows * wcc_ref[k:k + 1, :]
    cf = _ln_plain(cconv + bcc_ref[...]) * gcf_ref[...] + bcf_ref[...]
    cf = cf * jax.nn.sigmoid(cf)

    at = [jnp.where(is_s, ats_ref[p], atp_ref[p]) for p in range(HEADS // 2)]
    mix_in = jnp.concatenate(at + [sc_out.astype(BF16), cf.astype(BF16)], axis=1)
    mix = jnp.dot(mix_in, wout_ref[...], preferred_element_type=F32)

    mod = mod_ref[0]
    g1 = mod[:, 2 * D_MODEL:3 * D_MODEL]
    sh2 = mod[:, 3 * D_MODEL:4 * D_MODEL]
    sc2 = mod[:, 4 * D_MODEL:5 * D_MODEL]
    x = jnp.where(is_s, xs_ref[...], xp_ref[...])
    x1 = _ln_plain(alpha * x + g1 * mix) * g1_ref[...] + b1_ref[...]
    x1_ref[...] = x1
    h2 = _ln_plain(x1) * (1.0 + sc2) + sh2
    h2_hi = h2.astype(BF16)
    h2_ref[...] = h2_hi
    h2_lo = (h2 - h2_hi.astype(F32)).astype(BF16)
    hi_both = jnp.dot(h2_hi, wr_ref[...], preferred_element_type=F32)
    lo_hi = jnp.dot(h2_lo, wr_ref[:, 0:LANES], preferred_element_type=F32)
    logits = hi_both[:, 0:LANES] + hi_both[:, LANES:] + lo_hi + br_ref[...]
    lane = lax.broadcasted_iota(jnp.int32, (TM, LANES), 1)
    lane_f = lane.astype(F32)
    neg = jnp.float32(-jnp.inf)
    cur = jnp.where(lane < N_EXPERTS, logits, neg)
    vals, hots, idxs = [], [], []
    for _ in range(TOP_K):
        mx = jnp.max(cur, axis=-1, keepdims=True)
        ik = jnp.min(jnp.where(cur == mx, lane_f, float(LANES)), axis=-1, keepdims=True)
        ik = jnp.minimum(ik, float(N_EXPERTS - 1))
        hot = lane_f == ik
        cur = jnp.where(hot, neg, cur)
        vals.append(mx)
        hots.append(hot)
        idxs.append(ik.astype(jnp.int32))
    exps = [jnp.exp(v - vals[0]) for v in vals]
    denom = exps[0] + exps[1] + exps[2] + exps[3]
    sel = jnp.where(hots[0] | hots[1] | hots[2] | hots[3], 1.0, 0.0)

    r_i = lax.broadcasted_iota(jnp.int32, (TM, TM), 0)
    c_i = lax.broadcasted_iota(jnp.int32, (TM, TM), 1)
    tri = jnp.where(c_i < r_i, 1.0, 0.0).astype(BF16)
    rank_all = jnp.dot(tri, sel.astype(BF16), preferred_element_type=F32)
    cnt_ref[0] = jnp.broadcast_to(jnp.sum(sel, axis=0, keepdims=True), (SUBLANES, LANES))

    idx_o = jnp.zeros((TM, LANES), jnp.int32)
    rank_o = jnp.zeros((TM, LANES), jnp.int32)
    gate_o = jnp.zeros((TM, LANES), F32)
    for k in range(TOP_K):
        rk = jnp.sum(jnp.where(hots[k], rank_all, 0.0), axis=-1, keepdims=True).astype(jnp.int32)
        idx_o = jnp.where(lane == k, idxs[k], idx_o)
        rank_o = jnp.where(lane == k, rk, rank_o)
        gate_o = jnp.where(lane == k, exps[k] / denom, gate_o)
    idx_ref[...] = idx_o
    rank_ref[...] = rank_o
    gate_ref[...] = gate_o


def _mix(x_p, x_s, attn_p, attn_s, conv_in, mod, wl, n_p_tiles, tiles_per_s_seq, ctx_row, alpha):
    t = x_p.shape[0] + x_s.shape[0]
    nt = t // TM
    hpt = TM // HALO
    n_halo = t // HALO

    def mod_idx(i):
        return (jnp.where(i < n_p_tiles, ctx_row, (i - n_p_tiles) // tiles_per_s_seq), 0, 0)

    full = lambda shp: pl.BlockSpec(shp, lambda i: (0,) * len(shp))
    row = lambda w: pl.BlockSpec((TM, w), lambda i: (i, 0))
    return pl.pallas_call(
        functools.partial(_mix_kernel, n_p_tiles, tiles_per_s_seq, alpha),
        grid=(nt,),
        in_specs=_x_specs(n_p_tiles) + [
                  pl.BlockSpec((HEADS // 2, TM, LANES), lambda i: (0, jnp.minimum(i, n_p_tiles - 1), 0)),
                  pl.BlockSpec((HEADS // 2, TM, LANES), lambda i: (0, jnp.maximum(i - n_p_tiles, 0), 0)),
                  pl.BlockSpec((HALO, 3 * SC_DIM), lambda i: (jnp.maximum(i * hpt - 1, 0), 0)),
                  row(3 * SC_DIM),
                  pl.BlockSpec((HALO, 3 * SC_DIM), lambda i: (jnp.minimum((i + 1) * hpt, n_halo - 1), 0)),
                  pl.BlockSpec((1, 1, 6 * D_MODEL), mod_idx),
                  full((SC_WIDTH, SC_DIM)), full((CF_WIDTH, CF_DIM)), full((1, CF_DIM)),
                  full((1, CF_DIM)), full((1, CF_DIM)),
                  full((D_MODEL, D_MODEL)), full((1, D_MODEL)), full((1, D_MODEL)),
                  full((D_MODEL, 2 * LANES)), full((1, LANES))],
        out_specs=[row(D_MODEL), row(D_MODEL),
                   row(LANES), row(LANES), row(LANES),
                   pl.BlockSpec((1, SUBLANES, LANES), lambda i: (i, 0, 0))],
        out_shape=[jax.ShapeDtypeStruct((t, D_MODEL), F32),
                   jax.ShapeDtypeStruct((t, D_MODEL), BF16),
                   jax.ShapeDtypeStruct((t, LANES), jnp.int32),
                   jax.ShapeDtypeStruct((t, LANES), jnp.int32),
                   jax.ShapeDtypeStruct((t, LANES), F32),
                   jax.ShapeDtypeStruct((nt, SUBLANES, LANES), F32)],
        scratch_shapes=[pltpu.VMEM((TM + 2 * HALO, 2 * SC_DIM), F32),
                        pltpu.VMEM((SUBLANES - 1, TM + 2 * HALO - SUBLANES, CF_DIM), F32)],
        compiler_params=_cparams("arbitrary"),
        name="mix",
    )(x_p, x_s, attn_p, attn_s, conv_in, conv_in, conv_in, mod,
      wl["w_sconv"], wl["w_cconv"], wl["b_cconv"], wl["g_cf"], wl["b_cf"],
      wl["w_out"], wl["g_ln1"], wl["b_ln1"], wl["w_router"], wl["b_router"])


def _segment_copies(n, max_rows, make_copy, start=True):
    pos = jnp.int32(0)
    size = max_rows
    while size >= SEG:
        take = n & size

        @pl.when(take != 0)
        def _(pos=pos, size=size):
            cp = make_copy(pos, size)
            cp.start() if start else cp.wait()

        pos = pos + take
        size //= 2


def _dispatch_kernel(a_ref, off_ref, seg_ref, tot_ref, tail0_ref, tailn_ref, nu_ref,
                     h_ref, qt_ref, xs_hbm, buf, zbuf, sem, zsem):
    i = pl.program_id(0)
    last = pl.num_programs(0) - 1
    n_slots = buf.shape[0]
    slot = i % n_slots

    r = lax.broadcasted_iota(jnp.int32, (R_TILE, TD), 0)
    hit = r == qt_ref[0:1, :]
    for k in range(1, TOP_K):
        hit = jnp.logical_or(hit, r == qt_ref[k:k + 1, :])
    perm = jnp.where(hit, 1.0, 0.0).astype(BF16)
    buf[slot] = jnp.dot(perm, h_ref[...], preferred_element_type=F32)

    def wait_tile(j, s):
        n = pl.multiple_of(tot_ref[j], SEG)

        @pl.when(n > 0)
        def _():
            pltpu.make_async_copy(buf.at[s, pl.ds(0, n)], xs_hbm.at[pl.ds(0, n)], sem.at[s]).wait()

    @pl.when(i >= n_slots - 1)
    def _():
        wait_tile(i + 1 - n_slots, (i + 1) % n_slots)

    def per_expert(e, carry):
        n = a_ref[i * N_EXPERTS + e]
        src = off_ref[i * N_EXPERTS + e]
        dst = seg_ref[i * N_EXPERTS + e]
        _segment_copies(n, TD, lambda pos, size: pltpu.make_async_copy(
            buf.at[slot, pl.ds(pl.multiple_of(src + pos, SEG), size)],
            xs_hbm.at[pl.ds(pl.multiple_of(dst + pos, SEG), size)], sem.at[slot]))
        return carry

    lax.fori_loop(0, N_EXPERTS, per_expert, 0)

    @pl.when(i == last)
    def _():
        zbuf[...] = jnp.zeros_like(zbuf)

        def tail(start):
            def per_tail(e, carry):
                dst = tail0_ref[e]
                _segment_copies(tailn_ref[e], MB // 2, lambda pos, size: pltpu.make_async_copy(
                    zbuf.at[pl.ds(0, size)], xs_hbm.at[pl.ds(pl.multiple_of(dst + pos, SEG), size)], zsem),
                    start=start)
                return carry
            lax.fori_loop(0, N_EXPERTS, per_tail, 0)

        def spare(b):
            return pltpu.make_async_copy(zbuf, xs_hbm.at[pl.ds(pl.multiple_of(b * MB, MB), MB)], zsem)

        def start_spare(b, carry):
            spare(b).start()
            return carry

        def wait_spare(b, carry):
            spare(b).wait()
            return carry

        n_blocks = xs_hbm.shape[0] // MB
        tail(True)
        lax.fori_loop(nu_ref[0], n_blocks, start_spare, 0)
        tail(False)
        lax.fori_loop(nu_ref[0], n_blocks, wait_spare, 0)
        for back in range(n_slots - 1):
            @pl.when(i >= back)
            def _(back=back):
                wait_tile(i - back, (i - back) % n_slots)


def _dispatch(tables, n_used, h2, q_t, p_rows):
    t = h2.shape[0]
    grid_spec = pltpu.PrefetchScalarGridSpec(
        num_scalar_prefetch=7,
        grid=(t // TD,),
        in_specs=[pl.BlockSpec((TD, D_MODEL), lambda i, *_: (i, 0)),
                  pl.BlockSpec((TOP_K, TD), lambda i, *_: (0, i))],
        out_specs=pl.BlockSpec(memory_space=pl.ANY),
        scratch_shapes=[pltpu.VMEM((DISPATCH_SLOTS, R_TILE, D_MODEL), F32), pltpu.VMEM((MB, D_MODEL), F32),
                        pltpu.SemaphoreType.DMA((DISPATCH_SLOTS,)), pltpu.SemaphoreType.DMA(())],
    )
    return pl.pallas_call(
        _dispatch_kernel,
        grid_spec=grid_spec,
        out_shape=jax.ShapeDtypeStruct((p_rows, D_MODEL), F32),
        compiler_params=_cparams("arbitrary"),
        name="dispatch",
    )(tables["a"], tables["off"], tables["seg"], tables["tot"], tables["tail0"], tables["tailn"], n_used, h2, q_t)


def _expert_kernel(be_ref, nu_ref, xs_ref, w1_ref, b1_ref, w2_ref, b2_ref, ys_ref, w1_scr, w2_scr):
    i = pl.program_id(0)
    used = i < nu_ref[0]

    @pl.when(jnp.logical_and(used, jnp.logical_or(i == 0, be_ref[i] != be_ref[jnp.maximum(i - 1, 0)])))
    def _():
        w1_scr[...] = w1_ref[0, 0].astype(BF16)
        w2_scr[...] = w2_ref[0, 0].astype(BF16)

    @pl.when(used)
    def _():
        z = jnp.dot(xs_ref[...].astype(BF16), w1_scr[...], preferred_element_type=F32) + b1_ref[0, 0]
        gate = jnp.minimum(z[:, :D_EXPERT], SWIGLU_LIMIT)
        up = jnp.clip(z[:, D_EXPERT:], -SWIGLU_LIMIT, SWIGLU_LIMIT)
        act = (up + 1.0) * gate * jax.nn.sigmoid(SWIGLU_ALPHA * gate)
        ys_ref[...] = jnp.dot(act.astype(BF16), w2_scr[...], preferred_element_type=F32) + b2_ref[0, 0]

    @pl.when(jnp.logical_not(used))
    def _():
        ys_ref[...] = jnp.zeros_like(ys_ref)


def _experts(blk_expert, n_used, xs, l, w_exp_in, b_exp_in, w_exp_out, b_exp_out):
    p_rows = xs.shape[0]
    nb = p_rows // MB
    depth = w_exp_in.shape[0]
    blk = lambda i, nu: jnp.maximum(jnp.minimum(i, nu[0] - 1), 0)
    wmap = lambda i, be, nu: (l, be[blk(i, nu)], 0, 0)
    xmap = lambda i, be, nu: (blk(i, nu), 0)
    grid_spec = pltpu.PrefetchScalarGridSpec(
        num_scalar_prefetch=2,
        grid=(nb,),
        in_specs=[pl.BlockSpec((MB, D_MODEL), xmap),
                  pl.BlockSpec((1, 1, D_MODEL, 2 * D_EXPERT), wmap),
                  pl.BlockSpec((1, 1, 1, 2 * D_EXPERT), wmap),
                  pl.BlockSpec((1, 1, D_EXPERT, D_MODEL), wmap),
                  pl.BlockSpec((1, 1, 1, D_MODEL), wmap)],
        out_specs=pl.BlockSpec((MB, D_MODEL), lambda i, be, nu: (i, 0)),
        scratch_shapes=[pltpu.VMEM((D_MODEL, 2 * D_EXPERT), BF16), pltpu.VMEM((D_EXPERT, D_MODEL), BF16)],
    )
    return pl.pallas_call(
        _expert_kernel,
        grid_spec=grid_spec,
        out_shape=jax.ShapeDtypeStruct((p_rows, D_MODEL), F32),
        compiler_params=_cparams("arbitrary"),
        name="experts",
    )(blk_expert, n_used, xs, w_exp_in, b_exp_in.reshape(depth, N_EXPERTS, 1, -1),
      w_exp_out, b_exp_out.reshape(depth, N_EXPERTS, 1, -1))


def _combine_kernel(alpha, n_p_tiles, a_ref, off_ref, seg_ref, tot_ref,
                    q_ref, gate_ref, x1_ref, mod_ref, g2_ref, b2_ref, ys_hbm, o_p_ref, o_s_ref, buf, sem):
    i = pl.program_id(0)
    slot = i % 2

    def fetch(j, s):
        def per_expert(e, carry):
            n = a_ref[j * N_EXPERTS + e]
            src = seg_ref[j * N_EXPERTS + e]
            dst = off_ref[j * N_EXPERTS + e]
            _segment_copies(n, TD, lambda pos, size: pltpu.make_async_copy(
                ys_hbm.at[pl.ds(pl.multiple_of(src + pos, SEG), size)],
                buf.at[s, pl.ds(pl.multiple_of(dst + pos, SEG), size)], sem.at[s]))
            return carry

        lax.fori_loop(0, N_EXPERTS, per_expert, 0)

    @pl.when(i == 0)
    def _():
        buf[...] = jnp.zeros_like(buf)
        fetch(0, 0)

    @pl.when(i + 1 < pl.num_programs(0))
    def _():
        fetch(i + 1, 1 - slot)

    n = pl.multiple_of(tot_ref[i], SEG)

    @pl.when(n > 0)
    def _():
        pltpu.make_async_copy(ys_hbm.at[pl.ds(0, n)], buf.at[slot, pl.ds(0, n)], sem.at[slot]).wait()

    lane = lax.broadcasted_iota(jnp.int32, (TD, R_TILE), 1)
    q = q_ref[...]
    gates = gate_ref[...]
    g = jnp.zeros((TD, R_TILE), F32)
    for k in range(TOP_K):
        g = jnp.where(lane == q[:, k:k + 1], gates[:, k:k + 1], g)
    ffn = jnp.dot(g.astype(BF16), buf[slot].astype(BF16), preferred_element_type=F32)
    g2 = mod_ref[0][:, 5 * D_MODEL:6 * D_MODEL]
    res = _ln_plain(alpha * x1_ref[...] + g2 * ffn) * g2_ref[...] + b2_ref[...]
    @pl.when(i < n_p_tiles)
    def _():
        o_p_ref[...] = res

    @pl.when(i >= n_p_tiles)
    def _():
        o_s_ref[...] = res


def _combine(tables, q, gates, x1, mod, ys, wl, n_p_tiles, tiles_per_s_seq, ctx_row, alpha):
    t = x1.shape[0]
    nt = t // TD

    def mod_idx(i, *_):
        return (jnp.where(i < n_p_tiles, ctx_row, (i - n_p_tiles) // tiles_per_s_seq), 0, 0)

    out_specs = _x_specs(n_p_tiles, TD)
    out_shape = [jax.ShapeDtypeStruct((n_p_tiles * TD, D_MODEL), F32),
                 jax.ShapeDtypeStruct((t - n_p_tiles * TD, D_MODEL), F32)]
    grid_spec = pltpu.PrefetchScalarGridSpec(
        num_scalar_prefetch=4,
        grid=(nt,),
        in_specs=[pl.BlockSpec((TD, TOP_K), lambda i, *_: (i, 0)),
                  pl.BlockSpec((TD, LANES), lambda i, *_: (i, 0)),
                  pl.BlockSpec((TD, D_MODEL), lambda i, *_: (i, 0)),
                  pl.BlockSpec((1, 1, 6 * D_MODEL), mod_idx),
                  pl.BlockSpec((1, D_MODEL), lambda i, *_: (0, 0)),
                  pl.BlockSpec((1, D_MODEL), lambda i, *_: (0, 0)),
                  pl.BlockSpec(memory_space=pl.ANY)],
        out_specs=out_specs,
        scratch_shapes=[pltpu.VMEM((2, R_TILE, D_MODEL), F32), pltpu.SemaphoreType.DMA((2,))],
    )
    return pl.pallas_call(
        functools.partial(_combine_kernel, alpha, n_p_tiles),
        grid_spec=grid_spec,
        out_shape=out_shape,
        compiler_params=_cparams("arbitrary"),
        name="combine",
    )(tables["a"], tables["off"], tables["seg"], tables["tot"], q, gates, x1, mod, wl["g_ln2"], wl["b_ln2"], ys)


def _routing_tables(counts, idx, lrank, n_blocks):
    sub = TD // TM
    cnt_sub = counts[:, 0, :N_EXPERTS].astype(jnp.int32).reshape(-1, sub, N_EXPERTS)
    before = (jnp.cumsum(cnt_sub, axis=1) - cnt_sub).reshape(-1, N_EXPERTS)
    cnt = jnp.sum(cnt_sub, axis=1)
    a = (cnt + SEG - 1) // SEG * SEG
    off = jnp.cumsum(a, axis=1) - a
    acc = jnp.cumsum(a, axis=0) - a
    tot_e = jnp.sum(a, axis=0)
    padded = (tot_e + MB - 1) // MB * MB
    pend = jnp.cumsum(padded)
    pstart = pend - padded
    seg = pstart[None, :] + acc
    blk_row0 = jnp.arange(n_blocks, dtype=jnp.int32) * MB
    blk_expert = jnp.minimum(jnp.sum((pend[None, :] <= blk_row0[:, None]).astype(jnp.int32), axis=1), N_EXPERTS - 1)
    n_used = (pend[-1:] // MB).astype(jnp.int32)
    off_tok = jnp.repeat(off, TD, axis=0) + jnp.repeat(before, TM, axis=0)
    hot = idx[:, :TOP_K, None] == jnp.arange(N_EXPERTS, dtype=jnp.int32)
    q = jnp.sum(jnp.where(hot, off_tok[:, None, :], 0), axis=-1) + lrank[:, :TOP_K]
    tables = {"a": a.reshape(-1), "off": off.reshape(-1), "seg": seg.reshape(-1), "tot": jnp.sum(a, axis=1),
              "tail0": pstart + tot_e, "tailn": padded - tot_e}
    return tables, q, blk_expert, n_used


def _pad_cols(w, width):
    return jnp.pad(w, ((0, 0), (0, width - w.shape[1])))


def _hi_lo(w):
    hi = w.astype(BF16)
    return jnp.concatenate([hi, (w - hi.astype(F32)).astype(BF16)], axis=1)


def _rot_cols(w):
    half = w.shape[1] // 2
    return jnp.concatenate([-w[:, half:], w[:, :half]], axis=1)


def _layer_weights(l, w_in, g_q_norm, w_uq, g_kv_norm, w_uk, w_uv, w_sconv, w_cconv, b_cconv,
                   g_cf_norm, b_cf_norm, w_out, g_ln1, b_ln1, g_ln2, b_ln2, w_router, b_router):
    wi = w_in[l]
    o_ckv, o_kr, o_bg = Q_RANK, Q_RANK + KV_RANK, Q_RANK + KV_RANK + QK_ROPE
    w_kr = wi[:, o_kr:o_bg]
    w_in_p = jnp.concatenate([
        _pad_cols(wi[:, :Q_RANK], COL_CKV - COL_CQ),
        wi[:, o_ckv:o_kr],
        _pad_cols(jnp.concatenate([w_kr, _rot_cols(w_kr)], axis=1), COL_BG - COL_KR),
        wi[:, o_bg:]], axis=1).astype(BF16)

    uq = jnp.pad(w_uq[l], ((0, Q_RANK_PAD - Q_RANK), (0, 0))).reshape(Q_RANK_PAD, HEADS, QK_DIM)
    zer = jnp.zeros((Q_RANK_PAD, HEADS, HEAD_PAD - QK_DIM), F32)
    wq_a = jnp.concatenate([uq, zer], axis=2).reshape(Q_RANK_PAD, HEADS * HEAD_PAD).astype(BF16)
    rot = jnp.concatenate([-uq[:, :, QK_NOPE + QK_ROPE // 2:], uq[:, :, QK_NOPE:QK_NOPE + QK_ROPE // 2]], axis=2)
    wq_b = jnp.concatenate([jnp.zeros((Q_RANK_PAD, HEADS, QK_NOPE), F32), rot, zer],
                           axis=2).reshape(Q_RANK_PAD, HEADS * HEAD_PAD).astype(BF16)

    uk_t = w_uk[l].reshape(KV_RANK, HEADS, QK_NOPE).transpose(1, 2, 0)
    top = jnp.pad(uk_t, ((0, 0), (0, 0), (0, KFEAT - KV_RANK)))
    eye = jnp.pad(jnp.eye(QK_ROPE, dtype=F32), ((0, 0), (KV_RANK, KFEAT - KV_RANK - QK_ROPE)))
    mid = jnp.broadcast_to(eye[None], (HEADS, QK_ROPE, KFEAT))
    wk_t = jnp.concatenate([top, mid, jnp.zeros((HEADS, HEAD_PAD - QK_DIM, KFEAT), F32)], axis=1).astype(BF16)

    uv = w_uv[l].reshape(KV_RANK, HEADS, V_DIM)
    zv = jnp.zeros((KV_RANK, HEADS, V_DIM), F32)
    odd = (jnp.arange(HEADS) % 2 == 1)[None, :, None]
    wv = jnp.concatenate([jnp.where(odd, zv, uv), jnp.where(odd, uv, zv)], axis=2).reshape(KV_RANK, HEADS * LANES)
    wv = jnp.pad(wv, ((0, KFEAT - KV_RANK), (0, 0)))
    ones_col = jnp.arange(HEADS) * LANES + jnp.where(jnp.arange(HEADS) % 2 == 1, 0, V_DIM)
    wv = wv.at[ONES_FEAT, ones_col].set(1.0).astype(BF16)

    r1 = lambda v: v.reshape(1, -1)
    return {
        "w_in": w_in_p,
        "g_q": jnp.pad(g_q_norm[l], (0, Q_RANK_PAD - Q_RANK)).reshape(1, -1),
        "g_kv": r1(g_kv_norm[l]),
        "wq_a": wq_a, "wq_b": wq_b, "wk_t": wk_t, "wv": wv,
        "w_sconv": w_sconv[l], "w_cconv": w_cconv[l], "b_cconv": r1(b_cconv[l]),
        "g_cf": r1(g_cf_norm[l]), "b_cf": r1(b_cf_norm[l]),
        "w_out": w_out[l].astype(BF16),
        "g_ln1": r1(g_ln1[l]), "b_ln1": r1(b_ln1[l]), "g_ln2": r1(g_ln2[l]), "b_ln2": r1(b_ln2[l]),
        "w_router": _hi_lo(_pad_cols(w_router[l], LANES)),
        "b_router": jnp.pad(b_router[l], (0, LANES - N_EXPERTS)).reshape(1, -1),
    }


def _rope_tables(n_lat):
    rows = n_lat // GRID_W
    row = jnp.repeat(jnp.arange(rows, dtype=F32), GRID_W)
    col = jnp.tile(jnp.arange(GRID_W, dtype=F32), rows)
    axis_dim = QK_ROPE // 2
    inv = ROPE_BASE ** (-jnp.arange(0, axis_dim, 2, dtype=F32) / axis_dim)
    ang = jnp.concatenate([row[:, None] * inv, col[:, None] * inv], axis=-1)
    cos, sin = jnp.cos(ang), jnp.sin(ang)
    one = jnp.ones((n_lat, QK_NOPE), F32)
    zq = jnp.zeros((n_lat, HEAD_PAD - QK_DIM), F32)
    zk = jnp.zeros((n_lat, LANES - QK_ROPE), F32)
    return (jnp.concatenate([one, cos, cos, zq], axis=1),
            jnp.concatenate([0.0 * one, sin, sin, zq], axis=1),
            jnp.concatenate([cos, cos, zk], axis=1),
            jnp.concatenate([sin, sin, zk], axis=1))


def kernel(x_prompt, x_sample, cache_ckv, cache_krope, c, c_ctx, w_mod, b_mod, w_in, g_q_norm, w_uq, g_kv_norm, w_uk, w_uv, w_sconv, w_cconv, b_cconv, g_cf_norm, b_cf_norm, w_out, g_ln1, b_ln1, g_ln2, b_ln2, w_router, b_router, w_exp_in, b_exp_in, w_exp_out, b_exp_out):
    bp, n_p, d = x_prompt.shape
    bs, n_s, _ = x_sample.shape
    depth = w_mod.shape[0]
    past = cache_ckv.shape[2]
    assert d == D_MODEL and n_p == TM and n_s % TM == 0 and n_s % GRID_W == 0
    t_p, t_s = bp * n_p, bs * n_s
    assert t_p % n_s == 0 and past % LANES == 0
    t = t_p + t_s
    n_p_tiles = t_p // TM
    tps = n_s // TM
    alpha = float((2 * depth) ** 0.25)

    ctx_row = bs
    rows = -(-(bs + 1) // SUBLANES) * SUBLANES
    c_rows = jnp.concatenate([c, c_ctx[None], jnp.zeros((rows - bs - 1, d), F32)], axis=0)
    mod_all = _modulation(c_rows, w_mod, b_mod)

    tabs = _rope_tables(n_s)
    x_p, x_s = x_prompt.reshape(t_p, d), x_sample.reshape(t_s, d)
    assert t_p % TD == 0 and n_s % TD == 0 and TD % TM == 0 and t_p % TI == 0 and n_s % TI == 0
    n_blocks = -(-(t * TOP_K + t // TD * N_EXPERTS * (SEG - 1)) // MB) + N_EXPERTS
    p_rows = n_blocks * MB

    new_ckv, new_krope = [], []
    for l in range(depth):
        wl = _layer_weights(l, w_in, g_q_norm, w_uq, g_kv_norm, w_uk, w_uv, w_sconv, w_cconv, b_cconv,
                            g_cf_norm, b_cf_norm, w_out, g_ln1, b_ln1, g_ln2, b_ln2, w_router, b_router)
        mod = mod_all[l].reshape(rows, 1, 6 * d)
        q, ckv, krope, kfeat, conv_in = _in_proj(x_p, x_s, mod, wl, tabs, t_p // TI, n_s // TI, ctx_row)
        new_ckv.append(ckv[:t_p].reshape(bp, n_p, KV_RANK))
        new_krope.append(krope[:t_p].reshape(bp, n_p, QK_ROPE))

        kcache = jnp.concatenate(
            [cache_ckv[:, l], cache_krope[:, l], jnp.ones((bs, past, 1), F32),
             jnp.zeros((bs, past, KFEAT - ONES_FEAT - 1), F32)], axis=-1).astype(BF16)
        attn_p = _attention(q, kfeat, None, wl, bp, n_p, 0)
        attn_s = _attention(q, kfeat, kcache, wl, bs, n_s, t_p)
        x1, h2, idx, lrank, gates, counts = _mix(x_p, x_s, attn_p, attn_s, conv_in, mod, wl, n_p_tiles, tps,
                                                 ctx_row, alpha)
        tables, q, blk_expert, n_used = _routing_tables(counts, idx, lrank, n_blocks)
        xs = _dispatch(tables, n_used, h2, q.T, p_rows)
        ys = _experts(blk_expert, n_used, xs, l, w_exp_in, b_exp_in, w_exp_out, b_exp_out)
        x_p, x_s = _combine(tables, q, gates, x1, mod, ys, wl, t_p // TD, n_s // TD, ctx_row, alpha)

    return (x_p.reshape(bp, n_p, d), x_s.reshape(bs, n_s, d),
            jnp.stack(new_ckv, axis=1), jnp.stack(new_krope, axis=1))
```
